```python
import jax
import jax.numpy as jnp
from jax import lax
import numpy as np

D_MODEL = 1024
BATCH = 8
SEQ = 4096
DEPTH = 2

HEAD_DIM = 64
N_HEADS = 4
MIX_WIDTH = N_HEADS * HEAD_DIM
N_BRANCH = 4
NORM_EPS = 1e-6
MASK_VALUE = -1e30

SB_QBLOCK = 128

RWKV_DECAY_LORA = 64
RWKV_AAA_LORA = 64
RWKV_MV_LORA = 32
RWKV_GATE_LORA = 128
RWKV_GN_EPS = 64e-5

MOBA_BLOCK = 256
MOBA_TOPK = 3
MOBA_QCHUNK = 32
ROPE_THETA = 500000.0
ROPE_DIM = HEAD_DIM // 4

HGRN_CHUNK = 16

D_FF = -(-(8 * D_MODEL) // (3 * 256)) * 256

SB_COLS = 3 * MIX_WIDTH
RWKV_COLS = 3 * MIX_WIDTH + RWKV_DECAY_LORA + RWKV_AAA_LORA + RWKV_GATE_LORA
MOBA_COLS = 3 * MIX_WIDTH
HGRN_COLS = 4 * MIX_WIDTH
SB_OFF = 0
RWKV_OFF = SB_OFF + SB_COLS
MOBA_OFF = RWKV_OFF + RWKV_COLS
HGRN_OFF = MOBA_OFF + MOBA_COLS
GATE_OFF = HGRN_OFF + HGRN_COLS
IN_COLS = GATE_OFF + N_BRANCH * D_MODEL

kernel_name = 'hybrid_gated_four_mixer_trunk'


def rms_norm(x, g):
    xf = x.astype(jnp.float32)
    y = xf * lax.rsqrt(jnp.mean(xf * xf, axis=-1, keepdims=True) + NORM_EPS)
    return (y * g.astype(jnp.float32)).astype(x.dtype)


def split_heads(t):
    b, s, _ = t.shape
    return t.reshape(b, s, N_HEADS, HEAD_DIM).transpose(0, 2, 1, 3)


def merge_heads(t):
    b, h, s, d = t.shape
    return t.transpose(0, 2, 1, 3).reshape(b, s, h * d)


def stick_breaking_attention(q, k, v):
    seq = q.shape[2]
    scale = HEAD_DIM ** -0.5
    qf = q.astype(jnp.float32)
    kf = k.astype(jnp.float32)
    vf = v.astype(jnp.float32)
    outs = []
    for blk in range(seq // SB_QBLOCK):
        t0 = blk * SB_QBLOCK
        kv_len = t0 + SB_QBLOCK
        z = jnp.einsum('bhtd,bhsd->bhts', qf[:, :, t0:kv_len], kf[:, :, :kv_len]) * scale
        t_pos = t0 + jnp.arange(SB_QBLOCK)[:, None]
        s_pos = jnp.arange(kv_len)[None, :]
        past = s_pos < t_pos
        log_keep = jnp.where(past, jax.nn.log_sigmoid(-z), 0.0)
        log_between = lax.cumsum(log_keep, axis=3, reverse=True) - log_keep
        att = jnp.where(past, jnp.exp(jax.nn.log_sigmoid(z) + log_between), 0.0)
        outs.append(jnp.einsum('bhts,bhsd->bhtd', att, vf[:, :, :kv_len]))
    return jnp.concatenate(outs, axis=2).astype(v.dtype)


def token_shift(p, mu):
    prev = jnp.pad(p, ((0, 0), (1, 0), (0, 0)))[:, :-1]
    return p + (prev - p) * mu


def rwkv7_recurrence(r, w, k, v, a, b):
    bsz, _, nh, nd = r.shape

    def step(state, inp):
        r_t, w_t, k_t, v_t, a_t, b_t = inp
        sa = jnp.einsum('bhvk,bhk->bhv', state, a_t)
        state = (state * w_t[:, :, None, :] + sa[..., None] * b_t[:, :, None, :]
                 + v_t[..., None] * k_t[:, :, None, :])
        return state, jnp.einsum('bhvk,bhk->bhv', state, r_t)

    xs = (jnp.moveaxis(r, 1, 0), jnp.moveaxis(w, 1, 0), jnp.moveaxis(k, 1, 0),
          jnp.moveaxis(v, 1, 0), jnp.moveaxis(a, 1, 0), jnp.moveaxis(b, 1, 0))
    init = jnp.zeros((bsz, nh, nd, nd), jnp.float32)
    _, y = lax.scan(step, init, xs)
    return jnp.moveaxis(y, 0, 1)


def rwkv7_time_mix(p, mu, w0, w2, a0, a2, g2, k_k, k_a, r_k, gn_w, gn_b, v_first, v0, v1, v2):
    bsz, seq, _ = p.shape
    p = token_shift(p, mu).astype(jnp.float32)
    c = MIX_WIDTH
    r = p[..., :c]
    k = p[..., c:2 * c]
    v = p[..., 2 * c:3 * c]
    o = 3 * c
    wl = p[..., o:o + RWKV_DECAY_LORA]
    o += RWKV_DECAY_LORA
    al = p[..., o:o + RWKV_AAA_LORA]
    o += RWKV_AAA_LORA
    gl = p[..., o:o + RWKV_GATE_LORA]
    w_log = -jax.nn.softplus(-(w0 + jnp.tanh(wl) @ w2)) - 0.5
    decay = jnp.exp(-jnp.exp(w_log))
    if v_first is None:
        v_first = v
    else:
        v = v + (v_first - v) * jax.nn.sigmoid(v0 + (v @ v1) @ v2)
    a = jax.nn.sigmoid(a0 + al @ a2)
    g = jax.nn.sigmoid(gl) @ g2

    def hs(t):
        return t.reshape(bsz, seq, N_HEADS, HEAD_DIM)

    kk = hs(k * k_k)
    kk = kk / jnp.maximum(jnp.sqrt(jnp.sum(kk * kk, axis=-1, keepdims=True)), 1e-12)
    k = k * (1.0 + (a - 1.0) * k_a)
    rh, kh, vh, ah = hs(r), hs(k), hs(v), hs(a)
    y = rwkv7_recurrence(rh, hs(decay), kh, vh, -kk, kk * ah)
    mean = jnp.mean(y, axis=-1, keepdims=True)
    var = jnp.mean(jnp.square(y - mean), axis=-1, keepdims=True)
    y = ((y - mean) * lax.rsqrt(var + RWKV_GN_EPS)).reshape(bsz, seq, c) * gn_w + gn_b
    bonus = jnp.sum(rh * kh * r_k, axis=-1, keepdims=True) * vh
    y = (y + bonus.reshape(bsz, seq, c)) * g
    return y, v_first


def rotary_tables(seq):
    inv_freq = ROPE_THETA ** (-jnp.arange(0, ROPE_DIM, 2, dtype=jnp.float32) / ROPE_DIM)
    ang = jnp.arange(seq, dtype=jnp.float32)[:, None] * inv_freq[None, :]
    return jnp.cos(ang), jnp.sin(ang)


def partial_rotary(t, cos, sin):
    half = ROPE_DIM // 2
    x1 = t[..., :half]
    x2 = t[..., half:ROPE_DIM]
    return jnp.concatenate([x1 * cos - x2 * sin, x2 * cos + x1 * sin, t[..., ROPE_DIM:]], axis=-1)


def moba_attention(q, k, v):
    bsz, nh, seq, hd = q.shape
    dt = v.dtype
    seq_pad = -(-seq // MOBA_BLOCK) * MOBA_BLOCK
    pad = ((0, 0), (0, 0), (0, seq_pad - seq), (0, 0))
    q = jnp.pad(q.astype(jnp.float32), pad)
    k = jnp.pad(k.astype(jnp.float32), pad)
    v = jnp.pad(v.astype(jnp.float32), pad)
    n_blk = seq_pad // MOBA_BLOCK
    k_blocks = k.reshape(bsz, nh, n_blk, MOBA_BLOCK, hd)
    v_blocks = v.reshape(bsz, nh, n_blk, MOBA_BLOCK, hd)
    k_mean = jnp.mean(k_blocks, axis=3)
    gate = jnp.einsum('bhtd,bhnd->bhtn', q, k_mean)
    q_blk = jnp.arange(seq_pad) // MOBA_BLOCK
    fully_past = jnp.arange(n_blk)[None, :] < q_blk[:, None]
    gate = jnp.where(fully_past, gate, MASK_VALUE)
    n_sel = min(MOBA_TOPK, n_blk)
    _, sel_idx = lax.top_k(gate, n_sel)
    sel_valid = sel_idx < q_blk[None, None, :, None]
    bi = jnp.arange(bsz)[:, None, None, None]
    hi = jnp.arange(nh)[None, :, None, None]
    scale = hd ** -0.5
    n_gathered = n_sel * MOBA_BLOCK

    def attend_chunk(t0):
        qc = lax.dynamic_slice_in_dim(q, t0, MOBA_QCHUNK, axis=2)
        idx = lax.dynamic_slice_in_dim(sel_idx, t0, MOBA_QCHUNK, axis=2)
        valid = lax.dynamic_slice_in_dim(sel_valid, t0, MOBA_QCHUNK, axis=2)
        k_sel = k_blocks[bi, hi, idx]
        v_sel = v_blocks[bi, hi, idx]
        s_sel = jnp.einsum('bhqd,bhqnkd->bhqnk', qc, k_sel) * scale
        s_sel = jnp.where(valid[..., None], s_sel, MASK_VALUE)
        own = t0 // MOBA_BLOCK
        k_own = lax.dynamic_index_in_dim(k_blocks, own, axis=2, keepdims=False)
        v_own = lax.dynamic_index_in_dim(v_blocks, own, axis=2, keepdims=False)
        s_own = jnp.einsum('bhqd,bhkd->bhqk', qc, k_own) * scale
        q_pos = t0 + jnp.arange(MOBA_QCHUNK)
        k_pos = own * MOBA_BLOCK + jnp.arange(MOBA_BLOCK)
        s_own = jnp.where(k_pos[None, :] <= q_pos[:, None], s_own, MASK_VALUE)
        scores = jnp.concatenate([s_sel.reshape(bsz, nh, MOBA_QCHUNK, n_gathered), s_own], axis=-1)
        probs = jax.nn.softmax(scores, axis=-1)
        p_sel = probs[..., :n_gathered].reshape(bsz, nh, MOBA_QCHUNK, n_sel, MOBA_BLOCK)
        p_own = probs[..., n_gathered:]
        return (jnp.einsum('bhqnk,bhqnkd->bhqd', p_sel, v_sel)
                + jnp.einsum('bhqk,bhkd->bhqd', p_own, v_own))

    starts = jnp.arange(seq_pad // MOBA_QCHUNK) * MOBA_QCHUNK
    out = lax.map(attend_chunk, starts)
    out = out.transpose(1, 2, 0, 3, 4).reshape(bsz, nh, seq_pad, hd)[:, :, :seq]
    return out.astype(dt)


def hgrn2_mixer(q, f_logit, i, g, lb, gn_g):
    bsz, seq, _ = q.shape
    n_chunk = seq // HGRN_CHUNK
    q = q.astype(jnp.float32)
    i = i.astype(jnp.float32)
    f_logit = f_logit.astype(jnp.float32)
    lb = lb.astype(jnp.float32)
    sig = jax.nn.sigmoid(f_logit)
    log_f = jnp.log(lb + (1.0 - lb) * sig)
    key = (1.0 - lb) * (1.0 - sig)

    def chunks(t):
        return t.reshape(bsz, n_chunk, HGRN_CHUNK, N_HEADS, HEAD_DIM).transpose(0, 3, 1, 2, 4)

    qc, kc, vc, lc = chunks(q), chunks(key), chunks(i), chunks(log_f)
    b = jnp.cumsum(lc, axis=3)
    b_end = b[:, :, :, -1:, :]
    causal = jnp.tril(jnp.ones((HGRN_CHUNK, HGRN_CHUNK), dtype=bool))
    diff = b[:, :, :, :, None, :] - b[:, :, :, None, :, :]
    pair_decay = jnp.exp(jnp.where(causal[:, :, None], diff, 0.0))
    scores = jnp.einsum('bhnck,bhnsk,bhncsk->bhncs', qc, kc, pair_decay)
    scores = jnp.where(causal, scores, 0.0)
    o = jnp.einsum('bhncs,bhnsv->bhncv', scores, vc)
    q_dec = qc * jnp.exp(b)
    k_end = kc * jnp.exp(b_end - b)
    chunk_kv = jnp.einsum('bhnsk,bhnsv->bhnkv', k_end, vc)
    chunk_decay = jnp.exp(b_end[:, :, :, 0, :])

    def step(state, inp):
        dec, kv = inp
        return dec[..., None] * state + kv, state

    init = jnp.zeros((bsz, N_HEADS, HEAD_DIM, HEAD_DIM), jnp.float32)
    _, prev = lax.scan(step, init, (jnp.moveaxis(chunk_decay, 2, 0), jnp.moveaxis(chunk_kv, 2, 0)))
    prev = jnp.moveaxis(prev, 0, 2)
    o = o + jnp.einsum('bhnck,bhnkv->bhncv', q_dec, prev)
    o = o.transpose(0, 2, 3, 1, 4).reshape(bsz, seq, N_HEADS, HEAD_DIM)
    o = o * lax.rsqrt(jnp.mean(o * o, axis=-1, keepdims=True) + NORM_EPS)
    return o.reshape(bsz, seq, MIX_WIDTH) * gn_g * jax.nn.silu(g.astype(jnp.float32))


def setup_inputs(seed: int = 0) -> dict:
    key = jax.random.key(seed)
    ks = jax.random.split(key, 32)
    L, D, W = DEPTH, D_MODEL, MIX_WIDTH
    LV = max(DEPTH - 1, 0)

    def nrm(k, shape, scale):
        return jax.random.normal(k, shape, jnp.float32) * scale

    return {
        'x': nrm(ks[0], (BATCH, SEQ, D), 1.0),
        'norm1_g': 1.0 + nrm(ks[1], (L, D), 0.02),
        'w_in': nrm(ks[2], (L, D, IN_COLS), D ** -0.5),
        'rwkv_mu': jax.random.uniform(ks[3], (L, RWKV_COLS), jnp.float32),
        'rwkv_w0': nrm(ks[4], (L, W), 0.5),
        'rwkv_w2': nrm(ks[5], (L, RWKV_DECAY_LORA, W), 0.1),
        'rwkv_a0': nrm(ks[6], (L, W), 0.1),
        'rwkv_a2': nrm(ks[7], (L, RWKV_AAA_LORA, W), 0.1),
        'rwkv_g2': nrm(ks[8], (L, RWKV_GATE_LORA, W), RWKV_GATE_LORA ** -0.5),
        'rwkv_k_k': 1.0 + nrm(ks[9], (L, W), 0.1),
        'rwkv_k_a': 1.0 + nrm(ks[10], (L, W), 0.1),
        'rwkv_r_k': nrm(ks[11], (L, N_HEADS, HEAD_DIM), 0.1),
        'rwkv_gn_w': 1.0 + nrm(ks[12], (L, W), 0.02),
        'rwkv_gn_b': nrm(ks[13], (L, W), 0.02),
        'rwkv_v0': nrm(ks[14], (LV, W), 0.1),
        'rwkv_v1': nrm(ks[15], (LV, W, RWKV_MV_LORA), W ** -0.5),
        'rwkv_v2': nrm(ks[16], (LV, RWKV_MV_LORA, W), 0.1),
        'hgrn_lb_logits': nrm(ks[17], (L, W), 1.0),
        'hgrn_gn_g': 1.0 + nrm(ks[18], (L, W), 0.02),
        'w_branch': nrm(ks[19], (L, N_BRANCH, W, D), W ** -0.5),
        'w_out': nrm(ks[20], (L, D, D), D ** -0.5),
        'norm2_g': 1.0 + nrm(ks[21], (L, D), 0.02),
        'w_ffn_gate': nrm(ks[22], (L, D, D_FF), D ** -0.5),
        'w_ffn_up': nrm(ks[23], (L, D, D_FF), D ** -0.5),
        'w_ffn_down': nrm(ks[24], (L, D_FF, D), D_FF ** -0.5),
        'final_g': 1.0 + nrm(ks[25], (D,), 0.02),
    }


def reference(x, norm1_g, w_in, rwkv_mu, rwkv_w0, rwkv_w2, rwkv_a0, rwkv_a2, rwkv_g2,
              rwkv_k_k, rwkv_k_a, rwkv_r_k, rwkv_gn_w, rwkv_gn_b, rwkv_v0, rwkv_v1, rwkv_v2,
              hgrn_lb_logits, hgrn_gn_g, w_branch, w_out, norm2_g, w_ffn_gate, w_ffn_up,
              w_ffn_down, final_g):
    seq = x.shape[1]
    W = MIX_WIDTH
    cos, sin = rotary_tables(seq)
    lb_w = jax.nn.softmax(hgrn_lb_logits.astype(jnp.float32), axis=0)
    lower_bounds = jnp.cumsum(lb_w, axis=0) - lb_w[0]
    v_first = None
    for layer in range(DEPTH):
        h = rms_norm(x, norm1_g[layer])
        w_l = w_in[layer]

        def cols(lo, width):
            return h @ w_l[:, lo:lo + width]

        sb = cols(SB_OFF, SB_COLS)
        y_sb = merge_heads(stick_breaking_attention(
            split_heads(sb[..., :W]), split_heads(sb[..., W:2 * W]), split_heads(sb[..., 2 * W:])))
        if layer == 0:
            v0 = v1 = v2 = None
        else:
            v0, v1, v2 = rwkv_v0[layer - 1], rwkv_v1[layer - 1], rwkv_v2[layer - 1]
        y_rwkv, v_first = rwkv7_time_mix(
            cols(RWKV_OFF, RWKV_COLS), rwkv_mu[layer], rwkv_w0[layer], rwkv_w2[layer],
            rwkv_a0[layer], rwkv_a2[layer], rwkv_g2[layer], rwkv_k_k[layer], rwkv_k_a[layer],
            rwkv_r_k[layer], rwkv_gn_w[layer], rwkv_gn_b[layer], v_first, v0, v1, v2)
        mc = cols(MOBA_OFF, MOBA_COLS)
        q_m = partial_rotary(split_heads(mc[..., :W]), cos, sin)
        k_m = partial_rotary(split_heads(mc[..., W:2 * W]), cos, sin)
        y_moba = merge_heads(moba_attention(q_m, k_m, split_heads(mc[..., 2 * W:])))
        hc = cols(HGRN_OFF, HGRN_COLS)
        y_hgrn = hgrn2_mixer(hc[..., :W], hc[..., W:2 * W], hc[..., 2 * W:3 * W], hc[..., 3 * W:],
                             lower_bounds[layer], hgrn_gn_g[layer])
        merged = None
        for n, y_n in enumerate((y_sb, y_rwkv, y_moba, y_hgrn)):
            gate = jax.nn.sigmoid(cols(GATE_OFF + n * D_MODEL, D_MODEL))
            term = gate * (y_n.astype(x.dtype) @ w_branch[layer, n])
            merged = term if merged is None else merged + term
        x = x + merged @ w_out[layer]
        h2 = rms_norm(x, norm2_g[layer])
        x = x + (jax.nn.silu(h2 @ w_ffn_gate[layer]) * (h2 @ w_ffn_up[layer])) @ w_ffn_down[layer]
    return rms_norm(x, final_g)
```

```python
import functools

import jax
import jax.numpy as jnp
from jax import lax
from jax.experimental import pallas as pl
from jax.experimental.pallas import tpu as pltpu

F32 = jnp.float32
BF16 = jnp.bfloat16

D_MODEL = 1024
HEAD_DIM = 64
N_HEADS = 4
MIX_WIDTH = N_HEADS * HEAD_DIM
N_BRANCH = 4
NORM_EPS = 1e-6
MASK_VALUE = -1e30

RWKV_DECAY_LORA = 64
RWKV_AAA_LORA = 64
RWKV_MV_LORA = 32
RWKV_GATE_LORA = 128
RWKV_GN_EPS = 64e-5
RWKV_COLS = 3 * MIX_WIDTH + RWKV_DECAY_LORA + RWKV_AAA_LORA + RWKV_GATE_LORA

MOBA_BLOCK = 256
MOBA_TOPK = 3
ROPE_THETA = 500000.0
ROPE_DIM = HEAD_DIM // 4

HGRN_CHUNK = 16

SB_COLS = 3 * MIX_WIDTH
MOBA_COLS = 3 * MIX_WIDTH
HGRN_COLS = 4 * MIX_WIDTH
SB_OFF = 0
RWKV_OFF = SB_OFF + SB_COLS
MOBA_OFF = RWKV_OFF + RWKV_COLS
HGRN_OFF = MOBA_OFF + MOBA_COLS
GATE_OFF = HGRN_OFF + HGRN_COLS

LANES = 128
VMEM_LIMIT = 56 * 1024 * 1024

ROW_TILE = 512
SEQ_TILE = 256
REC_TILE = 32


def _params(*sem):
    return pltpu.CompilerParams(dimension_semantics=sem, vmem_limit_bytes=VMEM_LIMIT)


def _dot(a, b):
    return jnp.dot(a, b, preferred_element_type=F32)


def _split2(a):
    hi = a.astype(BF16)
    lo = (a - hi.astype(F32)).astype(BF16)
    return hi, lo


def _split3(a):
    hi = a.astype(BF16)
    r1 = a - hi.astype(F32)
    mid = r1.astype(BF16)
    lo = (r1 - mid.astype(F32)).astype(BF16)
    return hi, mid, lo


def _dot_f32_lhs(a, m):
    hi, lo = _split2(a)
    return _dot(hi, m) + _dot(lo, m)


def _dot_f32_rhs(m, a):
    hi, mid, lo = _split3(a)
    return _dot(m, hi) + _dot(m, mid) + _dot(m, lo)


def _dot3(a, b):
    ah, al = _split2(a)
    bh, bl = _split2(b)
    return _dot(ah, bh) + _dot(ah, bl) + _dot(al, bh)


def _rms(x, g):
    return x * lax.rsqrt(jnp.mean(x * x, axis=-1, keepdims=True) + NORM_EPS) * g


def _norm_proj_kernel(x_ref, g_ref, w_ref, o_ref):
    h = _rms(x_ref[...], g_ref[...])
    o_ref[...] = _dot(h.astype(BF16), w_ref[...])


def _norm_proj(x2, g, w):
    m, d = x2.shape
    n = w.shape[1]
    return pl.pallas_call(
        _norm_proj_kernel,
        grid=(m // ROW_TILE,),
        in_specs=[
            pl.BlockSpec((ROW_TILE, d), lambda i: (i, 0)),
            pl.BlockSpec((1, d), lambda i: (0, 0)),
            pl.BlockSpec((d, n), lambda i: (0, 0)),
        ],
        out_specs=pl.BlockSpec((ROW_TILE, n), lambda i: (i, 0)),
        out_shape=jax.ShapeDtypeStruct((m, n), F32),
        compiler_params=_params("parallel"),
        name="norm_proj",
    )(x2, g.reshape(1, d), w)


def _sb_kernel(q_ref, kt_ref, v_ref, u_ref, o_ref, *, tq):
    i = pl.program_id(1)
    q = q_ref[0]
    u = u_ref[...]
    row = lax.broadcasted_iota(jnp.int32, (tq, tq), 0)
    col = lax.broadcasted_iota(jnp.int32, (tq, tq), 1)
    past = col < row

    def logits(j):
        start = pl.multiple_of(j * tq, tq)
        z = _dot(q, kt_ref[0, :, pl.ds(start, tq)])
        sp = jnp.maximum(z, 0.0) + jnp.log1p(jnp.exp(-jnp.abs(z)))
        return -sp, z - sp, v_ref[0, pl.ds(start, tq), :]

    lk, ls, vv = logits(i)
    lk = jnp.where(past, lk, 0.0)
    between = _dot_f32_lhs(lk, u)
    att = jnp.where(past, jnp.exp(ls + between), 0.0)
    acc = _dot(att.astype(BF16), vv)
    run = jnp.sum(lk, axis=1, keepdims=True)

    def body(jj, carry):
        acc, run = carry
        lk, ls, vv = logits(i - 1 - jj)
        att = jnp.exp(ls + _dot_f32_lhs(lk, u) + run)
        acc = acc + _dot(att.astype(BF16), vv)
        run = run + jnp.sum(lk, axis=1, keepdims=True)
        return acc, run

    acc, run = lax.fori_loop(0, i, body, (acc, run))
    o_ref[0] = acc


def _sb_attention(q, kt, v):
    bh, s, d = q.shape
    tq = SEQ_TILE
    u = (jnp.arange(tq)[:, None] > jnp.arange(tq)[None, :]).astype(BF16)
    return pl.pallas_call(
        functools.partial(_sb_kernel, tq=tq),
        grid=(bh, s // tq),
        in_specs=[
            pl.BlockSpec((1, tq, d), lambda b, i: (b, i, 0)),
            pl.BlockSpec((1, d, s), lambda b, i: (b, 0, 0)),
            pl.BlockSpec((1, s, d), lambda b, i: (b, 0, 0)),
            pl.BlockSpec((tq, tq), lambda b, i: (0, 0)),
        ],
        out_specs=pl.BlockSpec((1, tq, d), lambda b, i: (b, i, 0)),
        out_shape=jax.ShapeDtypeStruct((bh, s, d), F32),
        compiler_params=_params("parallel", "parallel"),
        name="sb_attention",
    )(q, kt, v, u)


def _rwkv_prep_kernel(*refs, has_vres):
    if has_vres:
        (p_ref, mu_ref, w0_ref, w2_ref, a0_ref, a2_ref, g2_ref, kk_ref, ka_ref, rk_ref, e_ref,
         vf_ref, v0_ref, v1_ref, v2_ref,
         r_o, w_o, k_o, v_o, an_o, bb_o, g_o, bonus_o, carry) = refs
    else:
        (p_ref, mu_ref, w0_ref, w2_ref, a0_ref, a2_ref, g2_ref, kk_ref, ka_ref, rk_ref, e_ref,
         r_o, w_o, k_o, v_o, an_o, bb_o, g_o, bonus_o, carry) = refs
    c = MIX_WIDTH
    p = p_ref[0]
    t = p.shape[0]

    @pl.when(pl.program_id(1) == 0)
    def _():
        carry[...] = jnp.zeros_like(carry)

    rowid = lax.broadcasted_iota(jnp.int32, p.shape, 0)
    prev = jnp.where(rowid == 0, carry[...], pltpu.roll(p, 1, axis=0))
    carry[...] = p[t - 1:t, :]
    xs = p + (prev - p) * mu_ref[...]

    r = xs[:, :c]
    k = xs[:, c:2 * c]
    v = xs[:, 2 * c:3 * c]
    lora = xs[:, 3 * c:3 * c + LANES]
    gl = xs[:, 3 * c + LANES:]
    w_log = -jax.nn.softplus(-(w0_ref[...] + _dot3(jnp.tanh(lora), w2_ref[...]))) - 0.5
    decay = jnp.exp(-jnp.exp(w_log))
    if has_vres:
        mix = jax.nn.sigmoid(v0_ref[...] + _dot3(_dot3(v, v1_ref[...]), v2_ref[...]))
        v = v + (vf_ref[0] - v) * mix
    a = jax.nn.sigmoid(a0_ref[...] + _dot3(lora, a2_ref[...]))
    g = _dot3(jax.nn.sigmoid(gl), g2_ref[...])
    e = e_ref[...]
    kk = k * kk_ref[...]
    kk = kk / jnp.maximum(jnp.sqrt(_dot_f32_lhs(kk * kk, e)), 1e-12)
    k = k * (1.0 + (a - 1.0) * ka_ref[...])
    bonus = _dot_f32_lhs(r * k * rk_ref[...], e) * v
    r_o[0] = r
    w_o[0] = decay
    k_o[0] = k
    v_o[0] = v
    an_o[0] = -kk
    bb_o[0] = kk * a
    g_o[0] = g
    bonus_o[0] = bonus


def _rwkv_prep(p, prm, e, v_first):
    b, s, cols = p.shape
    c = MIX_WIDTH
    t = SEQ_TILE
    has_vres = v_first is not None
    row = lambda n: pl.BlockSpec((1, n), lambda bi, j: (0, 0))
    mat = lambda m, n: pl.BlockSpec((m, n), lambda bi, j: (0, 0))
    seq = lambda n: pl.BlockSpec((1, t, n), lambda bi, j: (bi, j, 0))
    in_specs = [seq(cols), row(cols), row(c), mat(LANES, c), row(c), mat(LANES, c),
                mat(LANES, c), row(c), row(c), row(c), mat(c, c)]
    args = [p, prm["mu"], prm["w0"], prm["w2"], prm["a0"], prm["a2"], prm["g2"],
            prm["k_k"], prm["k_a"], prm["r_k"], e]
    if has_vres:
        in_specs += [seq(c), row(c), mat(c, LANES), mat(LANES, c)]
        args += [v_first, prm["v0"], prm["v1"], prm["v2"]]
    out = jax.ShapeDtypeStruct((b, s, c), F32)
    return pl.pallas_call(
        functools.partial(_rwkv_prep_kernel, has_vres=has_vres),
        grid=(b, s // t),
        in_specs=in_specs,
        out_specs=[seq(c)] * 8,
        out_shape=[out] * 8,
        scratch_shapes=[pltpu.VMEM((1, cols), F32)],
        compiler_params=_params("parallel", "arbitrary"),
        name="rwkv_prep",
    )(*args)


def _rwkv_rec_kernel(an_ref, w_ref, bb_ref, k_ref, r_ref, v_ref, y_ref, st_ref, *, n_tiles):
    @pl.when(pl.program_id(0) == 0)
    def _():
        st_ref[...] = jnp.zeros_like(st_ref)

    def body(t, carry):
        a = an_ref[t]
        w = w_ref[t]
        bb = bb_ref[t]
        k = k_ref[t]
        r = r_ref[t]
        vt = v_ref[t]
        rows = []
        for j in range(n_tiles):
            st = st_ref[j]
            sa = jnp.sum(st * a, axis=0, keepdims=True)
            st = st * w + sa * bb + vt[j:j + 1, :] * k
            st_ref[j] = st
            rows.append(jnp.sum(st * r, axis=0, keepdims=True))
        y_ref[t] = jnp.concatenate(rows, axis=0)
        return carry

    lax.fori_loop(0, an_ref.shape[0], body, 0)


def _rwkv_recurrence(r, w, k, v, an, bb):
    b, s, c = r.shape
    group = LANES // (b * N_HEADS)
    n_tiles = HEAD_DIM // group

    def expand(x):
        x = x.reshape(b, s, N_HEADS, HEAD_DIM).transpose(1, 3, 0, 2)
        x = jnp.broadcast_to(x[..., None], (s, HEAD_DIM, b, N_HEADS, group))
        return x.reshape(s, HEAD_DIM, LANES)

    ve = v.reshape(b, s, N_HEADS, n_tiles, group).transpose(1, 3, 0, 2, 4).reshape(s, n_tiles, LANES)
    big = pl.BlockSpec((REC_TILE, HEAD_DIM, LANES), lambda i: (i, 0, 0))
    small = pl.BlockSpec((REC_TILE, n_tiles, LANES), lambda i: (i, 0, 0))
    y = pl.pallas_call(
        functools.partial(_rwkv_rec_kernel, n_tiles=n_tiles),
        grid=(s // REC_TILE,),
        in_specs=[big] * 5 + [small],
        out_specs=small,
        out_shape=jax.ShapeDtypeStruct((s, n_tiles, LANES), F32),
        scratch_shapes=[pltpu.VMEM((n_tiles, HEAD_DIM, LANES), F32)],
        compiler_params=_params("arbitrary"),
        name="rwkv_recurrence",
    )(expand(an), expand(w), expand(bb), expand(k), expand(r), ve)
    y = y.reshape(s, n_tiles, b, N_HEADS, group).transpose(2, 0, 3, 1, 4)
    return y.reshape(b, s, c)


def _rwkv_post_kernel(y_ref, g_ref, bonus_ref, gw_ref, gb_ref, e_ref, o_ref):
    y = y_ref[...]
    e = e_ref[...]
    mean = _dot_f32_lhs(y, e) * (1.0 / HEAD_DIM)
    yc = y - mean
    var = _dot_f32_lhs(yc * yc, e) * (1.0 / HEAD_DIM)
    yn = yc * lax.rsqrt(var + RWKV_GN_EPS) * gw_ref[...] + gb_ref[...]
    o_ref[...] = (yn + bonus_ref[...]) * g_ref[...]


def _rwkv_post(y, g, bonus, gn_w, gn_b, e):
    m, c = y.shape
    tile = pl.BlockSpec((ROW_TILE, c), lambda i: (i, 0))
    row = pl.BlockSpec((1, c), lambda i: (0, 0))
    return pl.pallas_call(
        _rwkv_post_kernel,
        grid=(m // ROW_TILE,),
        in_specs=[tile, tile, tile, row, row, pl.BlockSpec((c, c), lambda i: (0, 0))],
        out_specs=tile,
        out_shape=jax.ShapeDtypeStruct((m, c), F32),
        compiler_params=_params("parallel"),
        name="rwkv_post",
    )(y, g, bonus, gn_w, gn_b, e)


def _rope_kernel(q_ref, k_ref, cos_ref, sin_ref, qo_ref, ko_ref):
    cos = cos_ref[...]
    sin = sin_ref[...]
    c = cos.shape[-1]
    lane = lax.broadcasted_iota(jnp.int32, cos.shape, 1) % HEAD_DIM
    first_half = lane < ROPE_DIM // 2

    def rot(x):
        partner = jnp.where(first_half,
                            pltpu.roll(x, c - ROPE_DIM // 2, axis=1),
                            pltpu.roll(x, ROPE_DIM // 2, axis=1))
        return x * cos + partner * sin

    qo_ref[0] = rot(q_ref[0])
    ko_ref[0] = rot(k_ref[0])


def _rope(mc, cos, sin):
    b, s, _ = mc.shape
    c = MIX_WIDTH
    t = SEQ_TILE
    out = jax.ShapeDtypeStruct((b, s, c), F32)
    return pl.pallas_call(
        _rope_kernel,
        grid=(b, s // t),
        in_specs=[
            pl.BlockSpec((1, t, c), lambda bi, j: (bi, j, 0)),
            pl.BlockSpec((1, t, c), lambda bi, j: (bi, j, 1)),
            pl.BlockSpec((t, c), lambda bi, j: (j, 0)),
            pl.BlockSpec((t, c), lambda bi, j: (j, 0)),
        ],
        out_specs=[pl.BlockSpec((1, t, c), lambda bi, j: (bi, j, 0))] * 2,
        out_shape=[out, out],
        compiler_params=_params("parallel", "parallel"),
        name="moba_rope",
    )(mc, mc, cos, sin)


def _moba_kernel(q_ref, kt_ref, v_ref, avg_ref, o_ref, km_ref, *, blk, n_sel):
    i = pl.program_id(1)

    @pl.when(i == 0)
    def _():
        km_ref[...] = jnp.dot(kt_ref[0], avg_ref[...], preferred_element_type=F32,
                              precision=lax.Precision.HIGHEST)

    q = q_ref[0]
    gate = jnp.dot(q, km_ref[...], preferred_element_type=F32,
                   precision=lax.Precision.HIGHEST)
    lane = lax.broadcasted_iota(jnp.int32, gate.shape, 1)
    fully_past = lane < i
    gate = jnp.where(fully_past, gate, MASK_VALUE)
    chosen = jnp.zeros(gate.shape, F32)
    for _ in range(n_sel):
        top = jnp.max(gate, axis=1, keepdims=True)
        idx = jnp.min(jnp.where(gate == top, lane, LANES), axis=1, keepdims=True)
        pick = lane == idx
        chosen = jnp.where(pick & fully_past, 1.0, chosen)
        gate = jnp.where(pick, -jnp.inf, gate)
    weight = jnp.where(lane < 24, jnp.left_shift(1, jnp.minimum(lane, 23)), 0).astype(F32)
    bits = jnp.sum(chosen * weight, axis=1, keepdims=True).astype(jnp.int32)

    qs = (q * (HEAD_DIM ** -0.5)).astype(BF16)
    row = lax.broadcasted_iota(jnp.int32, (blk, blk), 0)
    col = lax.broadcasted_iota(jnp.int32, (blk, blk), 1)

    def scores(n):
        start = pl.multiple_of(n * blk, blk)
        s = _dot(qs, kt_ref[0, :, pl.ds(start, blk)].astype(BF16))
        return s, v_ref[0, pl.ds(start, blk), :]

    s, vv = scores(i)
    s = jnp.where(col <= row, s, MASK_VALUE)
    m = jnp.max(s, axis=1, keepdims=True)
    p = jnp.exp(s - m)
    l = jnp.sum(p, axis=1, keepdims=True)
    acc = _dot(p.astype(BF16), vv)

    def body(n, carry):
        m, l, acc = carry
        s, vv = scores(n)
        sel = (jnp.right_shift(bits, n) & 1) == 1
        s = jnp.where(sel, s, MASK_VALUE)
        m_new = jnp.maximum(m, jnp.max(s, axis=1, keepdims=True))
        alpha = jnp.exp(m - m_new)
        p = jnp.exp(s - m_new)
        l = alpha * l + jnp.sum(p, axis=1, keepdims=True)
        acc = alpha * acc + _dot(p.astype(BF16), vv)
        return m_new, l, acc

    m, l, acc = lax.fori_loop(0, i, body, (m, l, acc))
    o_ref[0] = acc / l


def _moba_attention(q, kt, v):
    bh, s, d = q.shape
    blk = MOBA_BLOCK
    n_blk = s // blk
    assert s % blk == 0 and n_blk <= 24
    n_sel = min(MOBA_TOPK, n_blk)
    avg = ((jnp.arange(s)[:, None] // blk) == jnp.arange(LANES)[None, :]).astype(F32) / blk
    return pl.pallas_call(
        functools.partial(_moba_kernel, blk=blk, n_sel=n_sel),
        grid=(bh, n_blk),
        in_specs=[
            pl.BlockSpec((1, blk, d), lambda b, i: (b, i, 0)),
            pl.BlockSpec((1, d, s), lambda b, i: (b, 0, 0)),
            pl.BlockSpec((1, s, d), lambda b, i: (b, 0, 0)),
            pl.BlockSpec((s, LANES), lambda b, i: (0, 0)),
        ],
        out_specs=pl.BlockSpec((1, blk, d), lambda b, i: (b, i, 0)),
        out_shape=jax.ShapeDtypeStruct((bh, s, d), F32),
        scratch_shapes=[pltpu.VMEM((d, LANES), F32)],
        compiler_params=_params("parallel", "arbitrary"),
        name="moba_attention",
    )(q, kt, v, avg)


def _hgrn_kernel(q_ref, f_ref, i_ref, g_ref, lb_ref, gn_ref, e_ref, tri_ref, ones_ref, hm_ref,
                 o_ref, st_ref, q_s, k_s, b_s, v_s, qd_s, ke_s, be_s, o_s, *, chunk):
    c = MIX_WIDTH
    rows = q_ref.shape[1]

    @pl.when(pl.program_id(1) == 0)
    def _():
        st_ref[...] = jnp.zeros_like(st_ref)

    lb = lb_ref[...]
    sig = jax.nn.sigmoid(f_ref[0])
    log_f = jnp.log(lb + (1.0 - lb) * sig)
    key = (1.0 - lb) * (1.0 - sig)
    b = _dot_f32_rhs(tri_ref[...], log_f)
    b_end = _dot_f32_rhs(ones_ref[...], log_f)
    q = q_ref[0]
    q_s[...] = q
    k_s[...] = key
    b_s[...] = b
    v_s[...] = i_ref[0]
    qd_s[...] = q * jnp.exp(b)
    ke_s[...] = key * jnp.exp(b_end - b)
    be_s[...] = b_end
    e = e_ref[...]
    head_mask = hm_ref[...]
    sub = lax.broadcasted_iota(jnp.int32, (chunk, c), 0)

    def body(n, carry):
        r0 = pl.multiple_of(n * chunk, chunk)
        sl = pl.ds(r0, chunk)
        qi = q_s[sl, :]
        ki = k_s[sl, :]
        bi = b_s[sl, :]
        vi = v_s[sl, :]
        parts = []
        for s in range(chunk):
            pr = qi * ki[s:s + 1, :] * jnp.exp(jnp.minimum(bi - bi[s:s + 1, :], 0.0))
            parts.append(jnp.where(sub >= s, pr, 0.0))
        sc = _dot_f32_lhs(jnp.concatenate(parts, axis=0), e)
        o = jnp.zeros((chunk, c), F32)
        for s in range(chunk):
            o = o + sc[s * chunk:(s + 1) * chunk, :] * vi[s:s + 1, :]
        st = st_ref[...]
        o = o + lax.dot_general(qd_s[sl, :].astype(BF16), st.astype(BF16),
                                (((1,), (1,)), ((), ())), preferred_element_type=F32)
        o_s[sl, :] = o
        kv = lax.dot_general(vi.astype(BF16), ke_s[sl, :].astype(BF16),
                             (((0,), (0,)), ((), ())), preferred_element_type=F32)
        st_ref[...] = st * jnp.exp(be_s[pl.ds(r0, 1), :]) + kv * head_mask
        return carry

    lax.fori_loop(0, rows // chunk, body, 0)
    o = o_s[...]
    ms = _dot_f32_lhs(o * o, e) * (1.0 / HEAD_DIM)
    g = g_ref[0]
    o_ref[0] = o * lax.rsqrt(ms + NORM_EPS) * gn_ref[...] * (g * jax.nn.sigmoid(g))


def _hgrn(hc, lb, gn_g, e):
    b, s, _ = hc.shape
    c = MIX_WIDTH
    t = SEQ_TILE
    ch = HGRN_CHUNK
    ridx = jnp.arange(t)
    same = (ridx[:, None] // ch) == (ridx[None, :] // ch)
    tri = (same & (ridx[None, :] <= ridx[:, None])).astype(BF16)
    ones = same.astype(BF16)
    seq = lambda n: pl.BlockSpec((1, t, c), lambda bi, j: (bi, j, n))
    row = pl.BlockSpec((1, c), lambda bi, j: (0, 0))
    mat = lambda n: pl.BlockSpec((n, n), lambda bi, j: (0, 0))
    return pl.pallas_call(
        functools.partial(_hgrn_kernel, chunk=ch),
        grid=(b, s // t),
        in_specs=[seq(0), seq(1), seq(2), seq(3), row, row, mat(c), mat(t), mat(t), mat(c)],
        out_specs=pl.BlockSpec((1, t, c), lambda bi, j: (bi, j, 0)),
        out_shape=jax.ShapeDtypeStruct((b, s, c), F32),
        scratch_shapes=[pltpu.VMEM((c, c), F32)] + [pltpu.VMEM((t, c), F32)] * 8,
        compiler_params=_params("parallel", "arbitrary"),
        name="hgrn2",
    )(hc, hc, hc, hc, lb.reshape(1, c), gn_g.reshape(1, c), e, tri, ones, e.astype(F32))


def _merge_kernel(x_ref, g_ref, wg_ref, y0_ref, y1_ref, y2_ref, y3_ref, wb_ref, wo_ref, o_ref):
    x = x_ref[...]
    d = x.shape[1]
    h = _rms(x, g_ref[...]).astype(BF16)
    merged = None
    for n, y_ref in enumerate((y0_ref, y1_ref, y2_ref, y3_ref)):
        gate = jax.nn.sigmoid(_dot(h, wg_ref[:, n * d:(n + 1) * d]))
        term = gate * _dot(y_ref[...].astype(BF16), wb_ref[n])
        merged = term if merged is None else merged + term
    o_ref[...] = x + _dot(merged.astype(BF16), wo_ref[...])


def _merge(x2, g, w_gate, ys, w_branch, w_out):
    m, d = x2.shape
    c = MIX_WIDTH
    tile = pl.BlockSpec((ROW_TILE, d), lambda i: (i, 0))
    ytile = pl.BlockSpec((ROW_TILE, c), lambda i: (i, 0))
    return pl.pallas_call(
        _merge_kernel,
        grid=(m // ROW_TILE,),
        in_specs=[tile, pl.BlockSpec((1, d), lambda i: (0, 0)),
                  pl.BlockSpec((d, N_BRANCH * d), lambda i: (0, 0)),
                  ytile, ytile, ytile, ytile,
                  pl.BlockSpec((N_BRANCH, c, d), lambda i: (0, 0, 0)),
                  pl.BlockSpec((d, d), lambda i: (0, 0))],
        out_specs=tile,
        out_shape=jax.ShapeDtypeStruct((m, d), F32),
        compiler_params=_params("parallel"),
        name="gated_merge",
    )(x2, g.reshape(1, d), w_gate, *ys, w_branch, w_out)


def _ffn_kernel(*refs, final):
    if final:
        x_ref, g_ref, wg_ref, wu_ref, wd_ref, fg_ref, o_ref, h_s, acc_s = refs
    else:
        x_ref, g_ref, wg_ref, wu_ref, wd_ref, o_ref, h_s, acc_s = refs
    f = pl.program_id(1)

    @pl.when(f == 0)
    def _():
        h_s[...] = _rms(x_ref[...], g_ref[...]).astype(BF16)
        acc_s[...] = jnp.zeros_like(acc_s)

    h = h_s[...]
    gate = _dot(h, wg_ref[...])
    act = gate * jax.nn.sigmoid(gate) * _dot(h, wu_ref[...])
    acc_s[...] += _dot(act.astype(BF16), wd_ref[...])

    @pl.when(f == pl.num_programs(1) - 1)
    def _():
        y = x_ref[...] + acc_s[...]
        o_ref[...] = _rms(y, fg_ref[...]) if final else y


def _ffn(x2, g, w_gate, w_up, w_down, final_g):
    m, d = x2.shape
    dff = w_gate.shape[1]
    tf = dff // 2
    assert tf % LANES == 0
    final = final_g is not None
    tile = pl.BlockSpec((ROW_TILE, d), lambda i, f: (i, 0))
    row = pl.BlockSpec((1, d), lambda i, f: (0, 0))
    in_specs = [tile, row,
                pl.BlockSpec((d, tf), lambda i, f: (0, f)),
                pl.BlockSpec((d, tf), lambda i, f: (0, f)),
                pl.BlockSpec((tf, d), lambda i, f: (f, 0))]
    args = [x2, g.reshape(1, d), w_gate, w_up, w_down]
    if final:
        in_specs.append(row)
        args.append(final_g.reshape(1, d))
    return pl.pallas_call(
        functools.partial(_ffn_kernel, final=final),
        grid=(m // ROW_TILE, dff // tf),
        in_specs=in_specs,
        out_specs=tile,
        out_shape=jax.ShapeDtypeStruct((m, d), F32),
        scratch_shapes=[pltpu.VMEM((ROW_TILE, d), BF16), pltpu.VMEM((ROW_TILE, d), F32)],
        compiler_params=_params("parallel", "arbitrary"),
        name="ffn",
    )(*args)


def _to_heads(t, b, s):
    return t.reshape(b, s, N_HEADS, HEAD_DIM).transpose(0, 2, 1, 3).reshape(b * N_HEADS, s, HEAD_DIM)


def _to_heads_t(t, b, s):
    return t.reshape(b, s, N_HEADS, HEAD_DIM).transpose(0, 2, 3, 1).reshape(b * N_HEADS, HEAD_DIM, s)


def _from_heads(t, b, s):
    return t.reshape(b, N_HEADS, s, HEAD_DIM).transpose(0, 2, 1, 3).reshape(b * s, MIX_WIDTH)


def _rope_tables(s):
    half = ROPE_DIM // 2
    inv_freq = ROPE_THETA ** (-jnp.arange(0, ROPE_DIM, 2, dtype=F32) / ROPE_DIM)
    ang = jnp.arange(s, dtype=F32)[:, None] * inv_freq[None, :]
    cos, sin = jnp.cos(ang), jnp.sin(ang)
    pad = HEAD_DIM - ROPE_DIM
    cos_h = jnp.concatenate([cos, cos, jnp.ones((s, pad), F32)], axis=1)
    sin_h = jnp.concatenate([-sin, sin, jnp.zeros((s, pad), F32)], axis=1)
    return jnp.tile(cos_h, (1, N_HEADS)), jnp.tile(sin_h, (1, N_HEADS))


def kernel(x, norm1_g, w_in, rwkv_mu, rwkv_w0, rwkv_w2, rwkv_a0, rwkv_a2, rwkv_g2, rwkv_k_k, rwkv_k_a, rwkv_r_k, rwkv_gn_w, rwkv_gn_b, rwkv_v0, rwkv_v1, rwkv_v2, hgrn_lb_logits, hgrn_gn_g, w_branch, w_out, norm2_g, w_ffn_gate, w_ffn_up, w_ffn_down, final_g):
    b, s, d = x.shape
    depth = w_in.shape[0]
    c = MIX_WIDTH
    m = b * s
    assert d == D_MODEL and s % SEQ_TILE == 0 and m % ROW_TILE == 0 and s % REC_TILE == 0
    assert LANES % (b * N_HEADS) == 0 and HEAD_DIM % (LANES // (b * N_HEADS)) == 0

    cos, sin = _rope_tables(s)
    lb_w = jax.nn.softmax(hgrn_lb_logits.astype(F32), axis=0)
    lower_bounds = jnp.cumsum(lb_w, axis=0) - lb_w[0]
    head_id = jnp.arange(c) // HEAD_DIM
    e = (head_id[:, None] == head_id[None, :]).astype(BF16)
    row = lambda t: t.reshape(1, -1)
    zpad = lambda t, top, bottom: jnp.pad(t, ((top, bottom), (0, 0)))

    x2 = x.reshape(m, d)
    v_first = None
    for layer in range(depth):
        w_l = w_in[layer].astype(BF16)
        g1 = norm1_g[layer]
        proj = lambda lo, width: _norm_proj(x2, g1, w_l[:, lo:lo + width])

        sb = proj(SB_OFF, SB_COLS).reshape(b, s, SB_COLS)
        q_sb = _to_heads(sb[..., :c] * (HEAD_DIM ** -0.5), b, s).astype(BF16)
        y_sb = _sb_attention(q_sb, _to_heads_t(sb[..., c:2 * c], b, s).astype(BF16),
                             _to_heads(sb[..., 2 * c:], b, s).astype(BF16))
        y_sb = _from_heads(y_sb, b, s)

        prm = {
            "mu": row(rwkv_mu[layer]), "w0": row(rwkv_w0[layer]),
            "w2": zpad(rwkv_w2[layer], 0, LANES - RWKV_DECAY_LORA),
            "a0": row(rwkv_a0[layer]),
            "a2": zpad(rwkv_a2[layer], RWKV_DECAY_LORA, LANES - RWKV_DECAY_LORA - RWKV_AAA_LORA),
            "g2": rwkv_g2[layer], "k_k": row(rwkv_k_k[layer]), "k_a": row(rwkv_k_a[layer]),
            "r_k": row(rwkv_r_k[layer]),
        }
        if layer > 0:
            prm["v0"] = row(rwkv_v0[layer - 1])
            prm["v1"] = jnp.pad(rwkv_v1[layer - 1], ((0, 0), (0, LANES - RWKV_MV_LORA)))
            prm["v2"] = zpad(rwkv_v2[layer - 1], 0, LANES - RWKV_MV_LORA)
        rp = proj(RWKV_OFF, RWKV_COLS).reshape(b, s, RWKV_COLS)
        r_, w_, k_, v_, an_, bb_, g_, bonus_ = _rwkv_prep(rp, prm, e, v_first)
        if layer == 0:
            v_first = v_
        y_rec = _rwkv_recurrence(r_, w_, k_, v_, an_, bb_)
        y_rwkv = _rwkv_post(y_rec.reshape(m, c), g_.reshape(m, c), bonus_.reshape(m, c),
                            row(rwkv_gn_w[layer]), row(rwkv_gn_b[layer]), e)

        mc = proj(MOBA_OFF, MOBA_COLS).reshape(b, s, MOBA_COLS)
        q_m, k_m = _rope(mc, cos, sin)
        y_moba = _moba_attention(_to_heads(q_m, b, s), _to_heads_t(k_m, b, s),
                                 _to_heads(mc[..., 2 * c:], b, s).astype(BF16))
        y_moba = _from_heads(y_moba, b, s)

        hc = proj(HGRN_OFF, HGRN_COLS).reshape(b, s, HGRN_COLS)
        y_hgrn = _hgrn(hc, lower_bounds[layer], hgrn_gn_g[layer], e).reshape(m, c)

        x2 = _merge(x2, g1, w_l[:, GATE_OFF:], (y_sb, y_rwkv, y_moba, y_hgrn),
                    w_branch[layer].astype(BF16), w_out[layer].astype(BF16))
        x2 = _ffn(x2, norm2_g[layer], w_ffn_gate[layer].astype(BF16), w_ffn_up[layer].astype(BF16),
                  w_ffn_down[layer].astype(BF16), final_g if layer == depth - 1 else None)
    return x2.reshape(b, s, d)
```

```python
import functools

import jax
import jax.numpy as jnp
from jax import lax
from jax.experimental import pallas as pl
from jax.experimental.pallas import tpu as pltpu

F32 = jnp.float32
BF16 = jnp.bfloat16

D_MODEL = 1024
HEAD_DIM = 64
N_HEADS = 4
MIX_WIDTH = N_HEADS * HEAD_DIM
N_BRANCH = 4
NORM_EPS = 1e-6
MASK_VALUE = -1e30

RWKV_DECAY_LORA = 64
RWKV_AAA_LORA = 64
RWKV_MV_LORA = 32
RWKV_GATE_LORA = 128
RWKV_GN_EPS = 64e-5
RWKV_COLS = 3 * MIX_WIDTH + RWKV_DECAY_LORA + RWKV_AAA_LORA + RWKV_GATE_LORA

MOBA_BLOCK = 256
MOBA_TOPK = 3
ROPE_THETA = 500000.0
ROPE_DIM = HEAD_DIM // 4

HGRN_CHUNK = 16

SB_COLS = 3 * MIX_WIDTH
MOBA_COLS = 3 * MIX_WIDTH
HGRN_COLS = 4 * MIX_WIDTH
SB_OFF = 0
RWKV_OFF = SB_OFF + SB_COLS
MOBA_OFF = RWKV_OFF + RWKV_COLS
HGRN_OFF = MOBA_OFF + MOBA_COLS
GATE_OFF = HGRN_OFF + HGRN_COLS

LANES = 128
VMEM_LIMIT = 56 * 1024 * 1024

ROW_TILE = 512
SEQ_TILE = 256
REC_TILE = 128


def _params(*sem):
    return pltpu.CompilerParams(dimension_semantics=sem, vmem_limit_bytes=VMEM_LIMIT)


def _dot(a, b):
    return jnp.dot(a, b, preferred_element_type=F32)


def _split2(a):
    hi = a.astype(BF16)
    lo = (a - hi.astype(F32)).astype(BF16)
    return hi, lo


def _split3(a):
    hi = a.astype(BF16)
    r1 = a - hi.astype(F32)
    mid = r1.astype(BF16)
    lo = (r1 - mid.astype(F32)).astype(BF16)
    return hi, mid, lo


def _dot_f32_lhs(a, m):
    hi, lo = _split2(a)
    return _dot(hi, m) + _dot(lo, m)


def _dot_f32_rhs(m, a):
    hi, mid, lo = _split3(a)
    return _dot(m, hi) + _dot(m, mid) + _dot(m, lo)


def _dot3(a, b):
    ah, al = _split2(a)
    bh, bl = _split2(b)
    return _dot(ah, bh) + _dot(ah, bl) + _dot(al, bh)


def _rms(x, g):
    return x * lax.rsqrt(jnp.mean(x * x, axis=-1, keepdims=True) + NORM_EPS) * g


def _norm_proj_kernel(x_ref, g_ref, w_ref, o_ref):
    h = _rms(x_ref[...], g_ref[...])
    o_ref[...] = _dot(h.astype(BF16), w_ref[...])


def _norm_proj(x2, g, w):
    m, d = x2.shape
    n = w.shape[1]
    return pl.pallas_call(
        _norm_proj_kernel,
        grid=(m // ROW_TILE,),
        in_specs=[
            pl.BlockSpec((ROW_TILE, d), lambda i: (i, 0)),
            pl.BlockSpec((1, d), lambda i: (0, 0)),
            pl.BlockSpec((d, n), lambda i: (0, 0)),
        ],
        out_specs=pl.BlockSpec((ROW_TILE, n), lambda i: (i, 0)),
        out_shape=jax.ShapeDtypeStruct((m, n), F32),
        compiler_params=_params("parallel"),
        name="norm_proj",
    )(x2, g.reshape(1, d), w)


SB_DEAD_LOG = -110.0


def _sb_kernel(qt_ref, k_ref, vt_ref, ut_ref, o_ref, acc_s, *, tq, n_heads):
    i = pl.program_id(1)
    ut = ut_ref[...]
    key = lax.broadcasted_iota(jnp.int32, (tq, tq), 0)
    qry = lax.broadcasted_iota(jnp.int32, (tq, tq), 1)
    past = key < qry

    def block(j, runs, diagonal):
        start = pl.multiple_of(j * tq, tq)
        heads = range(n_heads)
        zs = [_dot(k_ref[0, h, pl.ds(start, tq), :], qt_ref[0, h]) for h in heads]
        lks, lbs, pieces = [], [], []
        for h in heads:
            z = zs[h]
            sp = jnp.maximum(z, 0.0) + jnp.log(1.0 + jnp.exp(-jnp.abs(z)))
            lk = -sp
            if diagonal:
                lk = jnp.where(past, lk, 0.0)
            lks.append(lk)
            lbs.append(z - sp)
            pieces.append(_split2(lk))
        betweens = [_dot(ut, hi) + _dot(ut, lo) for hi, lo in pieces]
        atts = []
        for h in heads:
            att = jnp.exp(lbs[h] + betweens[h] + runs[h])
            if diagonal:
                att = jnp.where(past, att, 0.0)
            atts.append(att.astype(BF16))
        for h in heads:
            part = _dot(vt_ref[0, h, :, pl.ds(start, tq)], atts[h])
            acc_s[h] = part if diagonal else acc_s[h] + part
        return tuple(runs[h] + jnp.sum(lks[h], axis=0, keepdims=True) for h in heads)

    def live(runs):
        top = functools.reduce(jnp.maximum, [jnp.max(r) for r in runs])
        return (top > SB_DEAD_LOG).astype(jnp.int32)

    runs = block(i, tuple(jnp.zeros((1, tq), F32) for _ in range(n_heads)), True)

    def cond(c):
        jj, go, _ = c
        return jnp.logical_and(jj < i, go > 0)

    def body(c):
        jj, _, runs = c
        runs = block(i - 1 - jj, runs, False)
        return jj + 1, live(runs), runs

    lax.while_loop(cond, body, (jnp.int32(0), live(runs), runs))
    o_ref[0] = acc_s[...]


def _sb_attention(qt, k, vt):
    b, nh, d, s = qt.shape
    tq = SEQ_TILE
    ut = (jnp.arange(tq)[None, :] > jnp.arange(tq)[:, None]).astype(BF16)
    return pl.pallas_call(
        functools.partial(_sb_kernel, tq=tq, n_heads=nh),
        grid=(b, s // tq),
        in_specs=[
            pl.BlockSpec((1, nh, d, tq), lambda bi, i: (bi, 0, 0, i)),
            pl.BlockSpec((1, nh, s, d), lambda bi, i: (bi, 0, 0, 0)),
            pl.BlockSpec((1, nh, d, s), lambda bi, i: (bi, 0, 0, 0)),
            pl.BlockSpec((tq, tq), lambda bi, i: (0, 0)),
        ],
        out_specs=pl.BlockSpec((1, nh, d, tq), lambda bi, i: (bi, 0, 0, i)),
        out_shape=jax.ShapeDtypeStruct((b, nh, d, s), F32),
        scratch_shapes=[pltpu.VMEM((nh, d, tq), F32)],
        compiler_params=_params("parallel", "parallel"),
        name="sb_attention",
    )(qt, k, vt, ut)


def _rwkv_prep_kernel(*refs, has_vres):
    if has_vres:
        (p_ref, mu_ref, w0_ref, w2_ref, a0_ref, a2_ref, g2_ref, kk_ref, ka_ref, rk_ref, e_ref,
         vf_ref, v0_ref, v1_ref, v2_ref,
         r_o, w_o, k_o, v_o, an_o, bb_o, g_o, bonus_o, carry) = refs
    else:
        (p_ref, mu_ref, w0_ref, w2_ref, a0_ref, a2_ref, g2_ref, kk_ref, ka_ref, rk_ref, e_ref,
         r_o, w_o, k_o, v_o, an_o, bb_o, g_o, bonus_o, carry) = refs
    c = MIX_WIDTH
    p = p_ref[0]
    t = p.shape[0]

    @pl.when(pl.program_id(1) == 0)
    def _():
        carry[...] = jnp.zeros_like(carry)

    rowid = lax.broadcasted_iota(jnp.int32, p.shape, 0)
    prev = jnp.where(rowid == 0, carry[...], pltpu.roll(p, 1, axis=0))
    carry[...] = p[t - 1:t, :]
    xs = p + (prev - p) * mu_ref[...]

    r = xs[:, :c]
    k = xs[:, c:2 * c]
    v = xs[:, 2 * c:3 * c]
    lora = xs[:, 3 * c:3 * c + LANES]
    gl = xs[:, 3 * c + LANES:]
    w_log = -jax.nn.softplus(-(w0_ref[...] + _dot3(jnp.tanh(lora), w2_ref[...]))) - 0.5
    decay = jnp.exp(-jnp.exp(w_log))
    if has_vres:
        mix = jax.nn.sigmoid(v0_ref[...] + _dot3(_dot3(v, v1_ref[...]), v2_ref[...]))
        v = v + (vf_ref[0] - v) * mix
    a = jax.nn.sigmoid(a0_ref[...] + _dot3(lora, a2_ref[...]))
    g = _dot3(jax.nn.sigmoid(gl), g2_ref[...])
    e = e_ref[...]
    kk = k * kk_ref[...]
    kk = kk / jnp.maximum(jnp.sqrt(_dot_f32_lhs(kk * kk, e)), 1e-12)
    k = k * (1.0 + (a - 1.0) * ka_ref[...])
    bonus = _dot_f32_lhs(r * k * rk_ref[...], e) * v
    r_o[0] = r
    w_o[0] = decay
    k_o[0] = k
    v_o[0] = v
    an_o[0] = -kk
    bb_o[0] = kk * a
    g_o[0] = g
    bonus_o[0] = bonus


def _rwkv_prep(p, prm, e, v_first):
    b, s, cols = p.shape
    c = MIX_WIDTH
    t = SEQ_TILE
    has_vres = v_first is not None
    row = lambda n: pl.BlockSpec((1, n), lambda bi, j: (0, 0))
    mat = lambda m, n: pl.BlockSpec((m, n), lambda bi, j: (0, 0))
    seq = lambda n: pl.BlockSpec((1, t, n), lambda bi, j: (bi, j, 0))
    in_specs = [seq(cols), row(cols), row(c), mat(LANES, c), row(c), mat(LANES, c),
                mat(LANES, c), row(c), row(c), row(c), mat(c, c)]
    args = [p, prm["mu"], prm["w0"], prm["w2"], prm["a0"], prm["a2"], prm["g2"],
            prm["k_k"], prm["k_a"], prm["r_k"], e]
    if has_vres:
        in_specs += [seq(c), row(c), mat(c, LANES), mat(LANES, c)]
        args += [v_first, prm["v0"], prm["v1"], prm["v2"]]
    out = jax.ShapeDtypeStruct((b, s, c), F32)
    return pl.pallas_call(
        functools.partial(_rwkv_prep_kernel, has_vres=has_vres),
        grid=(b, s // t),
        in_specs=in_specs,
        out_specs=[seq(c)] * 8,
        out_shape=[out] * 8,
        scratch_shapes=[pltpu.VMEM((1, cols), F32)],
        compiler_params=_params("parallel", "arbitrary"),
        name="rwkv_prep",
    )(*args)


def _rwkv_rec_kernel(an_ref, w_ref, bb_ref, k_ref, r_ref, v_ref, y_ref, st_ref, *, n_tiles, group):
    @pl.when(pl.program_id(0) == 0)
    def _():
        st_ref[...] = jnp.zeros_like(st_ref)

    lane = lax.broadcasted_iota(jnp.int32, (HEAD_DIM, LANES), 1)
    per_token = LANES // group

    def body(tp, carry):
        for u in range(group):
            idx = u * per_token + lane // group
            spread = lambda ref: jnp.take_along_axis(ref[tp], idx, axis=1)
            a, w, bb, k, r = (spread(ref) for ref in (an_ref, w_ref, bb_ref, k_ref, r_ref))
            t = tp * group + u
            vt = v_ref[t]
            rows = []
            for j in range(n_tiles):
                st = st_ref[j]
                sa = jnp.sum(st * a, axis=0, keepdims=True)
                st = st * w + sa * bb + vt[j:j + 1, :] * k
                st_ref[j] = st
                rows.append(jnp.sum(st * r, axis=0, keepdims=True))
            y_ref[t] = jnp.concatenate(rows, axis=0)
        return carry

    lax.fori_loop(0, an_ref.shape[0], body, 0)


def _rwkv_recurrence(r, w, k, v, an, bb):
    b, s, c = r.shape
    group = LANES // (b * N_HEADS)
    n_tiles = HEAD_DIM // group

    def pack(x):
        x = x.reshape(b, s // group, group, N_HEADS, HEAD_DIM).transpose(1, 4, 2, 0, 3)
        return x.reshape(s // group, HEAD_DIM, LANES)

    ve = v.reshape(b, s, N_HEADS, n_tiles, group).transpose(1, 3, 0, 2, 4).reshape(s, n_tiles, LANES)
    big = pl.BlockSpec((REC_TILE // group, HEAD_DIM, LANES), lambda i: (i, 0, 0))
    small = pl.BlockSpec((REC_TILE, n_tiles, LANES), lambda i: (i, 0, 0))
    y = pl.pallas_call(
        functools.partial(_rwkv_rec_kernel, n_tiles=n_tiles, group=group),
        grid=(s // REC_TILE,),
        in_specs=[big] * 5 + [small],
        out_specs=small,
        out_shape=jax.ShapeDtypeStruct((s, n_tiles, LANES), F32),
        scratch_shapes=[pltpu.VMEM((n_tiles, HEAD_DIM, LANES), F32)],
        compiler_params=_params("arbitrary"),
        name="rwkv_recurrence",
    )(pack(an), pack(w), pack(bb), pack(k), pack(r), ve)
    y = y.reshape(s, n_tiles, b, N_HEADS, group).transpose(2, 0, 3, 1, 4)
    return y.reshape(b, s, c)


def _rwkv_post_kernel(y_ref, g_ref, bonus_ref, gw_ref, gb_ref, e_ref, o_ref):
    y = y_ref[...]
    e = e_ref[...]
    mean = _dot_f32_lhs(y, e) * (1.0 / HEAD_DIM)
    yc = y - mean
    var = _dot_f32_lhs(yc * yc, e) * (1.0 / HEAD_DIM)
    yn = yc * lax.rsqrt(var + RWKV_GN_EPS) * gw_ref[...] + gb_ref[...]
    o_ref[...] = (yn + bonus_ref[...]) * g_ref[...]


def _rwkv_post(y, g, bonus, gn_w, gn_b, e):
    m, c = y.shape
    tile = pl.BlockSpec((ROW_TILE, c), lambda i: (i, 0))
    row = pl.BlockSpec((1, c), lambda i: (0, 0))
    return pl.pallas_call(
        _rwkv_post_kernel,
        grid=(m // ROW_TILE,),
        in_specs=[tile, tile, tile, row, row, pl.BlockSpec((c, c), lambda i: (0, 0))],
        out_specs=tile,
        out_shape=jax.ShapeDtypeStruct((m, c), F32),
        compiler_params=_params("parallel"),
        name="rwkv_post",
    )(y, g, bonus, gn_w, gn_b, e)


def _rope_kernel(q_ref, k_ref, cos_ref, sin_ref, qo_ref, ko_ref, km_ref):
    cos = cos_ref[...]
    sin = sin_ref[...]
    c = cos.shape[-1]
    lane = lax.broadcasted_iota(jnp.int32, cos.shape, 1) % HEAD_DIM
    first_half = lane < ROPE_DIM // 2

    def rot(x):
        partner = jnp.where(first_half,
                            pltpu.roll(x, c - ROPE_DIM // 2, axis=1),
                            pltpu.roll(x, ROPE_DIM // 2, axis=1))
        return x * cos + partner * sin

    qo_ref[0] = rot(q_ref[0])
    kr = rot(k_ref[0])
    ko_ref[0] = kr
    km_ref[0, 0] = jnp.mean(kr, axis=0, keepdims=True)


def _rope(mc, cos, sin):
    b, s, _ = mc.shape
    c = MIX_WIDTH
    t = MOBA_BLOCK
    out = jax.ShapeDtypeStruct((b, s, c), F32)
    return pl.pallas_call(
        _rope_kernel,
        grid=(b, s // t),
        in_specs=[
            pl.BlockSpec((1, t, c), lambda bi, j: (bi, j, 0)),
            pl.BlockSpec((1, t, c), lambda bi, j: (bi, j, 1)),
            pl.BlockSpec((t, c), lambda bi, j: (j, 0)),
            pl.BlockSpec((t, c), lambda bi, j: (j, 0)),
        ],
        out_specs=[pl.BlockSpec((1, t, c), lambda bi, j: (bi, j, 0))] * 2
        + [pl.BlockSpec((1, 1, 1, c), lambda bi, j: (bi, j, 0, 0))],
        out_shape=[out, out, jax.ShapeDtypeStruct((b, s // t, 1, c), F32)],
        compiler_params=_params("parallel", "parallel"),
        name="moba_rope",
    )(mc, mc, cos, sin)


def _moba_kernel(qt_ref, k_ref, vt_ref, km_ref, o_ref, acc_s, *, blk, n_sel, n_heads):
    i = pl.program_id(1)
    n_pad = km_ref.shape[2]
    blk_id = lax.broadcasted_iota(jnp.int32, (n_pad, blk), 0)
    fully_past = blk_id < i
    weight = jnp.where(blk_id < 24, jnp.left_shift(1, jnp.minimum(blk_id, 23)), 0).astype(F32)
    key = lax.broadcasted_iota(jnp.int32, (blk, blk), 0)
    qry = lax.broadcasted_iota(jnp.int32, (blk, blk), 1)
    causal = key <= qry

    qs, bits = [], []
    for h in range(n_heads):
        qt = qt_ref[0, h]
        gate = jnp.dot(km_ref[0, h], qt, preferred_element_type=F32,
                       precision=lax.Precision.HIGHEST)
        gate = jnp.where(fully_past, gate, MASK_VALUE)
        chosen = jnp.zeros(gate.shape, F32)
        for _ in range(n_sel):
            top = jnp.max(gate, axis=0, keepdims=True)
            idx = jnp.min(jnp.where(gate == top, blk_id, n_pad), axis=0, keepdims=True)
            pick = blk_id == idx
            chosen = jnp.where(pick & fully_past, 1.0, chosen)
            gate = jnp.where(pick, -jnp.inf, gate)
        bits.append(jnp.sum(chosen * weight, axis=0, keepdims=True).astype(jnp.int32))
        qs.append((qt * (HEAD_DIM ** -0.5)).astype(BF16))

    def scores(n, h):
        start = pl.multiple_of(n * blk, blk)
        return _dot(k_ref[0, h, pl.ds(start, blk), :], qs[h]), vt_ref[0, h, :, pl.ds(start, blk)]

    heads = range(n_heads)
    ms, ls, ps = [], [], []
    own = [scores(i, h) for h in heads]
    for h in heads:
        s = jnp.where(causal, own[h][0], MASK_VALUE)
        m = jnp.max(s, axis=0, keepdims=True)
        p = jnp.exp(s - m)
        ms.append(m)
        ls.append(jnp.sum(p, axis=0, keepdims=True))
        ps.append(p.astype(BF16))
    for h in heads:
        acc_s[h] = _dot(own[h][1], ps[h])

    def body(n, carry):
        ms, ls = carry
        blocks = [scores(n, h) for h in heads]
        new_ms, new_ls, alphas, ps = [], [], [], []
        for h in heads:
            sel = (jnp.right_shift(bits[h], n) & 1) == 1
            s = jnp.where(sel, blocks[h][0], MASK_VALUE)
            m_new = jnp.maximum(ms[h], jnp.max(s, axis=0, keepdims=True))
            alpha = jnp.exp(ms[h] - m_new)
            p = jnp.exp(s - m_new)
            new_ms.append(m_new)
            new_ls.append(alpha * ls[h] + jnp.sum(p, axis=0, keepdims=True))
            alphas.append(alpha)
            ps.append(p.astype(BF16))
        for h in heads:
            acc_s[h] = alphas[h] * acc_s[h] + _dot(blocks[h][1], ps[h])
        return tuple(new_ms), tuple(new_ls)

    ms, ls = lax.fori_loop(0, i, body, (tuple(ms), tuple(ls)))
    for h in range(n_heads):
        o_ref[0, h] = acc_s[h] / ls[h]


def _moba_attention(qt, k, vt, km):
    b, nh, d, s = qt.shape
    blk = MOBA_BLOCK
    n_blk = s // blk
    n_pad = km.shape[2]
    assert s % blk == 0 and n_blk <= 24
    n_sel = min(MOBA_TOPK, n_blk)
    return pl.pallas_call(
        functools.partial(_moba_kernel, blk=blk, n_sel=n_sel, n_heads=nh),
        grid=(b, n_blk),
        in_specs=[
            pl.BlockSpec((1, nh, d, blk), lambda bi, i: (bi, 0, 0, i)),
            pl.BlockSpec((1, nh, s, d), lambda bi, i: (bi, 0, 0, 0)),
            pl.BlockSpec((1, nh, d, s), lambda bi, i: (bi, 0, 0, 0)),
            pl.BlockSpec((1, nh, n_pad, d), lambda bi, i: (bi, 0, 0, 0)),
        ],
        out_specs=pl.BlockSpec((1, nh, d, blk), lambda bi, i: (bi, 0, 0, i)),
        out_shape=jax.ShapeDtypeStruct((b, nh, d, s), F32),
        scratch_shapes=[pltpu.VMEM((nh, d, blk), F32)],
        compiler_params=_params("parallel", "parallel"),
        name="moba_attention",
    )(qt, k, vt, km)


def _hgrn_kernel(q_ref, f_ref, i_ref, g_ref, lb_ref, gn_ref, e_ref, tri_ref, ones_ref, hm_ref,
                 o_ref, st_ref, q_s, k_s, b_s, v_s, qd_s, ke_s, be_s, o_s, *, chunk):
    c = MIX_WIDTH
    rows = q_ref.shape[1]

    @pl.when(pl.program_id(1) == 0)
    def _():
        st_ref[...] = jnp.zeros_like(st_ref)

    lb = lb_ref[...]
    sig = jax.nn.sigmoid(f_ref[0])
    log_f = jnp.log(lb + (1.0 - lb) * sig)
    key = (1.0 - lb) * (1.0 - sig)
    b = _dot_f32_rhs(tri_ref[...], log_f)
    b_end = _dot_f32_rhs(ones_ref[...], log_f)
    q = q_ref[0]
    q_s[...] = q
    k_s[...] = key
    b_s[...] = b
    v_s[...] = i_ref[0]
    qd_s[...] = q * jnp.exp(b)
    ke_s[...] = key * jnp.exp(b_end - b)
    be_s[...] = b_end
    e = e_ref[...]
    head_mask = hm_ref[...]
    sub = lax.broadcasted_iota(jnp.int32, (chunk, c), 0)

    def body(n, carry):
        r0 = pl.multiple_of(n * chunk, chunk)
        sl = pl.ds(r0, chunk)
        qi = q_s[sl, :]
        ki = k_s[sl, :]
        bi = b_s[sl, :]
        vi = v_s[sl, :]
        parts = []
        for s in range(chunk):
            pr = qi * ki[s:s + 1, :] * jnp.exp(jnp.minimum(bi - bi[s:s + 1, :], 0.0))
            parts.append(jnp.where(sub >= s, pr, 0.0))
        sc = _dot_f32_lhs(jnp.concatenate(parts, axis=0), e)
        o = jnp.zeros((chunk, c), F32)
        for s in range(chunk):
            o = o + sc[s * chunk:(s + 1) * chunk, :] * vi[s:s + 1, :]
        st = st_ref[...]
        o = o + lax.dot_general(qd_s[sl, :].astype(BF16), st.astype(BF16),
                                (((1,), (1,)), ((), ())), preferred_element_type=F32)
        o_s[sl, :] = o
        kv = lax.dot_general(vi.astype(BF16), ke_s[sl, :].astype(BF16),
                             (((0,), (0,)), ((), ())), preferred_element_type=F32)
        st_ref[...] = st * jnp.exp(be_s[pl.ds(r0, 1), :]) + kv * head_mask
        return carry

    lax.fori_loop(0, rows // chunk, body, 0)
    o = o_s[...]
    ms = _dot_f32_lhs(o * o, e) * (1.0 / HEAD_DIM)
    g = g_ref[0]
    o_ref[0] = o * lax.rsqrt(ms + NORM_EPS) * gn_ref[...] * (g * jax.nn.sigmoid(g))


def _hgrn(hc, lb, gn_g, e):
    b, s, _ = hc.shape
    c = MIX_WIDTH
    t = SEQ_TILE
    ch = HGRN_CHUNK
    ridx = jnp.arange(t)
    same = (ridx[:, None] // ch) == (ridx[None, :] // ch)
    tri = (same & (ridx[None, :] <= ridx[:, None])).astype(BF16)
    ones = same.astype(BF16)
    seq = lambda n: pl.BlockSpec((1, t, c), lambda bi, j: (bi, j, n))
    row = pl.BlockSpec((1, c), lambda bi, j: (0, 0))
    mat = lambda n: pl.BlockSpec((n, n), lambda bi, j: (0, 0))
    return pl.pallas_call(
        functools.partial(_hgrn_kernel, chunk=ch),
        grid=(b, s // t),
        in_specs=[seq(0), seq(1), seq(2), seq(3), row, row, mat(c), mat(t), mat(t), mat(c)],
        out_specs=pl.BlockSpec((1, t, c), lambda bi, j: (bi, j, 0)),
        out_shape=jax.ShapeDtypeStruct((b, s, c), F32),
        scratch_shapes=[pltpu.VMEM((c, c), F32)] + [pltpu.VMEM((t, c), F32)] * 8,
        compiler_params=_params("parallel", "arbitrary"),
        name="hgrn2",
    )(hc, hc, hc, hc, lb.reshape(1, c), gn_g.reshape(1, c), e, tri, ones, e.astype(F32))


def _merge_kernel(x_ref, g_ref, wg_ref, y0_ref, y1_ref, y2_ref, y3_ref, wb_ref, wo_ref, o_ref):
    x = x_ref[...]
    d = x.shape[1]
    h = _rms(x, g_ref[...]).astype(BF16)
    merged = None
    for n, y_ref in enumerate((y0_ref, y1_ref, y2_ref, y3_ref)):
        gate = jax.nn.sigmoid(_dot(h, wg_ref[:, n * d:(n + 1) * d]))
        term = gate * _dot(y_ref[...].astype(BF16), wb_ref[n])
        merged = term if merged is None else merged + term
    o_ref[...] = x + _dot(merged.astype(BF16), wo_ref[...])


def _merge(x2, g, w_gate, ys, w_branch, w_out):
    m, d = x2.shape
    c = MIX_WIDTH
    tile = pl.BlockSpec((ROW_TILE, d), lambda i: (i, 0))
    ytile = pl.BlockSpec((ROW_TILE, c), lambda i: (i, 0))
    return pl.pallas_call(
        _merge_kernel,
        grid=(m // ROW_TILE,),
        in_specs=[tile, pl.BlockSpec((1, d), lambda i: (0, 0)),
                  pl.BlockSpec((d, N_BRANCH * d), lambda i: (0, 0)),
                  ytile, ytile, ytile, ytile,
                  pl.BlockSpec((N_BRANCH, c, d), lambda i: (0, 0, 0)),
                  pl.BlockSpec((d, d), lambda i: (0, 0))],
        out_specs=tile,
        out_shape=jax.ShapeDtypeStruct((m, d), F32),
        compiler_params=_params("parallel"),
        name="gated_merge",
    )(x2, g.reshape(1, d), w_gate, *ys, w_branch, w_out)


def _ffn_kernel(*refs, final):
    if final:
        x_ref, g_ref, wg_ref, wu_ref, wd_ref, fg_ref, o_ref, h_s, acc_s = refs
    else:
        x_ref, g_ref, wg_ref, wu_ref, wd_ref, o_ref, h_s, acc_s = refs
    f = pl.program_id(1)

    @pl.when(f == 0)
    def _():
        h_s[...] = _rms(x_ref[...], g_ref[...]).astype(BF16)
        acc_s[...] = jnp.zeros_like(acc_s)

    h = h_s[...]
    gate = _dot(h, wg_ref[...])
    act = gate * jax.nn.sigmoid(gate) * _dot(h, wu_ref[...])
    acc_s[...] += _dot(act.astype(BF16), wd_ref[...])

    @pl.when(f == pl.num_programs(1) - 1)
    def _():
        y = x_ref[...] + acc_s[...]
        o_ref[...] = _rms(y, fg_ref[...]) if final else y


def _ffn(x2, g, w_gate, w_up, w_down, final_g):
    m, d = x2.shape
    dff = w_gate.shape[1]
    tf = dff // 2
    assert tf % LANES == 0
    final = final_g is not None
    tile = pl.BlockSpec((ROW_TILE, d), lambda i, f: (i, 0))
    row = pl.BlockSpec((1, d), lambda i, f: (0, 0))
    in_specs = [tile, row,
                pl.BlockSpec((d, tf), lambda i, f: (0, f)),
                pl.BlockSpec((d, tf), lambda i, f: (0, f)),
                pl.BlockSpec((tf, d), lambda i, f: (f, 0))]
    args = [x2, g.reshape(1, d), w_gate, w_up, w_down]
    if final:
        in_specs.append(row)
        args.append(final_g.reshape(1, d))
    return pl.pallas_call(
        functools.partial(_ffn_kernel, final=final),
        grid=(m // ROW_TILE, dff // tf),
        in_specs=in_specs,
        out_specs=tile,
        out_shape=jax.ShapeDtypeStruct((m, d), F32),
        scratch_shapes=[pltpu.VMEM((ROW_TILE, d), BF16), pltpu.VMEM((ROW_TILE, d), F32)],
        compiler_params=_params("parallel", "arbitrary"),
        name="ffn",
    )(*args)


def _to_heads(t):
    b, s, _ = t.shape
    return t.reshape(b, s, N_HEADS, HEAD_DIM).transpose(0, 2, 1, 3)


def _to_heads_t(t):
    b, s, _ = t.shape
    return t.reshape(b, s, N_HEADS, HEAD_DIM).transpose(0, 2, 3, 1)


def _from_heads_t(t):
    b, _, _, s = t.shape
    return t.transpose(0, 3, 1, 2).reshape(b * s, MIX_WIDTH)


def _rope_tables(s):
    half = ROPE_DIM // 2
    inv_freq = ROPE_THETA ** (-jnp.arange(0, ROPE_DIM, 2, dtype=F32) / ROPE_DIM)
    ang = jnp.arange(s, dtype=F32)[:, None] * inv_freq[None, :]
    cos, sin = jnp.cos(ang), jnp.sin(ang)
    pad = HEAD_DIM - ROPE_DIM
    cos_h = jnp.concatenate([cos, cos, jnp.ones((s, pad), F32)], axis=1)
    sin_h = jnp.concatenate([-sin, sin, jnp.zeros((s, pad), F32)], axis=1)
    return jnp.tile(cos_h, (1, N_HEADS)), jnp.tile(sin_h, (1, N_HEADS))


def kernel(x, norm1_g, w_in, rwkv_mu, rwkv_w0, rwkv_w2, rwkv_a0, rwkv_a2, rwkv_g2, rwkv_k_k, rwkv_k_a, rwkv_r_k, rwkv_gn_w, rwkv_gn_b, rwkv_v0, rwkv_v1, rwkv_v2, hgrn_lb_logits, hgrn_gn_g, w_branch, w_out, norm2_g, w_ffn_gate, w_ffn_up, w_ffn_down, final_g):
    b, s, d = x.shape
    depth = w_in.shape[0]
    c = MIX_WIDTH
    m = b * s
    assert d == D_MODEL and s % SEQ_TILE == 0 and m % ROW_TILE == 0 and s % REC_TILE == 0
    assert LANES % (b * N_HEADS) == 0 and HEAD_DIM % (LANES // (b * N_HEADS)) == 0

    cos, sin = _rope_tables(s)
    lb_w = jax.nn.softmax(hgrn_lb_logits.astype(F32), axis=0)
    lower_bounds = jnp.cumsum(lb_w, axis=0) - lb_w[0]
    head_id = jnp.arange(c) // HEAD_DIM
    e = (head_id[:, None] == head_id[None, :]).astype(BF16)
    row = lambda t: t.reshape(1, -1)
    zpad = lambda t, top, bottom: jnp.pad(t, ((top, bottom), (0, 0)))

    x2 = x.reshape(m, d)
    v_first = None
    for layer in range(depth):
        w_l = w_in[layer].astype(BF16)
        g1 = norm1_g[layer]
        proj = lambda lo, width: _norm_proj(x2, g1, w_l[:, lo:lo + width])

        sb = proj(SB_OFF, SB_COLS).reshape(b, s, SB_COLS)
        q_sb = _to_heads_t(sb[..., :c] * (HEAD_DIM ** -0.5)).astype(BF16)
        y_sb = _sb_attention(q_sb, _to_heads(sb[..., c:2 * c]).astype(BF16),
                             _to_heads_t(sb[..., 2 * c:]).astype(BF16))
        y_sb = _from_heads_t(y_sb)

        prm = {
            "mu": row(rwkv_mu[layer]), "w0": row(rwkv_w0[layer]),
            "w2": zpad(rwkv_w2[layer], 0, LANES - RWKV_DECAY_LORA),
            "a0": row(rwkv_a0[layer]),
            "a2": zpad(rwkv_a2[layer], RWKV_DECAY_LORA, LANES - RWKV_DECAY_LORA - RWKV_AAA_LORA),
            "g2": rwkv_g2[layer], "k_k": row(rwkv_k_k[layer]), "k_a": row(rwkv_k_a[layer]),
            "r_k": row(rwkv_r_k[layer]),
        }
        if layer > 0:
            prm["v0"] = row(rwkv_v0[layer - 1])
            prm["v1"] = jnp.pad(rwkv_v1[layer - 1], ((0, 0), (0, LANES - RWKV_MV_LORA)))
            prm["v2"] = zpad(rwkv_v2[layer - 1], 0, LANES - RWKV_MV_LORA)
        rp = proj(RWKV_OFF, RWKV_COLS).reshape(b, s, RWKV_COLS)
        r_, w_, k_, v_, an_, bb_, g_, bonus_ = _rwkv_prep(rp, prm, e, v_first)
        if layer == 0:
            v_first = v_
        y_rec = _rwkv_recurrence(r_, w_, k_, v_, an_, bb_)
        y_rwkv = _rwkv_post(y_rec.reshape(m, c), g_.reshape(m, c), bonus_.reshape(m, c),
                            row(rwkv_gn_w[layer]), row(rwkv_gn_b[layer]), e)

        mc = proj(MOBA_OFF, MOBA_COLS).reshape(b, s, MOBA_COLS)
        q_m, k_m, k_mean = _rope(mc, cos, sin)
        n_blk = k_mean.shape[1]
        k_mean = k_mean.reshape(b, n_blk, N_HEADS, HEAD_DIM).transpose(0, 2, 1, 3)
        k_mean = jnp.pad(k_mean, ((0, 0), (0, 0), (0, -n_blk % 8), (0, 0)))
        y_moba = _moba_attention(_to_heads_t(q_m), _to_heads(k_m).astype(BF16),
                                 _to_heads_t(mc[..., 2 * c:]).astype(BF16), k_mean)
        y_moba = _from_heads_t(y_moba)

        hc = proj(HGRN_OFF, HGRN_COLS).reshape(b, s, HGRN_COLS)
        y_hgrn = _hgrn(hc, lower_bounds[layer], hgrn_gn_g[layer], e).reshape(m, c)

        x2 = _merge(x2, g1, w_l[:, GATE_OFF:], (y_sb, y_rwkv, y_moba, y_hgrn),
                    w_branch[layer].astype(BF16), w_out[layer].astype(BF16))
        x2 = _ffn(x2, norm2_g[layer], w_ffn_gate[layer].astype(BF16), w_ffn_up[layer].astype(BF16),
                  w_ffn_down[layer].astype(BF16), final_g if layer == depth - 1 else None)
    return x2.reshape(b, s, d)
```

```python
import functools

import jax
import jax.numpy as jnp
from jax import lax
from jax.experimental import pallas as pl
from jax.experimental.pallas import tpu as pltpu

F32 = jnp.float32
BF16 = jnp.bfloat16

D_MODEL = 1024
HEAD_DIM = 64
N_HEADS = 4
MIX_WIDTH = N_HEADS * HEAD_DIM
N_BRANCH = 4
NORM_EPS = 1e-6
MASK_VALUE = -1e30

RWKV_DECAY_LORA = 64
RWKV_AAA_LORA = 64
RWKV_MV_LORA = 32
RWKV_GATE_LORA = 128
RWKV_GN_EPS = 64e-5
RWKV_COLS = 3 * MIX_WIDTH + RWKV_DECAY_LORA + RWKV_AAA_LORA + RWKV_GATE_LORA

MOBA_BLOCK = 256
MOBA_TOPK = 3
ROPE_THETA = 500000.0
ROPE_DIM = HEAD_DIM // 4

HGRN_CHUNK = 16

SB_COLS = 3 * MIX_WIDTH
MOBA_COLS = 3 * MIX_WIDTH
HGRN_COLS = 4 * MIX_WIDTH
SB_OFF = 0
RWKV_OFF = SB_OFF + SB_COLS
MOBA_OFF = RWKV_OFF + RWKV_COLS
HGRN_OFF = MOBA_OFF + MOBA_COLS
GATE_OFF = HGRN_OFF + HGRN_COLS

LANES = 128
VMEM_LIMIT = 56 * 1024 * 1024

ROW_TILE = 512
SEQ_TILE = 256
REC_TILE = 128


def _params(*sem):
    return pltpu.CompilerParams(dimension_semantics=sem, vmem_limit_bytes=VMEM_LIMIT)


def _dot(a, b):
    return jnp.dot(a, b, preferred_element_type=F32)


def _split2(a):
    hi = a.astype(BF16)
    lo = (a - hi.astype(F32)).astype(BF16)
    return hi, lo


def _split3(a):
    hi = a.astype(BF16)
    r1 = a - hi.astype(F32)
    mid = r1.astype(BF16)
    lo = (r1 - mid.astype(F32)).astype(BF16)
    return hi, mid, lo


def _dot_f32_lhs(a, m):
    hi, lo = _split2(a)
    return _dot(hi, m) + _dot(lo, m)


def _dot_f32_rhs(m, a):
    hi, mid, lo = _split3(a)
    return _dot(m, hi) + _dot(m, mid) + _dot(m, lo)


def _dot3(a, b):
    ah, al = _split2(a)
    bh, bl = _split2(b)
    return _dot(ah, bh) + _dot(ah, bl) + _dot(al, bh)


def _rms(x, g):
    return x * lax.rsqrt(jnp.mean(x * x, axis=-1, keepdims=True) + NORM_EPS) * g


def _norm_proj_kernel(x_ref, g_ref, w_ref, o_ref):
    h = _rms(x_ref[...], g_ref[...])
    o_ref[...] = _dot(h.astype(BF16), w_ref[...])


def _norm_proj(x2, g, w):
    m, d = x2.shape
    n = w.shape[1]
    return pl.pallas_call(
        _norm_proj_kernel,
        grid=(m // ROW_TILE,),
        in_specs=[
            pl.BlockSpec((ROW_TILE, d), lambda i: (i, 0)),
            pl.BlockSpec((1, d), lambda i: (0, 0)),
            pl.BlockSpec((d, n), lambda i: (0, 0)),
        ],
        out_specs=pl.BlockSpec((ROW_TILE, n), lambda i: (i, 0)),
        out_shape=jax.ShapeDtypeStruct((m, n), F32),
        compiler_params=_params("parallel"),
        name="norm_proj",
    )(x2, g.reshape(1, d), w)


_NT = (((1,), (1,)), ((), ()))


def _attn_proj_kernel(*refs, rope):
    if rope:
        (x_ref, g_ref, wk_ref, wqt_ref, wvt_ref, cos_ref, sin_ref, cost_ref, sint_ref,
         k_o, qt_o, vt_o, km_o) = refs
    else:
        x_ref, g_ref, wk_ref, wqt_ref, wvt_ref, k_o, qt_o, vt_o = refs
    h = _rms(x_ref[0], g_ref[...]).astype(BF16)
    k = _dot(h, wk_ref[...])
    qt = lax.dot_general(wqt_ref[...], h, _NT, preferred_element_type=F32)
    vt_o[0] = lax.dot_general(wvt_ref[...], h, _NT, preferred_element_type=F32).astype(BF16)
    if rope:
        c = k.shape[1]
        half = ROPE_DIM // 2
        lane = lax.broadcasted_iota(jnp.int32, k.shape, 1) % HEAD_DIM
        partner = jnp.where(lane < half, pltpu.roll(k, c - half, axis=1), pltpu.roll(k, half, axis=1))
        k = k * cos_ref[...] + partner * sin_ref[...]
        chan = lax.broadcasted_iota(jnp.int32, qt.shape, 0) % HEAD_DIM
        partner = jnp.where(chan < half, pltpu.roll(qt, c - half, axis=0), pltpu.roll(qt, half, axis=0))
        qt_o[0] = qt * cost_ref[...] + partner * sint_ref[...]
        km_o[0, 0] = jnp.mean(k, axis=0, keepdims=True)
    else:
        qt_o[0] = qt.astype(BF16)
    k_o[0] = k.astype(BF16)


def _attn_proj(x3, g, wk, wqt, wvt, rope_tables=None):
    b, s, d = x3.shape
    c = MIX_WIDTH
    t = MOBA_BLOCK
    rope = rope_tables is not None
    full = lambda m, n: pl.BlockSpec((m, n), lambda bi, j: (0, 0))
    rows = pl.BlockSpec((1, t, c), lambda bi, j: (bi, j, 0))
    cols = pl.BlockSpec((1, c, t), lambda bi, j: (bi, 0, j))
    in_specs = [pl.BlockSpec((1, t, d), lambda bi, j: (bi, j, 0)), full(1, d),
                full(d, c), full(c, d), full(c, d)]
    args = [x3, g.reshape(1, d), wk, wqt, wvt]
    out_specs = [rows, cols, cols]
    out_shape = [jax.ShapeDtypeStruct((b, s, c), BF16),
                 jax.ShapeDtypeStruct((b, c, s), F32 if rope else BF16),
                 jax.ShapeDtypeStruct((b, c, s), BF16)]
    if rope:
        cos, sin = rope_tables
        in_specs += [pl.BlockSpec((t, c), lambda bi, j: (j, 0))] * 2
        in_specs += [pl.BlockSpec((c, t), lambda bi, j: (0, j))] * 2
        args += [cos, sin, cos.T, sin.T]
        out_specs.append(pl.BlockSpec((1, 1, 1, c), lambda bi, j: (bi, j, 0, 0)))
        out_shape.append(jax.ShapeDtypeStruct((b, s // t, 1, c), F32))
    return pl.pallas_call(
        functools.partial(_attn_proj_kernel, rope=rope),
        grid=(b, s // t),
        in_specs=in_specs,
        out_specs=out_specs,
        out_shape=out_shape,
        compiler_params=_params("parallel", "parallel"),
        name="rope_proj" if rope else "attn_proj",
    )(*args)


SB_DEAD_LOG = -110.0


def _sb_kernel(qt_ref, k_ref, vt_ref, ut_ref, o_ref, acc_s, *, tq, n_heads):
    i = pl.program_id(1)
    ut = ut_ref[...]
    key = lax.broadcasted_iota(jnp.int32, (tq, tq), 0)
    qry = lax.broadcasted_iota(jnp.int32, (tq, tq), 1)
    past = key < qry
    qt = qt_ref[0]
    chan_head = lax.broadcasted_iota(jnp.int32, qt.shape, 0) // HEAD_DIM
    qh = [jnp.where(chan_head == h, qt, jnp.zeros_like(qt)) for h in range(n_heads)]
    hs = [slice(h * HEAD_DIM, (h + 1) * HEAD_DIM) for h in range(n_heads)]

    def block(j, runs, diagonal):
        start = pl.multiple_of(j * tq, tq)
        heads = range(n_heads)
        kb = k_ref[0, pl.ds(start, tq), :]
        zs = [_dot(kb, qh[h]) for h in heads]
        lks, lbs, pieces = [], [], []
        for h in heads:
            z = zs[h]
            sp = jnp.maximum(z, 0.0) + jnp.log(1.0 + jnp.exp(-jnp.abs(z)))
            lk = -sp
            if diagonal:
                lk = jnp.where(past, lk, 0.0)
            lks.append(lk)
            lbs.append(z - sp)
            pieces.append(_split2(lk))
        betweens = [_dot(ut, hi) + _dot(ut, lo) for hi, lo in pieces]
        atts = []
        for h in heads:
            att = jnp.exp(lbs[h] + betweens[h] + runs[h])
            if diagonal:
                att = jnp.where(past, att, 0.0)
            atts.append(att.astype(BF16))
        for h in heads:
            part = _dot(vt_ref[0, hs[h], pl.ds(start, tq)], atts[h])
            acc_s[hs[h], :] = part if diagonal else acc_s[hs[h], :] + part
        return tuple(runs[h] + jnp.sum(lks[h], axis=0, keepdims=True) for h in heads)

    def live(runs):
        top = functools.reduce(jnp.maximum, [jnp.max(r) for r in runs])
        return (top > SB_DEAD_LOG).astype(jnp.int32)

    runs = block(i, tuple(jnp.zeros((1, tq), F32) for _ in range(n_heads)), True)

    def cond(c):
        jj, go, _ = c
        return jnp.logical_and(jj < i, go > 0)

    def body(c):
        jj, _, runs = c
        runs = block(i - 1 - jj, runs, False)
        return jj + 1, live(runs), runs

    lax.while_loop(cond, body, (jnp.int32(0), live(runs), runs))
    o_ref[0] = acc_s[...].T


def _sb_attention(qt, k, vt):
    b, s, c = k.shape
    tq = SEQ_TILE
    ut = (jnp.arange(tq)[None, :] > jnp.arange(tq)[:, None]).astype(BF16)
    return pl.pallas_call(
        functools.partial(_sb_kernel, tq=tq, n_heads=N_HEADS),
        grid=(b, s // tq),
        in_specs=[
            pl.BlockSpec((1, c, tq), lambda bi, i: (bi, 0, i)),
            pl.BlockSpec((1, s, c), lambda bi, i: (bi, 0, 0)),
            pl.BlockSpec((1, c, s), lambda bi, i: (bi, 0, 0)),
            pl.BlockSpec((tq, tq), lambda bi, i: (0, 0)),
        ],
        out_specs=pl.BlockSpec((1, tq, c), lambda bi, i: (bi, i, 0)),
        out_shape=jax.ShapeDtypeStruct((b, s, c), F32),
        scratch_shapes=[pltpu.VMEM((c, tq), F32)],
        compiler_params=_params("parallel", "parallel"),
        name="sb_attention",
    )(qt, k, vt, ut)


def _rwkv_prep_kernel(*refs, has_vres):
    if has_vres:
        (p_ref, mu_ref, w0_ref, w2_ref, a0_ref, a2_ref, g2_ref, kk_ref, ka_ref, rk_ref, e_ref,
         vf_ref, v0_ref, v1_ref, v2_ref,
         r_o, w_o, k_o, v_o, an_o, bb_o, g_o, bonus_o, carry) = refs
    else:
        (p_ref, mu_ref, w0_ref, w2_ref, a0_ref, a2_ref, g2_ref, kk_ref, ka_ref, rk_ref, e_ref,
         r_o, w_o, k_o, v_o, an_o, bb_o, g_o, bonus_o, carry) = refs
    c = MIX_WIDTH
    p = p_ref[0]
    t = p.shape[0]

    @pl.when(pl.program_id(1) == 0)
    def _():
        carry[...] = jnp.zeros_like(carry)

    rowid = lax.broadcasted_iota(jnp.int32, p.shape, 0)
    prev = jnp.where(rowid == 0, carry[...], pltpu.roll(p, 1, axis=0))
    carry[...] = p[t - 1:t, :]
    xs = p + (prev - p) * mu_ref[...]

    r = xs[:, :c]
    k = xs[:, c:2 * c]
    v = xs[:, 2 * c:3 * c]
    lora = xs[:, 3 * c:3 * c + LANES]
    gl = xs[:, 3 * c + LANES:]
    w_log = -jax.nn.softplus(-(w0_ref[...] + _dot3(jnp.tanh(lora), w2_ref[...]))) - 0.5
    decay = jnp.exp(-jnp.exp(w_log))
    if has_vres:
        mix = jax.nn.sigmoid(v0_ref[...] + _dot3(_dot3(v, v1_ref[...]), v2_ref[...]))
        v = v + (vf_ref[0] - v) * mix
    a = jax.nn.sigmoid(a0_ref[...] + _dot3(lora, a2_ref[...]))
    g = _dot3(jax.nn.sigmoid(gl), g2_ref[...])
    e = e_ref[...]
    kk = k * kk_ref[...]
    kk = kk / jnp.maximum(jnp.sqrt(_dot_f32_lhs(kk * kk, e)), 1e-12)
    k = k * (1.0 + (a - 1.0) * ka_ref[...])
    bonus = _dot_f32_lhs(r * k * rk_ref[...], e) * v
    r_o[0] = r
    w_o[0] = decay
    k_o[0] = k
    v_o[0] = v
    an_o[0] = -kk
    bb_o[0] = kk * a
    g_o[0] = g
    bonus_o[0] = bonus


def _rwkv_prep(p, prm, e, v_first):
    b, s, cols = p.shape
    c = MIX_WIDTH
    t = SEQ_TILE
    has_vres = v_first is not None
    row = lambda n: pl.BlockSpec((1, n), lambda bi, j: (0, 0))
    mat = lambda m, n: pl.BlockSpec((m, n), lambda bi, j: (0, 0))
    seq = lambda n: pl.BlockSpec((1, t, n), lambda bi, j: (bi, j, 0))
    in_specs = [seq(cols), row(cols), row(c), mat(LANES, c), row(c), mat(LANES, c),
                mat(LANES, c), row(c), row(c), row(c), mat(c, c)]
    args = [p, prm["mu"], prm["w0"], prm["w2"], prm["a0"], prm["a2"], prm["g2"],
            prm["k_k"], prm["k_a"], prm["r_k"], e]
    if has_vres:
        in_specs += [seq(c), row(c), mat(c, LANES), mat(LANES, c)]
        args += [v_first, prm["v0"], prm["v1"], prm["v2"]]
    out = jax.ShapeDtypeStruct((b, s, c), F32)
    return pl.pallas_call(
        functools.partial(_rwkv_prep_kernel, has_vres=has_vres),
        grid=(b, s // t),
        in_specs=in_specs,
        out_specs=[seq(c)] * 8,
        out_shape=[out] * 8,
        scratch_shapes=[pltpu.VMEM((1, cols), F32)],
        compiler_params=_params("parallel", "arbitrary"),
        name="rwkv_prep",
    )(*args)


def _rwkv_rec_kernel(an_ref, w_ref, bb_ref, k_ref, r_ref, v_ref, y_ref, st_ref, *, n_tiles, group):
    @pl.when(pl.program_id(0) == 0)
    def _():
        st_ref[...] = jnp.zeros_like(st_ref)

    lane = lax.broadcasted_iota(jnp.int32, (HEAD_DIM, LANES), 1)
    per_token = LANES // group

    def body(tp, carry):
        for u in range(group):
            idx = u * per_token + lane // group
            spread = lambda ref: jnp.take_along_axis(ref[tp], idx, axis=1)
            a, w, bb, k, r = (spread(ref) for ref in (an_ref, w_ref, bb_ref, k_ref, r_ref))
            t = tp * group + u
            vt = v_ref[t]
            rows = []
            for j in range(n_tiles):
                st = st_ref[j]
                sa = jnp.sum(st * a, axis=0, keepdims=True)
                st = st * w + sa * bb + vt[j:j + 1, :] * k
                st_ref[j] = st
                rows.append(jnp.sum(st * r, axis=0, keepdims=True))
            y_ref[t] = jnp.concatenate(rows, axis=0)
        return carry

    lax.fori_loop(0, an_ref.shape[0], body, 0)


def _rwkv_recurrence(r, w, k, v, an, bb):
    b, s, c = r.shape
    group = LANES // (b * N_HEADS)
    n_tiles = HEAD_DIM // group

    def pack(x):
        x = x.reshape(b, s // group, group, N_HEADS, HEAD_DIM).transpose(1, 4, 2, 0, 3)
        return x.reshape(s // group, HEAD_DIM, LANES)

    ve = v.reshape(b, s, N_HEADS, n_tiles, group).transpose(1, 3, 0, 2, 4).reshape(s, n_tiles, LANES)
    big = pl.BlockSpec((REC_TILE // group, HEAD_DIM, LANES), lambda i: (i, 0, 0))
    small = pl.BlockSpec((REC_TILE, n_tiles, LANES), lambda i: (i, 0, 0))
    y = pl.pallas_call(
        functools.partial(_rwkv_rec_kernel, n_tiles=n_tiles, group=group),
        grid=(s // REC_TILE,),
        in_specs=[big] * 5 + [small],
        out_specs=small,
        out_shape=jax.ShapeDtypeStruct((s, n_tiles, LANES), F32),
        scratch_shapes=[pltpu.VMEM((n_tiles, HEAD_DIM, LANES), F32)],
        compiler_params=_params("arbitrary"),
        name="rwkv_recurrence",
    )(pack(an), pack(w), pack(bb), pack(k), pack(r), ve)
    y = y.reshape(s, n_tiles, b, N_HEADS, group).transpose(2, 0, 3, 1, 4)
    return y.reshape(b, s, c)


def _rwkv_post_kernel(y_ref, g_ref, bonus_ref, gw_ref, gb_ref, e_ref, o_ref):
    y = y_ref[...]
    e = e_ref[...]
    mean = _dot_f32_lhs(y, e) * (1.0 / HEAD_DIM)
    yc = y - mean
    var = _dot_f32_lhs(yc * yc, e) * (1.0 / HEAD_DIM)
    yn = yc * lax.rsqrt(var + RWKV_GN_EPS) * gw_ref[...] + gb_ref[...]
    o_ref[...] = (yn + bonus_ref[...]) * g_ref[...]


def _rwkv_post(y, g, bonus, gn_w, gn_b, e):
    m, c = y.shape
    tile = pl.BlockSpec((ROW_TILE, c), lambda i: (i, 0))
    row = pl.BlockSpec((1, c), lambda i: (0, 0))
    return pl.pallas_call(
        _rwkv_post_kernel,
        grid=(m // ROW_TILE,),
        in_specs=[tile, tile, tile, row, row, pl.BlockSpec((c, c), lambda i: (0, 0))],
        out_specs=tile,
        out_shape=jax.ShapeDtypeStruct((m, c), F32),
        compiler_params=_params("parallel"),
        name="rwkv_post",
    )(y, g, bonus, gn_w, gn_b, e)


def _moba_kernel(qt_ref, k_ref, vt_ref, km_ref, o_ref, acc_s, *, blk, n_sel, n_heads):
    i = pl.program_id(1)
    n_pad = km_ref.shape[1]
    blk_id = lax.broadcasted_iota(jnp.int32, (n_pad, blk), 0)
    fully_past = blk_id < i
    weight = jnp.where(blk_id < 24, jnp.left_shift(1, jnp.minimum(blk_id, 23)), 0).astype(F32)
    key = lax.broadcasted_iota(jnp.int32, (blk, blk), 0)
    qry = lax.broadcasted_iota(jnp.int32, (blk, blk), 1)
    causal = key <= qry
    qt_all = qt_ref[0]
    km_all = km_ref[0]
    chan_head = lax.broadcasted_iota(jnp.int32, qt_all.shape, 0) // HEAD_DIM
    lane_head = lax.broadcasted_iota(jnp.int32, km_all.shape, 1) // HEAD_DIM
    q_scaled = (qt_all * (HEAD_DIM ** -0.5)).astype(BF16)
    hs = [slice(h * HEAD_DIM, (h + 1) * HEAD_DIM) for h in range(n_heads)]

    qs, bits = [], []
    for h in range(n_heads):
        gate = jnp.dot(jnp.where(lane_head == h, km_all, 0.0), qt_all, preferred_element_type=F32,
                       precision=lax.Precision.HIGHEST)
        gate = jnp.where(fully_past, gate, MASK_VALUE)
        chosen = jnp.zeros(gate.shape, F32)
        for _ in range(n_sel):
            top = jnp.max(gate, axis=0, keepdims=True)
            idx = jnp.min(jnp.where(gate == top, blk_id, n_pad), axis=0, keepdims=True)
            pick = blk_id == idx
            chosen = jnp.where(pick & fully_past, 1.0, chosen)
            gate = jnp.where(pick, -jnp.inf, gate)
        bits.append(jnp.sum(chosen * weight, axis=0, keepdims=True).astype(jnp.int32))
        qs.append(jnp.where(chan_head == h, q_scaled, jnp.zeros_like(q_scaled)))

    def scores(n, h):
        start = pl.multiple_of(n * blk, blk)
        return _dot(k_ref[0, pl.ds(start, blk), :], qs[h]), vt_ref[0, hs[h], pl.ds(start, blk)]

    heads = range(n_heads)
    ms, ls, ps = [], [], []
    own = [scores(i, h) for h in heads]
    for h in heads:
        s = jnp.where(causal, own[h][0], MASK_VALUE)
        m = jnp.max(s, axis=0, keepdims=True)
        p = jnp.exp(s - m)
        ms.append(m)
        ls.append(jnp.sum(p, axis=0, keepdims=True))
        ps.append(p.astype(BF16))
    for h in heads:
        acc_s[hs[h], :] = _dot(own[h][1], ps[h])

    def body(n, carry):
        ms, ls = carry
        blocks = [scores(n, h) for h in heads]
        new_ms, new_ls, alphas, ps = [], [], [], []
        for h in heads:
            sel = (jnp.right_shift(bits[h], n) & 1) == 1
            s = jnp.where(sel, blocks[h][0], MASK_VALUE)
            m_new = jnp.maximum(ms[h], jnp.max(s, axis=0, keepdims=True))
            alpha = jnp.exp(ms[h] - m_new)
            p = jnp.exp(s - m_new)
            new_ms.append(m_new)
            new_ls.append(alpha * ls[h] + jnp.sum(p, axis=0, keepdims=True))
            alphas.append(alpha)
            ps.append(p.astype(BF16))
        for h in heads:
            acc_s[hs[h], :] = alphas[h] * acc_s[hs[h], :] + _dot(blocks[h][1], ps[h])
        return tuple(new_ms), tuple(new_ls)

    ms, ls = lax.fori_loop(0, i, body, (tuple(ms), tuple(ls)))
    for h in range(n_heads):
        acc_s[hs[h], :] = acc_s[hs[h], :] / ls[h]
    o_ref[0] = acc_s[...].T


def _moba_attention(qt, k, vt, km):
    b, s, c = k.shape
    blk = MOBA_BLOCK
    n_blk = s // blk
    n_pad = km.shape[1]
    assert s % blk == 0 and n_blk <= 24
    n_sel = min(MOBA_TOPK, n_blk)
    return pl.pallas_call(
        functools.partial(_moba_kernel, blk=blk, n_sel=n_sel, n_heads=N_HEADS),
        grid=(b, n_blk),
        in_specs=[
            pl.BlockSpec((1, c, blk), lambda bi, i: (bi, 0, i)),
            pl.BlockSpec((1, s, c), lambda bi, i: (bi, 0, 0)),
            pl.BlockSpec((1, c, s), lambda bi, i: (bi, 0, 0)),
            pl.BlockSpec((1, n_pad, c), lambda bi, i: (bi, 0, 0)),
        ],
        out_specs=pl.BlockSpec((1, blk, c), lambda bi, i: (bi, i, 0)),
        out_shape=jax.ShapeDtypeStruct((b, s, c), F32),
        scratch_shapes=[pltpu.VMEM((c, blk), F32)],
        compiler_params=_params("parallel", "parallel"),
        name="moba_attention",
    )(qt, k, vt, km)


def _hgrn_kernel(q_ref, f_ref, i_ref, g_ref, lb_ref, gn_ref, e_ref, tri_ref, ones_ref, hm_ref,
                 o_ref, st_ref, q_s, k_s, b_s, v_s, qd_s, ke_s, be_s, o_s, *, chunk):
    c = MIX_WIDTH
    rows = q_ref.shape[1]

    @pl.when(pl.program_id(1) == 0)
    def _():
        st_ref[...] = jnp.zeros_like(st_ref)

    lb = lb_ref[...]
    sig = jax.nn.sigmoid(f_ref[0])
    log_f = jnp.log(lb + (1.0 - lb) * sig)
    key = (1.0 - lb) * (1.0 - sig)
    b = _dot_f32_rhs(tri_ref[...], log_f)
    b_end = _dot_f32_rhs(ones_ref[...], log_f)
    q = q_ref[0]
    q_s[...] = q
    k_s[...] = key
    b_s[...] = b
    v_s[...] = i_ref[0]
    qd_s[...] = q * jnp.exp(b)
    ke_s[...] = key * jnp.exp(b_end - b)
    be_s[...] = b_end
    e = e_ref[...]
    head_mask = hm_ref[...]
    sub = lax.broadcasted_iota(jnp.int32, (chunk, c), 0)

    def body(n, carry):
        r0 = pl.multiple_of(n * chunk, chunk)
        sl = pl.ds(r0, chunk)
        qi = q_s[sl, :]
        ki = k_s[sl, :]
        bi = b_s[sl, :]
        vi = v_s[sl, :]
        parts = []
        for s in range(chunk):
            pr = qi * ki[s:s + 1, :] * jnp.exp(jnp.minimum(bi - bi[s:s + 1, :], 0.0))
            parts.append(jnp.where(sub >= s, pr, 0.0))
        sc = _dot_f32_lhs(jnp.concatenate(parts, axis=0), e)
        o = jnp.zeros((chunk, c), F32)
        for s in range(chunk):
            o = o + sc[s * chunk:(s + 1) * chunk, :] * vi[s:s + 1, :]
        st = st_ref[...]
        o = o + lax.dot_general(qd_s[sl, :].astype(BF16), st.astype(BF16),
                                (((1,), (1,)), ((), ())), preferred_element_type=F32)
        o_s[sl, :] = o
        kv = lax.dot_general(vi.astype(BF16), ke_s[sl, :].astype(BF16),
                             (((0,), (0,)), ((), ())), preferred_element_type=F32)
        st_ref[...] = st * jnp.exp(be_s[pl.ds(r0, 1), :]) + kv * head_mask
        return carry

    lax.fori_loop(0, rows // chunk, body, 0)
    o = o_s[...]
    ms = _dot_f32_lhs(o * o, e) * (1.0 / HEAD_DIM)
    g = g_ref[0]
    o_ref[0] = o * lax.rsqrt(ms + NORM_EPS) * gn_ref[...] * (g * jax.nn.sigmoid(g))


def _hgrn(hc, lb, gn_g, e):
    b, s, _ = hc.shape
    c = MIX_WIDTH
    t = SEQ_TILE
    ch = HGRN_CHUNK
    ridx = jnp.arange(t)
    same = (ridx[:, None] // ch) == (ridx[None, :] // ch)
    tri = (same & (ridx[None, :] <= ridx[:, None])).astype(BF16)
    ones = same.astype(BF16)
    seq = lambda n: pl.BlockSpec((1, t, c), lambda bi, j: (bi, j, n))
    row = pl.BlockSpec((1, c), lambda bi, j: (0, 0))
    mat = lambda n: pl.BlockSpec((n, n), lambda bi, j: (0, 0))
    return pl.pallas_call(
        functools.partial(_hgrn_kernel, chunk=ch),
        grid=(b, s // t),
        in_specs=[seq(0), seq(1), seq(2), seq(3), row, row, mat(c), mat(t), mat(t), mat(c)],
        out_specs=pl.BlockSpec((1, t, c), lambda bi, j: (bi, j, 0)),
        out_shape=jax.ShapeDtypeStruct((b, s, c), F32),
        scratch_shapes=[pltpu.VMEM((c, c), F32)] + [pltpu.VMEM((t, c), F32)] * 8,
        compiler_params=_params("parallel", "arbitrary"),
        name="hgrn2",
    )(hc, hc, hc, hc, lb.reshape(1, c), gn_g.reshape(1, c), e, tri, ones, e.astype(F32))


def _merge_kernel(x_ref, g_ref, wg_ref, y0_ref, y1_ref, y2_ref, y3_ref, wb_ref, wo_ref, o_ref):
    x = x_ref[...]
    d = x.shape[1]
    h = _rms(x, g_ref[...]).astype(BF16)
    merged = None
    for n, y_ref in enumerate((y0_ref, y1_ref, y2_ref, y3_ref)):
        gate = jax.nn.sigmoid(_dot(h, wg_ref[:, n * d:(n + 1) * d]))
        term = gate * _dot(y_ref[...].astype(BF16), wb_ref[n])
        merged = term if merged is None else merged + term
    o_ref[...] = x + _dot(merged.astype(BF16), wo_ref[...])


def _merge(x2, g, w_gate, ys, w_branch, w_out):
    m, d = x2.shape
    c = MIX_WIDTH
    tile = pl.BlockSpec((ROW_TILE, d), lambda i: (i, 0))
    ytile = pl.BlockSpec((ROW_TILE, c), lambda i: (i, 0))
    return pl.pallas_call(
        _merge_kernel,
        grid=(m // ROW_TILE,),
        in_specs=[tile, pl.BlockSpec((1, d), lambda i: (0, 0)),
                  pl.BlockSpec((d, N_BRANCH * d), lambda i: (0, 0)),
                  ytile, ytile, ytile, ytile,
                  pl.BlockSpec((N_BRANCH, c, d), lambda i: (0, 0, 0)),
                  pl.BlockSpec((d, d), lambda i: (0, 0))],
        out_specs=tile,
        out_shape=jax.ShapeDtypeStruct((m, d), F32),
        compiler_params=_params("parallel"),
        name="gated_merge",
    )(x2, g.reshape(1, d), w_gate, *ys, w_branch, w_out)


def _ffn_kernel(*refs, final):
    if final:
        x_ref, g_ref, wg_ref, wu_ref, wd_ref, fg_ref, o_ref, h_s, acc_s = refs
    else:
        x_ref, g_ref, wg_ref, wu_ref, wd_ref, o_ref, h_s, acc_s = refs
    f = pl.program_id(1)

    @pl.when(f == 0)
    def _():
        h_s[...] = _rms(x_ref[...], g_ref[...]).astype(BF16)
        acc_s[...] = jnp.zeros_like(acc_s)

    h = h_s[...]
    gate = _dot(h, wg_ref[...])
    act = gate * jax.nn.sigmoid(gate) * _dot(h, wu_ref[...])
    acc_s[...] += _dot(act.astype(BF16), wd_ref[...])

    @pl.when(f == pl.num_programs(1) - 1)
    def _():
        y = x_ref[...] + acc_s[...]
        o_ref[...] = _rms(y, fg_ref[...]) if final else y


def _ffn(x2, g, w_gate, w_up, w_down, final_g):
    m, d = x2.shape
    dff = w_gate.shape[1]
    tf = dff // 2
    assert tf % LANES == 0
    final = final_g is not None
    tile = pl.BlockSpec((ROW_TILE, d), lambda i, f: (i, 0))
    row = pl.BlockSpec((1, d), lambda i, f: (0, 0))
    in_specs = [tile, row,
                pl.BlockSpec((d, tf), lambda i, f: (0, f)),
                pl.BlockSpec((d, tf), lambda i, f: (0, f)),
                pl.BlockSpec((tf, d), lambda i, f: (f, 0))]
    args = [x2, g.reshape(1, d), w_gate, w_up, w_down]
    if final:
        in_specs.append(row)
        args.append(final_g.reshape(1, d))
    return pl.pallas_call(
        functools.partial(_ffn_kernel, final=final),
        grid=(m // ROW_TILE, dff // tf),
        in_specs=in_specs,
        out_specs=tile,
        out_shape=jax.ShapeDtypeStruct((m, d), F32),
        scratch_shapes=[pltpu.VMEM((ROW_TILE, d), BF16), pltpu.VMEM((ROW_TILE, d), F32)],
        compiler_params=_params("parallel", "arbitrary"),
        name="ffn",
    )(*args)


def _rope_tables(s):
    half = ROPE_DIM // 2
    inv_freq = ROPE_THETA ** (-jnp.arange(0, ROPE_DIM, 2, dtype=F32) / ROPE_DIM)
    ang = jnp.arange(s, dtype=F32)[:, None] * inv_freq[None, :]
    cos, sin = jnp.cos(ang), jnp.sin(ang)
    pad = HEAD_DIM - ROPE_DIM
    cos_h = jnp.concatenate([cos, cos, jnp.ones((s, pad), F32)], axis=1)
    sin_h = jnp.concatenate([-sin, sin, jnp.zeros((s, pad), F32)], axis=1)
    return jnp.tile(cos_h, (1, N_HEADS)), jnp.tile(sin_h, (1, N_HEADS))


def kernel(x, norm1_g, w_in, rwkv_mu, rwkv_w0, rwkv_w2, rwkv_a0, rwkv_a2, rwkv_g2, rwkv_k_k, rwkv_k_a, rwkv_r_k, rwkv_gn_w, rwkv_gn_b, rwkv_v0, rwkv_v1, rwkv_v2, hgrn_lb_logits, hgrn_gn_g, w_branch, w_out, norm2_g, w_ffn_gate, w_ffn_up, w_ffn_down, final_g):
    b, s, d = x.shape
    depth = w_in.shape[0]
    c = MIX_WIDTH
    m = b * s
    assert d == D_MODEL and s % SEQ_TILE == 0 and m % ROW_TILE == 0 and s % REC_TILE == 0
    assert LANES % (b * N_HEADS) == 0 and HEAD_DIM % (LANES // (b * N_HEADS)) == 0

    cos, sin = _rope_tables(s)
    lb_w = jax.nn.softmax(hgrn_lb_logits.astype(F32), axis=0)
    lower_bounds = jnp.cumsum(lb_w, axis=0) - lb_w[0]
    head_id = jnp.arange(c) // HEAD_DIM
    e = (head_id[:, None] == head_id[None, :]).astype(BF16)
    row = lambda t: t.reshape(1, -1)
    zpad = lambda t, top, bottom: jnp.pad(t, ((top, bottom), (0, 0)))

    x2 = x.reshape(m, d)
    v_first = None
    for layer in range(depth):
        w_l = w_in[layer].astype(BF16)
        g1 = norm1_g[layer]
        x3 = x2.reshape(b, s, d)
        proj = lambda lo, width: _norm_proj(x2, g1, w_l[:, lo:lo + width])

        k_sb, qt_sb, vt_sb = _attn_proj(
            x3, g1, w_l[:, SB_OFF + c:SB_OFF + 2 * c],
            (w_l[:, SB_OFF:SB_OFF + c] * (HEAD_DIM ** -0.5)).T,
            w_l[:, SB_OFF + 2 * c:SB_OFF + 3 * c].T)
        y_sb = _sb_attention(qt_sb, k_sb, vt_sb).reshape(m, c)

        prm = {
            "mu": row(rwkv_mu[layer]), "w0": row(rwkv_w0[layer]),
            "w2": zpad(rwkv_w2[layer], 0, LANES - RWKV_DECAY_LORA),
            "a0": row(rwkv_a0[layer]),
            "a2": zpad(rwkv_a2[layer], RWKV_DECAY_LORA, LANES - RWKV_DECAY_LORA - RWKV_AAA_LORA),
            "g2": rwkv_g2[layer], "k_k": row(rwkv_k_k[layer]), "k_a": row(rwkv_k_a[layer]),
            "r_k": row(rwkv_r_k[layer]),
        }
        if layer > 0:
            prm["v0"] = row(rwkv_v0[layer - 1])
            prm["v1"] = jnp.pad(rwkv_v1[layer - 1], ((0, 0), (0, LANES - RWKV_MV_LORA)))
            prm["v2"] = zpad(rwkv_v2[layer - 1], 0, LANES - RWKV_MV_LORA)
        rp = proj(RWKV_OFF, RWKV_COLS).reshape(b, s, RWKV_COLS)
        r_, w_, k_, v_, an_, bb_, g_, bonus_ = _rwkv_prep(rp, prm, e, v_first)
        if layer == 0:
            v_first = v_
        y_rec = _rwkv_recurrence(r_, w_, k_, v_, an_, bb_)
        y_rwkv = _rwkv_post(y_rec.reshape(m, c), g_.reshape(m, c), bonus_.reshape(m, c),
                            row(rwkv_gn_w[layer]), row(rwkv_gn_b[layer]), e)

        k_m, qt_m, vt_m, k_mean = _attn_proj(
            x3, g1, w_l[:, MOBA_OFF + c:MOBA_OFF + 2 * c], w_l[:, MOBA_OFF:MOBA_OFF + c].T,
            w_l[:, MOBA_OFF + 2 * c:MOBA_OFF + 3 * c].T, (cos, sin))
        n_blk = k_mean.shape[1]
        k_mean = jnp.pad(k_mean.reshape(b, n_blk, c), ((0, 0), (0, -n_blk % 8), (0, 0)))
        y_moba = _moba_attention(qt_m, k_m, vt_m, k_mean).reshape(m, c)

        hc = proj(HGRN_OFF, HGRN_COLS).reshape(b, s, HGRN_COLS)
        y_hgrn = _hgrn(hc, lower_bounds[layer], hgrn_gn_g[layer], e).reshape(m, c)

        x2 = _merge(x2, g1, w_l[:, GATE_OFF:], (y_sb, y_rwkv, y_moba, y_hgrn),
                    w_branch[layer].astype(BF16), w_out[layer].astype(BF16))
        x2 = _ffn(x2, norm2_g[layer], w_ffn_gate[layer].astype(BF16), w_ffn_up[layer].astype(BF16),
                  w_ffn_down[layer].astype(BF16), final_g if layer == depth - 1 else None)
    return x2.reshape(b, s, d)
```

```python
import functools

import jax
import jax.numpy as jnp
from jax import lax
from jax.experimental import pallas as pl
from jax.experimental.pallas import tpu as pltpu

F32 = jnp.float32
BF16 = jnp.bfloat16

D_MODEL = 1024
HEAD_DIM = 64
N_HEADS = 4
MIX_WIDTH = N_HEADS * HEAD_DIM
N_BRANCH = 4
NORM_EPS = 1e-6
MASK_VALUE = -1e30

RWKV_DECAY_LORA = 64
RWKV_AAA_LORA = 64
RWKV_MV_LORA = 32
RWKV_GATE_LORA = 128
RWKV_GN_EPS = 64e-5
RWKV_COLS = 3 * MIX_WIDTH + RWKV_DECAY_LORA + RWKV_AAA_LORA + RWKV_GATE_LORA

MOBA_BLOCK = 256
MOBA_TOPK = 3
ROPE_THETA = 500000.0
ROPE_DIM = HEAD_DIM // 4

HGRN_CHUNK = 16

SB_COLS = 3 * MIX_WIDTH
MOBA_COLS = 3 * MIX_WIDTH
HGRN_COLS = 4 * MIX_WIDTH
SB_OFF = 0
RWKV_OFF = SB_OFF + SB_COLS
MOBA_OFF = RWKV_OFF + RWKV_COLS
HGRN_OFF = MOBA_OFF + MOBA_COLS
GATE_OFF = HGRN_OFF + HGRN_COLS

LANES = 128
VMEM_LIMIT = 56 * 1024 * 1024

ROW_TILE = 512
SEQ_TILE = 256
REC_TILE = 128


def _params(*sem):
    return pltpu.CompilerParams(dimension_semantics=sem, vmem_limit_bytes=VMEM_LIMIT)


def _dot(a, b):
    return jnp.dot(a, b, preferred_element_type=F32)


def _split2(a):
    hi = a.astype(BF16)
    lo = (a - hi.astype(F32)).astype(BF16)
    return hi, lo


def _split3(a):
    hi = a.astype(BF16)
    r1 = a - hi.astype(F32)
    mid = r1.astype(BF16)
    lo = (r1 - mid.astype(F32)).astype(BF16)
    return hi, mid, lo


def _dot_f32_lhs(a, m):
    hi, lo = _split2(a)
    return _dot(hi, m) + _dot(lo, m)


def _dot_f32_rhs(m, a):
    hi, mid, lo = _split3(a)
    return _dot(m, hi) + _dot(m, mid) + _dot(m, lo)


def _dot3(a, b):
    ah, al = _split2(a)
    bh, bl = _split2(b)
    return _dot(ah, bh) + _dot(ah, bl) + _dot(al, bh)


def _rms(x, g):
    return x * lax.rsqrt(jnp.mean(x * x, axis=-1, keepdims=True) + NORM_EPS) * g


def _norm_proj_kernel(x_ref, g_ref, w_ref, o_ref):
    h = _rms(x_ref[...], g_ref[...])
    o_ref[...] = _dot(h.astype(BF16), w_ref[...])


def _norm_proj(x2, g, w):
    m, d = x2.shape
    n = w.shape[1]
    return pl.pallas_call(
        _norm_proj_kernel,
        grid=(m // ROW_TILE,),
        in_specs=[
            pl.BlockSpec((ROW_TILE, d), lambda i: (i, 0)),
            pl.BlockSpec((1, d), lambda i: (0, 0)),
            pl.BlockSpec((d, n), lambda i: (0, 0)),
        ],
        out_specs=pl.BlockSpec((ROW_TILE, n), lambda i: (i, 0)),
        out_shape=jax.ShapeDtypeStruct((m, n), F32),
        compiler_params=_params("parallel"),
        name="norm_proj",
    )(x2, g.reshape(1, d), w)


_NT = (((1,), (1,)), ((), ()))


def _attn_proj_kernel(*refs, rope):
    if rope:
        (x_ref, g_ref, wk_ref, wqt_ref, wvt_ref, cos_ref, sin_ref, cost_ref, sint_ref,
         k_o, qt_o, vt_o, km_o) = refs
    else:
        x_ref, g_ref, wk_ref, wqt_ref, wvt_ref, k_o, qt_o, vt_o = refs
    h = _rms(x_ref[0], g_ref[...]).astype(BF16)
    k = _dot(h, wk_ref[...])
    qt = lax.dot_general(wqt_ref[...], h, _NT, preferred_element_type=F32)
    vt_o[0] = lax.dot_general(wvt_ref[...], h, _NT, preferred_element_type=F32).astype(BF16)
    if rope:
        c = k.shape[1]
        half = ROPE_DIM // 2
        lane = lax.broadcasted_iota(jnp.int32, k.shape, 1) % HEAD_DIM
        partner = jnp.where(lane < half, pltpu.roll(k, c - half, axis=1), pltpu.roll(k, half, axis=1))
        k = k * cos_ref[...] + partner * sin_ref[...]
        chan = lax.broadcasted_iota(jnp.int32, qt.shape, 0) % HEAD_DIM
        partner = jnp.where(chan < half, pltpu.roll(qt, c - half, axis=0), pltpu.roll(qt, half, axis=0))
        qt_o[0] = qt * cost_ref[...] + partner * sint_ref[...]
        km_o[0, 0] = jnp.mean(k, axis=0, keepdims=True)
    else:
        qt_o[0] = qt.astype(BF16)
    k_o[0] = k.astype(BF16)


def _attn_proj(x3, g, wk, wqt, wvt, rope_tables=None):
    b, s, d = x3.shape
    c = MIX_WIDTH
    t = MOBA_BLOCK
    rope = rope_tables is not None
    full = lambda m, n: pl.BlockSpec((m, n), lambda bi, j: (0, 0))
    rows = pl.BlockSpec((1, t, c), lambda bi, j: (bi, j, 0))
    cols = pl.BlockSpec((1, c, t), lambda bi, j: (bi, 0, j))
    in_specs = [pl.BlockSpec((1, t, d), lambda bi, j: (bi, j, 0)), full(1, d),
                full(d, c), full(c, d), full(c, d)]
    args = [x3, g.reshape(1, d), wk, wqt, wvt]
    out_specs = [rows, cols, cols]
    out_shape = [jax.ShapeDtypeStruct((b, s, c), BF16),
                 jax.ShapeDtypeStruct((b, c, s), F32 if rope else BF16),
                 jax.ShapeDtypeStruct((b, c, s), BF16)]
    if rope:
        cos, sin = rope_tables
        in_specs += [pl.BlockSpec((t, c), lambda bi, j: (j, 0))] * 2
        in_specs += [pl.BlockSpec((c, t), lambda bi, j: (0, j))] * 2
        args += [cos, sin, cos.T, sin.T]
        out_specs.append(pl.BlockSpec((1, 1, 1, c), lambda bi, j: (bi, j, 0, 0)))
        out_shape.append(jax.ShapeDtypeStruct((b, s // t, 1, c), F32))
    return pl.pallas_call(
        functools.partial(_attn_proj_kernel, rope=rope),
        grid=(b, s // t),
        in_specs=in_specs,
        out_specs=out_specs,
        out_shape=out_shape,
        compiler_params=_params("parallel", "parallel"),
        name="rope_proj" if rope else "attn_proj",
    )(*args)


SB_DEAD_LOG = -110.0


def _sb_kernel(qt_ref, k_ref, vt_ref, ut_ref, o_ref, acc_s, *, tq, n_heads):
    i = pl.program_id(1)
    ut = ut_ref[...]
    key = lax.broadcasted_iota(jnp.int32, (tq, tq), 0)
    qry = lax.broadcasted_iota(jnp.int32, (tq, tq), 1)
    past = key < qry
    qt = qt_ref[0]
    chan_head = lax.broadcasted_iota(jnp.int32, qt.shape, 0) // HEAD_DIM
    qh = [jnp.where(chan_head == h, qt, jnp.zeros_like(qt)) for h in range(n_heads)]
    hs = [slice(h * HEAD_DIM, (h + 1) * HEAD_DIM) for h in range(n_heads)]

    def block(j, runs, diagonal):
        start = pl.multiple_of(j * tq, tq)
        heads = range(n_heads)
        kb = k_ref[0, pl.ds(start, tq), :]
        zs = [_dot(kb, qh[h]) for h in heads]
        lks, lbs, pieces = [], [], []
        for h in heads:
            z = zs[h]
            sp = jnp.maximum(z, 0.0) + jnp.log(1.0 + jnp.exp(-jnp.abs(z)))
            lk = -sp
            if diagonal:
                lk = jnp.where(past, lk, 0.0)
            lks.append(lk)
            lbs.append(z - sp)
            pieces.append(_split2(lk))
        betweens = [_dot(ut, hi) + _dot(ut, lo) for hi, lo in pieces]
        atts = []
        for h in heads:
            att = jnp.exp(lbs[h] + betweens[h] + runs[h])
            if diagonal:
                att = jnp.where(past, att, 0.0)
            atts.append(att.astype(BF16))
        for h in heads:
            part = _dot(vt_ref[0, hs[h], pl.ds(start, tq)], atts[h])
            acc_s[hs[h], :] = part if diagonal else acc_s[hs[h], :] + part
        return tuple(runs[h] + jnp.sum(lks[h], axis=0, keepdims=True) for h in heads)

    def live(runs):
        top = functools.reduce(jnp.maximum, [jnp.max(r) for r in runs])
        return (top > SB_DEAD_LOG).astype(jnp.int32)

    runs = block(i, tuple(jnp.zeros((1, tq), F32) for _ in range(n_heads)), True)

    def cond(c):
        jj, go, _ = c
        return jnp.logical_and(jj < i, go > 0)

    def body(c):
        jj, _, runs = c
        runs = block(i - 1 - jj, runs, False)
        return jj + 1, live(runs), runs

    lax.while_loop(cond, body, (jnp.int32(0), live(runs), runs))
    o_ref[0] = acc_s[...].T


def _sb_attention(qt, k, vt):
    b, s, c = k.shape
    tq = SEQ_TILE
    ut = (jnp.arange(tq)[None, :] > jnp.arange(tq)[:, None]).astype(BF16)
    return pl.pallas_call(
        functools.partial(_sb_kernel, tq=tq, n_heads=N_HEADS),
        grid=(b, s // tq),
        in_specs=[
            pl.BlockSpec((1, c, tq), lambda bi, i: (bi, 0, i)),
            pl.BlockSpec((1, s, c), lambda bi, i: (bi, 0, 0)),
            pl.BlockSpec((1, c, s), lambda bi, i: (bi, 0, 0)),
            pl.BlockSpec((tq, tq), lambda bi, i: (0, 0)),
        ],
        out_specs=pl.BlockSpec((1, tq, c), lambda bi, i: (bi, i, 0)),
        out_shape=jax.ShapeDtypeStruct((b, s, c), F32),
        scratch_shapes=[pltpu.VMEM((c, tq), F32)],
        compiler_params=_params("parallel", "parallel"),
        name="sb_attention",
    )(qt, k, vt, ut)


def _rwkv_prep_kernel(*refs, has_vres):
    if has_vres:
        (p_ref, mu_ref, w0_ref, w2_ref, a0_ref, a2_ref, g2_ref, kk_ref, ka_ref, rk_ref, e_ref,
         vf_ref, v0_ref, v1_ref, v2_ref,
         r_o, w_o, k_o, v_o, an_o, bb_o, g_o, bonus_o, carry) = refs
    else:
        (p_ref, mu_ref, w0_ref, w2_ref, a0_ref, a2_ref, g2_ref, kk_ref, ka_ref, rk_ref, e_ref,
         r_o, w_o, k_o, v_o, an_o, bb_o, g_o, bonus_o, carry) = refs
    c = MIX_WIDTH
    p = p_ref[0]
    t = p.shape[0]

    @pl.when(pl.program_id(1) == 0)
    def _():
        carry[...] = jnp.zeros_like(carry)

    rowid = lax.broadcasted_iota(jnp.int32, p.shape, 0)
    prev = jnp.where(rowid == 0, carry[...], pltpu.roll(p, 1, axis=0))
    carry[...] = p[t - 1:t, :]
    xs = p + (prev - p) * mu_ref[...]

    r = xs[:, :c]
    k = xs[:, c:2 * c]
    v = xs[:, 2 * c:3 * c]
    lora = xs[:, 3 * c:3 * c + LANES]
    gl = xs[:, 3 * c + LANES:]
    w_log = -jax.nn.softplus(-(w0_ref[...] + _dot3(jnp.tanh(lora), w2_ref[...]))) - 0.5
    decay = jnp.exp(-jnp.exp(w_log))
    if has_vres:
        mix = jax.nn.sigmoid(v0_ref[...] + _dot3(_dot3(v, v1_ref[...]), v2_ref[...]))
        v = v + (vf_ref[0] - v) * mix
    a = jax.nn.sigmoid(a0_ref[...] + _dot3(lora, a2_ref[...]))
    g = _dot3(jax.nn.sigmoid(gl), g2_ref[...])
    e = e_ref[...]
    kk = k * kk_ref[...]
    kk = kk / jnp.maximum(jnp.sqrt(_dot_f32_lhs(kk * kk, e)), 1e-12)
    k = k * (1.0 + (a - 1.0) * ka_ref[...])
    bonus = _dot_f32_lhs(r * k * rk_ref[...], e) * v
    r_o[0] = r
    w_o[0] = decay
    k_o[0] = k
    v_o[0] = v
    an_o[0] = -kk
    bb_o[0] = kk * a
    g_o[0] = g
    bonus_o[0] = bonus


def _rwkv_prep(p, prm, e, v_first):
    b, s, cols = p.shape
    c = MIX_WIDTH
    t = SEQ_TILE
    has_vres = v_first is not None
    row = lambda n: pl.BlockSpec((1, n), lambda bi, j: (0, 0))
    mat = lambda m, n: pl.BlockSpec((m, n), lambda bi, j: (0, 0))
    seq = lambda n: pl.BlockSpec((1, t, n), lambda bi, j: (bi, j, 0))
    in_specs = [seq(cols), row(cols), row(c), mat(LANES, c), row(c), mat(LANES, c),
                mat(LANES, c), row(c), row(c), row(c), mat(c, c)]
    args = [p, prm["mu"], prm["w0"], prm["w2"], prm["a0"], prm["a2"], prm["g2"],
            prm["k_k"], prm["k_a"], prm["r_k"], e]
    if has_vres:
        in_specs += [seq(c), row(c), mat(c, LANES), mat(LANES, c)]
        args += [v_first, prm["v0"], prm["v1"], prm["v2"]]
    out = jax.ShapeDtypeStruct((b, s, c), F32)
    return pl.pallas_call(
        functools.partial(_rwkv_prep_kernel, has_vres=has_vres),
        grid=(b, s // t),
        in_specs=in_specs,
        out_specs=[seq(c)] * 8,
        out_shape=[out] * 8,
        scratch_shapes=[pltpu.VMEM((1, cols), F32)],
        compiler_params=_params("parallel", "arbitrary"),
        name="rwkv_prep",
    )(*args)


def _rwkv_rec_kernel(an_ref, w_ref, bb_ref, k_ref, r_ref, v_ref, y_ref, st_ref, *, n_tiles, group):
    @pl.when(pl.program_id(0) == 0)
    def _():
        st_ref[...] = jnp.zeros_like(st_ref)

    nb = an_ref.shape[0]
    per_head = LANES // N_HEADS
    lane = lax.broadcasted_iota(jnp.int32, (HEAD_DIM, LANES), 1)
    lane_b = lane // (N_HEADS * group)
    lane_h = (lane // group) % N_HEADS

    def packed(ref, t0):
        w = jnp.concatenate([ref[:, t0 + u, :] for u in range(group)], axis=0)
        pieces = []
        for h in range(N_HEADS):
            col, off = divmod(h * HEAD_DIM, LANES)
            piece = w[:, col * LANES:(col + 1) * LANES]
            pieces.append(pltpu.roll(piece, LANES - off, axis=1) if off else piece)
        return jnp.concatenate(pieces, axis=0).T[:HEAD_DIM, :]

    operands = (an_ref, w_ref, bb_ref, k_ref, r_ref)
    n_groups = an_ref.shape[1] // group

    def body(tp, packs):
        t0 = tp * group
        t_next = jnp.minimum(tp + 1, n_groups - 1) * group
        next_packs = tuple(packed(ref, t_next) for ref in operands)
        for u in range(group):
            idx = lane_h * per_head + u * nb + lane_b
            a, w, bb, k, r = (jnp.take_along_axis(p, idx, axis=1) for p in packs)
            t = t0 + u
            vt = v_ref[t]
            rows = []
            for j in range(n_tiles):
                st = st_ref[j]
                sa = jnp.sum(st * a, axis=0, keepdims=True)
                st = st * w + sa * bb + vt[j:j + 1, :] * k
                st_ref[j] = st
                rows.append(jnp.sum(st * r, axis=0, keepdims=True))
            y_ref[t] = jnp.concatenate(rows, axis=0)
        return next_packs

    lax.fori_loop(0, n_groups, body, tuple(packed(ref, 0) for ref in operands))


def _rwkv_recurrence(r, w, k, v, an, bb):
    b, s, c = r.shape
    group = LANES // (b * N_HEADS)
    n_tiles = HEAD_DIM // group
    ve = v.reshape(b, s, N_HEADS, n_tiles, group).transpose(1, 3, 0, 2, 4).reshape(s, n_tiles, LANES)
    big = pl.BlockSpec((b, REC_TILE, c), lambda i: (0, i, 0))
    small = pl.BlockSpec((REC_TILE, n_tiles, LANES), lambda i: (i, 0, 0))
    y = pl.pallas_call(
        functools.partial(_rwkv_rec_kernel, n_tiles=n_tiles, group=group),
        grid=(s // REC_TILE,),
        in_specs=[big] * 5 + [small],
        out_specs=small,
        out_shape=jax.ShapeDtypeStruct((s, n_tiles, LANES), F32),
        scratch_shapes=[pltpu.VMEM((n_tiles, HEAD_DIM, LANES), F32)],
        compiler_params=_params("arbitrary"),
        name="rwkv_recurrence",
    )(an, w, bb, k, r, ve)
    y = y.reshape(s, n_tiles, b, N_HEADS, group).transpose(2, 0, 3, 1, 4)
    return y.reshape(b, s, c)


def _rwkv_post_kernel(y_ref, g_ref, bonus_ref, gw_ref, gb_ref, e_ref, o_ref):
    y = y_ref[...]
    e = e_ref[...]
    mean = _dot_f32_lhs(y, e) * (1.0 / HEAD_DIM)
    yc = y - mean
    var = _dot_f32_lhs(yc * yc, e) * (1.0 / HEAD_DIM)
    yn = yc * lax.rsqrt(var + RWKV_GN_EPS) * gw_ref[...] + gb_ref[...]
    o_ref[...] = (yn + bonus_ref[...]) * g_ref[...]


def _rwkv_post(y, g, bonus, gn_w, gn_b, e):
    m, c = y.shape
    tile = pl.BlockSpec((ROW_TILE, c), lambda i: (i, 0))
    row = pl.BlockSpec((1, c), lambda i: (0, 0))
    return pl.pallas_call(
        _rwkv_post_kernel,
        grid=(m // ROW_TILE,),
        in_specs=[tile, tile, tile, row, row, pl.BlockSpec((c, c), lambda i: (0, 0))],
        out_specs=tile,
        out_shape=jax.ShapeDtypeStruct((m, c), F32),
        compiler_params=_params("parallel"),
        name="rwkv_post",
    )(y, g, bonus, gn_w, gn_b, e)


def _moba_kernel(qt_ref, k_ref, vt_ref, km_ref, o_ref, acc_s, *, blk, n_sel, n_heads):
    i = pl.program_id(1)
    n_pad = km_ref.shape[1]
    blk_id = lax.broadcasted_iota(jnp.int32, (n_pad, blk), 0)
    fully_past = blk_id < i
    weight = jnp.where(blk_id < 24, jnp.left_shift(1, jnp.minimum(blk_id, 23)), 0).astype(F32)
    key = lax.broadcasted_iota(jnp.int32, (blk, blk), 0)
    qry = lax.broadcasted_iota(jnp.int32, (blk, blk), 1)
    causal = key <= qry
    qt_all = qt_ref[0]
    km_all = km_ref[0]
    chan_head = lax.broadcasted_iota(jnp.int32, qt_all.shape, 0) // HEAD_DIM
    lane_head = lax.broadcasted_iota(jnp.int32, km_all.shape, 1) // HEAD_DIM
    q_scaled = (qt_all * (HEAD_DIM ** -0.5)).astype(BF16)
    hs = [slice(h * HEAD_DIM, (h + 1) * HEAD_DIM) for h in range(n_heads)]

    qs, bits = [], []
    for h in range(n_heads):
        gate = jnp.dot(jnp.where(lane_head == h, km_all, 0.0), qt_all, preferred_element_type=F32,
                       precision=lax.Precision.HIGHEST)
        gate = jnp.where(fully_past, gate, MASK_VALUE)
        chosen = jnp.zeros(gate.shape, F32)
        for _ in range(n_sel):
            top = jnp.max(gate, axis=0, keepdims=True)
            idx = jnp.min(jnp.where(gate == top, blk_id, n_pad), axis=0, keepdims=True)
            pick = blk_id == idx
            chosen = jnp.where(pick & fully_past, 1.0, chosen)
            gate = jnp.where(pick, -jnp.inf, gate)
        bits.append(jnp.sum(chosen * weight, axis=0, keepdims=True).astype(jnp.int32))
        qs.append(jnp.where(chan_head == h, q_scaled, jnp.zeros_like(q_scaled)))

    def scores(n, h):
        start = pl.multiple_of(n * blk, blk)
        return _dot(k_ref[0, pl.ds(start, blk), :], qs[h]), vt_ref[0, hs[h], pl.ds(start, blk)]

    heads = range(n_heads)
    ms, ls, ps = [], [], []
    own = [scores(i, h) for h in heads]
    for h in heads:
        s = jnp.where(causal, own[h][0], MASK_VALUE)
        m = jnp.max(s, axis=0, keepdims=True)
        p = jnp.exp(s - m)
        ms.append(m)
        ls.append(jnp.sum(p, axis=0, keepdims=True))
        ps.append(p.astype(BF16))
    for h in heads:
        acc_s[hs[h], :] = _dot(own[h][1], ps[h])

    def body(n, carry):
        ms, ls = carry
        blocks = [scores(n, h) for h in heads]
        new_ms, new_ls, alphas, ps = [], [], [], []
        for h in heads:
            sel = (jnp.right_shift(bits[h], n) & 1) == 1
            s = jnp.where(sel, blocks[h][0], MASK_VALUE)
            m_new = jnp.maximum(ms[h], jnp.max(s, axis=0, keepdims=True))
            alpha = jnp.exp(ms[h] - m_new)
            p = jnp.exp(s - m_new)
            new_ms.append(m_new)
            new_ls.append(alpha * ls[h] + jnp.sum(p, axis=0, keepdims=True))
            alphas.append(alpha)
            ps.append(p.astype(BF16))
        for h in heads:
            acc_s[hs[h], :] = alphas[h] * acc_s[hs[h], :] + _dot(blocks[h][1], ps[h])
        return tuple(new_ms), tuple(new_ls)

    ms, ls = lax.fori_loop(0, i, body, (tuple(ms), tuple(ls)))
    for h in range(n_heads):
        acc_s[hs[h], :] = acc_s[hs[h], :] / ls[h]
    o_ref[0] = acc_s[...].T


def _moba_attention(qt, k, vt, km):
    b, s, c = k.shape
    blk = MOBA_BLOCK
    n_blk = s // blk
    n_pad = km.shape[1]
    assert s % blk == 0 and n_blk <= 24
    n_sel = min(MOBA_TOPK, n_blk)
    return pl.pallas_call(
        functools.partial(_moba_kernel, blk=blk, n_sel=n_sel, n_heads=N_HEADS),
        grid=(b, n_blk),
        in_specs=[
            pl.BlockSpec((1, c, blk), lambda bi, i: (bi, 0, i)),
            pl.BlockSpec((1, s, c), lambda bi, i: (bi, 0, 0)),
            pl.BlockSpec((1, c, s), lambda bi, i: (bi, 0, 0)),
            pl.BlockSpec((1, n_pad, c), lambda bi, i: (bi, 0, 0)),
        ],
        out_specs=pl.BlockSpec((1, blk, c), lambda bi, i: (bi, i, 0)),
        out_shape=jax.ShapeDtypeStruct((b, s, c), F32),
        scratch_shapes=[pltpu.VMEM((c, blk), F32)],
        compiler_params=_params("parallel", "parallel"),
        name="moba_attention",
    )(qt, k, vt, km)


def _hgrn_kernel(q_ref, f_ref, i_ref, g_ref, lb_ref, gn_ref, e_ref, tri_ref, ones_ref, hm_ref,
                 o_ref, st_ref, q_s, k_s, b_s, v_s, qd_s, ke_s, be_s, o_s, *, chunk):
    c = MIX_WIDTH
    rows = q_ref.shape[1]

    @pl.when(pl.program_id(1) == 0)
    def _():
        st_ref[...] = jnp.zeros_like(st_ref)

    lb = lb_ref[...]
    sig = jax.nn.sigmoid(f_ref[0])
    log_f = jnp.log(lb + (1.0 - lb) * sig)
    key = (1.0 - lb) * (1.0 - sig)
    b = _dot_f32_rhs(tri_ref[...], log_f)
    b_end = _dot_f32_rhs(ones_ref[...], log_f)
    q = q_ref[0]
    q_s[...] = q
    k_s[...] = key
    b_s[...] = b
    v_s[...] = i_ref[0]
    qd_s[...] = q * jnp.exp(b)
    ke_s[...] = key * jnp.exp(b_end - b)
    be_s[...] = b_end
    e = e_ref[...]
    head_mask = hm_ref[...]
    sub = lax.broadcasted_iota(jnp.int32, (chunk, c), 0)

    def body(n, carry):
        r0 = pl.multiple_of(n * chunk, chunk)
        sl = pl.ds(r0, chunk)
        qi = q_s[sl, :]
        ki = k_s[sl, :]
        bi = b_s[sl, :]
        vi = v_s[sl, :]
        parts = []
        for s in range(chunk):
            pr = qi * ki[s:s + 1, :] * jnp.exp(jnp.minimum(bi - bi[s:s + 1, :], 0.0))
            parts.append(jnp.where(sub >= s, pr, 0.0))
        sc = _dot_f32_lhs(jnp.concatenate(parts, axis=0), e)
        o = jnp.zeros((chunk, c), F32)
        for s in range(chunk):
            o = o + sc[s * chunk:(s + 1) * chunk, :] * vi[s:s + 1, :]
        st = st_ref[...]
        o = o + lax.dot_general(qd_s[sl, :].astype(BF16), st.astype(BF16),
                                (((1,), (1,)), ((), ())), preferred_element_type=F32)
        o_s[sl, :] = o
        kv = lax.dot_general(vi.astype(BF16), ke_s[sl, :].astype(BF16),
                             (((0,), (0,)), ((), ())), preferred_element_type=F32)
        st_ref[...] = st * jnp.exp(be_s[pl.ds(r0, 1), :]) + kv * head_mask
        return carry

    lax.fori_loop(0, rows // chunk, body, 0, unroll=4)
    o = o_s[...]
    ms = _dot_f32_lhs(o * o, e) * (1.0 / HEAD_DIM)
    g = g_ref[0]
    o_ref[0] = o * lax.rsqrt(ms + NORM_EPS) * gn_ref[...] * (g * jax.nn.sigmoid(g))


def _hgrn(hc, lb, gn_g, e):
    b, s, _ = hc.shape
    c = MIX_WIDTH
    t = SEQ_TILE
    ch = HGRN_CHUNK
    ridx = jnp.arange(t)
    same = (ridx[:, None] // ch) == (ridx[None, :] // ch)
    tri = (same & (ridx[None, :] <= ridx[:, None])).astype(BF16)
    ones = same.astype(BF16)
    seq = lambda n: pl.BlockSpec((1, t, c), lambda bi, j: (bi, j, n))
    row = pl.BlockSpec((1, c), lambda bi, j: (0, 0))
    mat = lambda n: pl.BlockSpec((n, n), lambda bi, j: (0, 0))
    return pl.pallas_call(
        functools.partial(_hgrn_kernel, chunk=ch),
        grid=(b, s // t),
        in_specs=[seq(0), seq(1), seq(2), seq(3), row, row, mat(c), mat(t), mat(t), mat(c)],
        out_specs=pl.BlockSpec((1, t, c), lambda bi, j: (bi, j, 0)),
        out_shape=jax.ShapeDtypeStruct((b, s, c), F32),
        scratch_shapes=[pltpu.VMEM((c, c), F32)] + [pltpu.VMEM((t, c), F32)] * 8,
        compiler_params=_params("parallel", "arbitrary"),
        name="hgrn2",
    )(hc, hc, hc, hc, lb.reshape(1, c), gn_g.reshape(1, c), e, tri, ones, e.astype(F32))


def _merge_kernel(x_ref, g_ref, wg_ref, y0_ref, y1_ref, y2_ref, y3_ref, wb_ref, wo_ref, o_ref):
    x = x_ref[...]
    d = x.shape[1]
    h = _rms(x, g_ref[...]).astype(BF16)
    merged = None
    for n, y_ref in enumerate((y0_ref, y1_ref, y2_ref, y3_ref)):
        gate = jax.nn.sigmoid(_dot(h, wg_ref[:, n * d:(n + 1) * d]))
        term = gate * _dot(y_ref[...].astype(BF16), wb_ref[n])
        merged = term if merged is None else merged + term
    o_ref[...] = x + _dot(merged.astype(BF16), wo_ref[...])


def _merge(x2, g, w_gate, ys, w_branch, w_out):
    m, d = x2.shape
    c = MIX_WIDTH
    tile = pl.BlockSpec((ROW_TILE, d), lambda i: (i, 0))
    ytile = pl.BlockSpec((ROW_TILE, c), lambda i: (i, 0))
    return pl.pallas_call(
        _merge_kernel,
        grid=(m // ROW_TILE,),
        in_specs=[tile, pl.BlockSpec((1, d), lambda i: (0, 0)),
                  pl.BlockSpec((d, N_BRANCH * d), lambda i: (0, 0)),
                  ytile, ytile, ytile, ytile,
                  pl.BlockSpec((N_BRANCH, c, d), lambda i: (0, 0, 0)),
                  pl.BlockSpec((d, d), lambda i: (0, 0))],
        out_specs=tile,
        out_shape=jax.ShapeDtypeStruct((m, d), F32),
        compiler_params=_params("parallel"),
        name="gated_merge",
    )(x2, g.reshape(1, d), w_gate, *ys, w_branch, w_out)


def _ffn_kernel(*refs, final):
    if final:
        x_ref, g_ref, wg_ref, wu_ref, wd_ref, fg_ref, o_ref, h_s, acc_s = refs
    else:
        x_ref, g_ref, wg_ref, wu_ref, wd_ref, o_ref, h_s, acc_s = refs
    f = pl.program_id(1)

    @pl.when(f == 0)
    def _():
        h_s[...] = _rms(x_ref[...], g_ref[...]).astype(BF16)
        acc_s[...] = jnp.zeros_like(acc_s)

    h = h_s[...]
    gate = _dot(h, wg_ref[...])
    act = gate * jax.nn.sigmoid(gate) * _dot(h, wu_ref[...])
    acc_s[...] += _dot(act.astype(BF16), wd_ref[...])

    @pl.when(f == pl.num_programs(1) - 1)
    def _():
        y = x_ref[...] + acc_s[...]
        o_ref[...] = _rms(y, fg_ref[...]) if final else y


def _ffn(x2, g, w_gate, w_up, w_down, final_g):
    m, d = x2.shape
    dff = w_gate.shape[1]
    tf = dff // 2
    assert tf % LANES == 0
    final = final_g is not None
    tile = pl.BlockSpec((ROW_TILE, d), lambda i, f: (i, 0))
    row = pl.BlockSpec((1, d), lambda i, f: (0, 0))
    in_specs = [tile, row,
                pl.BlockSpec((d, tf), lambda i, f: (0, f)),
                pl.BlockSpec((d, tf), lambda i, f: (0, f)),
                pl.BlockSpec((tf, d), lambda i, f: (f, 0))]
    args = [x2, g.reshape(1, d), w_gate, w_up, w_down]
    if final:
        in_specs.append(row)
        args.append(final_g.reshape(1, d))
    return pl.pallas_call(
        functools.partial(_ffn_kernel, final=final),
        grid=(m // ROW_TILE, dff // tf),
        in_specs=in_specs,
        out_specs=tile,
        out_shape=jax.ShapeDtypeStruct((m, d), F32),
        scratch_shapes=[pltpu.VMEM((ROW_TILE, d), BF16), pltpu.VMEM((ROW_TILE, d), F32)],
        compiler_params=_params("parallel", "arbitrary"),
        name="ffn",
    )(*args)


def _rope_tables(s):
    half = ROPE_DIM // 2
    inv_freq = ROPE_THETA ** (-jnp.arange(0, ROPE_DIM, 2, dtype=F32) / ROPE_DIM)
    ang = jnp.arange(s, dtype=F32)[:, None] * inv_freq[None, :]
    cos, sin = jnp.cos(ang), jnp.sin(ang)
    pad = HEAD_DIM - ROPE_DIM
    cos_h = jnp.concatenate([cos, cos, jnp.ones((s, pad), F32)], axis=1)
    sin_h = jnp.concatenate([-sin, sin, jnp.zeros((s, pad), F32)], axis=1)
    return jnp.tile(cos_h, (1, N_HEADS)), jnp.tile(sin_h, (1, N_HEADS))


def kernel(x, norm1_g, w_in, rwkv_mu, rwkv_w0, rwkv_w2, rwkv_a0, rwkv_a2, rwkv_g2, rwkv_k_k, rwkv_k_a, rwkv_r_k, rwkv_gn_w, rwkv_gn_b, rwkv_v0, rwkv_v1, rwkv_v2, hgrn_lb_logits, hgrn_gn_g, w_branch, w_out, norm2_g, w_ffn_gate, w_ffn_up, w_ffn_down, final_g):
    b, s, d = x.shape
    depth = w_in.shape[0]
    c = MIX_WIDTH
    m = b * s
    assert d == D_MODEL and s % SEQ_TILE == 0 and m % ROW_TILE == 0 and s % REC_TILE == 0
    assert LANES % (b * N_HEADS) == 0 and HEAD_DIM % (LANES // (b * N_HEADS)) == 0

    cos, sin = _rope_tables(s)
    lb_w = jax.nn.softmax(hgrn_lb_logits.astype(F32), axis=0)
    lower_bounds = jnp.cumsum(lb_w, axis=0) - lb_w[0]
    head_id = jnp.arange(c) // HEAD_DIM
    e = (head_id[:, None] == head_id[None, :]).astype(BF16)
    row = lambda t: t.reshape(1, -1)
    zpad = lambda t, top, bottom: jnp.pad(t, ((top, bottom), (0, 0)))

    x2 = x.reshape(m, d)
    v_first = None
    for layer in range(depth):
        w_l = w_in[layer].astype(BF16)
        g1 = norm1_g[layer]
        x3 = x2.reshape(b, s, d)
        proj = lambda lo, width: _norm_proj(x2, g1, w_l[:, lo:lo + width])

        k_sb, qt_sb, vt_sb = _attn_proj(
            x3, g1, w_l[:, SB_OFF + c:SB_OFF + 2 * c],
            (w_l[:, SB_OFF:SB_OFF + c] * (HEAD_DIM ** -0.5)).T,
            w_l[:, SB_OFF + 2 * c:SB_OFF + 3 * c].T)
        y_sb = _sb_attention(qt_sb, k_sb, vt_sb).reshape(m, c)

        prm = {
            "mu": row(rwkv_mu[layer]), "w0": row(rwkv_w0[layer]),
            "w2": zpad(rwkv_w2[layer], 0, LANES - RWKV_DECAY_LORA),
            "a0": row(rwkv_a0[layer]),
            "a2": zpad(rwkv_a2[layer], RWKV_DECAY_LORA, LANES - RWKV_DECAY_LORA - RWKV_AAA_LORA),
            "g2": rwkv_g2[layer], "k_k": row(rwkv_k_k[layer]), "k_a": row(rwkv_k_a[layer]),
            "r_k": row(rwkv_r_k[layer]),
        }
        if layer > 0:
            prm["v0"] = row(rwkv_v0[layer - 1])
            prm["v1"] = jnp.pad(rwkv_v1[layer - 1], ((0, 0), (0, LANES - RWKV_MV_LORA)))
            prm["v2"] = zpad(rwkv_v2[layer - 1], 0, LANES - RWKV_MV_LORA)
        rp = proj(RWKV_OFF, RWKV_COLS).reshape(b, s, RWKV_COLS)
        r_, w_, k_, v_, an_, bb_, g_, bonus_ = _rwkv_prep(rp, prm, e, v_first)
        if layer == 0:
            v_first = v_
        y_rec = _rwkv_recurrence(r_, w_, k_, v_, an_, bb_)
        y_rwkv = _rwkv_post(y_rec.reshape(m, c), g_.reshape(m, c), bonus_.reshape(m, c),
                            row(rwkv_gn_w[layer]), row(rwkv_gn_b[layer]), e)

        k_m, qt_m, vt_m, k_mean = _attn_proj(
            x3, g1, w_l[:, MOBA_OFF + c:MOBA_OFF + 2 * c], w_l[:, MOBA_OFF:MOBA_OFF + c].T,
            w_l[:, MOBA_OFF + 2 * c:MOBA_OFF + 3 * c].T, (cos, sin))
        n_blk = k_mean.shape[1]
        k_mean = jnp.pad(k_mean.reshape(b, n_blk, c), ((0, 0), (0, -n_blk % 8), (0, 0)))
        y_moba = _moba_attention(qt_m, k_m, vt_m, k_mean).reshape(m, c)

        hc = proj(HGRN_OFF, HGRN_COLS).reshape(b, s, HGRN_COLS)
        y_hgrn = _hgrn(hc, lower_bounds[layer], hgrn_gn_g[layer], e).reshape(m, c)

        x2 = _merge(x2, g1, w_l[:, GATE_OFF:], (y_sb, y_rwkv, y_moba, y_hgrn),
                    w_branch[layer].astype(BF16), w_out[layer].astype(BF16))
        x2 = _ffn(x2, norm2_g[layer], w_ffn_gate[layer].astype(BF16), w_ffn_up[layer].astype(BF16),
                  w_ffn_down[layer].astype(BF16), final_g if layer == depth - 1 else None)
    return x2.reshape(b, s, d)
```

```python
import functools

import jax
import jax.numpy as jnp
from jax import lax
from jax.experimental import pallas as pl
from jax.experimental.pallas import tpu as pltpu

F32 = jnp.float32
BF16 = jnp.bfloat16

D_MODEL = 1024
HEAD_DIM = 64
N_HEADS = 4
MIX_WIDTH = N_HEADS * HEAD_DIM
N_BRANCH = 4
NORM_EPS = 1e-6
MASK_VALUE = -1e30

RWKV_DECAY_LORA = 64
RWKV_AAA_LORA = 64
RWKV_MV_LORA = 32
RWKV_GATE_LORA = 128
RWKV_GN_EPS = 64e-5
RWKV_COLS = 3 * MIX_WIDTH + RWKV_DECAY_LORA + RWKV_AAA_LORA + RWKV_GATE_LORA

MOBA_BLOCK = 256
MOBA_TOPK = 3
ROPE_THETA = 500000.0
ROPE_DIM = HEAD_DIM // 4

HGRN_CHUNK = 16

SB_COLS = 3 * MIX_WIDTH
MOBA_COLS = 3 * MIX_WIDTH
HGRN_COLS = 4 * MIX_WIDTH
SB_OFF = 0
RWKV_OFF = SB_OFF + SB_COLS
MOBA_OFF = RWKV_OFF + RWKV_COLS
HGRN_OFF = MOBA_OFF + MOBA_COLS
GATE_OFF = HGRN_OFF + HGRN_COLS

LANES = 128
VMEM_LIMIT = 56 * 1024 * 1024

ROW_TILE = 512
SEQ_TILE = 256
REC_TILE = 128


def _params(*sem):
    return pltpu.CompilerParams(dimension_semantics=sem, vmem_limit_bytes=VMEM_LIMIT)


def _dot(a, b):
    return jnp.dot(a, b, preferred_element_type=F32)


def _split2(a):
    hi = a.astype(BF16)
    lo = (a - hi.astype(F32)).astype(BF16)
    return hi, lo


def _split3(a):
    hi = a.astype(BF16)
    r1 = a - hi.astype(F32)
    mid = r1.astype(BF16)
    lo = (r1 - mid.astype(F32)).astype(BF16)
    return hi, mid, lo


def _dot_f32_lhs(a, m):
    hi, lo = _split2(a)
    return _dot(hi, m) + _dot(lo, m)


def _dot_f32_rhs(m, a):
    hi, mid, lo = _split3(a)
    return _dot(m, hi) + _dot(m, mid) + _dot(m, lo)


def _dot3(a, b):
    ah, al = _split2(a)
    bh, bl = _split2(b)
    return _dot(ah, bh) + _dot(ah, bl) + _dot(al, bh)


def _rms(x, g):
    return x * lax.rsqrt(jnp.mean(x * x, axis=-1, keepdims=True) + NORM_EPS) * g


_NT = (((1,), (1,)), ((), ()))


def _in_proj_kernel(x_ref, g_ref, wr_ref, wh_ref, sk_ref, sqt_ref, svt_ref, mk_ref, mqt_ref, mvt_ref,
                    cos_ref, sin_ref, cost_ref, sint_ref,
                    rw_o, hg_o, sk_o, sqt_o, svt_o, mk_o, mqt_o, mvt_o, km_o):
    h = _rms(x_ref[0], g_ref[...]).astype(BF16)
    nt = lambda w_ref: lax.dot_general(w_ref[...], h, _NT, preferred_element_type=F32)
    rw_o[0] = _dot(h, wr_ref[...])
    hg_o[0] = _dot(h, wh_ref[...])
    sk_o[0] = _dot(h, sk_ref[...]).astype(BF16)
    sqt_o[0] = nt(sqt_ref).astype(BF16)
    svt_o[0] = nt(svt_ref).astype(BF16)
    mvt_o[0] = nt(mvt_ref).astype(BF16)
    k = _dot(h, mk_ref[...])
    qt = nt(mqt_ref)
    c = k.shape[1]
    half = ROPE_DIM // 2
    lane = lax.broadcasted_iota(jnp.int32, k.shape, 1) % HEAD_DIM
    partner = jnp.where(lane < half, pltpu.roll(k, c - half, axis=1), pltpu.roll(k, half, axis=1))
    k = k * cos_ref[...] + partner * sin_ref[...]
    chan = lax.broadcasted_iota(jnp.int32, qt.shape, 0) % HEAD_DIM
    partner = jnp.where(chan < half, pltpu.roll(qt, c - half, axis=0), pltpu.roll(qt, half, axis=0))
    mqt_o[0] = qt * cost_ref[...] + partner * sint_ref[...]
    mk_o[0] = k.astype(BF16)
    for i in range(k.shape[0] // MOBA_BLOCK):
        km_o[0, i] = jnp.mean(k[i * MOBA_BLOCK:(i + 1) * MOBA_BLOCK], axis=0, keepdims=True)


def _in_proj(x3, g, w_l, cos, sin):
    b, s, d = x3.shape
    c = MIX_WIDTH
    t = ROW_TILE
    assert t % MOBA_BLOCK == 0
    resident = lambda m, n: pl.BlockSpec((m, n), lambda bi, j: (0, 0), pipeline_mode=pl.Buffered(1))
    rows = lambda n: pl.BlockSpec((1, t, n), lambda bi, j: (bi, j, 0))
    cols = pl.BlockSpec((1, c, t), lambda bi, j: (bi, 0, j))
    sb, mo = w_l[:, SB_OFF:SB_OFF + SB_COLS], w_l[:, MOBA_OFF:MOBA_OFF + MOBA_COLS]
    args = [x3, g.reshape(1, d), w_l[:, RWKV_OFF:RWKV_OFF + RWKV_COLS], w_l[:, HGRN_OFF:HGRN_OFF + HGRN_COLS],
            sb[:, c:2 * c], (sb[:, :c] * (HEAD_DIM ** -0.5)).T, sb[:, 2 * c:].T,
            mo[:, c:2 * c], mo[:, :c].T, mo[:, 2 * c:].T, cos, sin, cos.T, sin.T]
    in_specs = [rows(d), resident(1, d), resident(d, RWKV_COLS), resident(d, HGRN_COLS),
                resident(d, c), resident(c, d), resident(c, d),
                resident(d, c), resident(c, d), resident(c, d),
                pl.BlockSpec((t, c), lambda bi, j: (j, 0)), pl.BlockSpec((t, c), lambda bi, j: (j, 0)),
                pl.BlockSpec((c, t), lambda bi, j: (0, j)), pl.BlockSpec((c, t), lambda bi, j: (0, j))]
    n_mean = t // MOBA_BLOCK
    out_specs = [rows(RWKV_COLS), rows(HGRN_COLS), rows(c), cols, cols, rows(c), cols, cols,
                 pl.BlockSpec((1, n_mean, 1, c), lambda bi, j: (bi, j, 0, 0))]
    sds = jax.ShapeDtypeStruct
    out_shape = [sds((b, s, RWKV_COLS), F32), sds((b, s, HGRN_COLS), F32),
                 sds((b, s, c), BF16), sds((b, c, s), BF16), sds((b, c, s), BF16),
                 sds((b, s, c), BF16), sds((b, c, s), F32), sds((b, c, s), BF16),
                 sds((b, s // MOBA_BLOCK, 1, c), F32)]
    return pl.pallas_call(
        _in_proj_kernel,
        grid=(b, s // t),
        in_specs=in_specs,
        out_specs=out_specs,
        out_shape=out_shape,
        compiler_params=_params("parallel", "parallel"),
        name="in_proj",
    )(*args)


SB_DEAD_LOG = -110.0


def _sb_kernel(qt_ref, k_ref, vt_ref, ut_ref, o_ref, acc_s, *, tq, n_heads):
    i = pl.program_id(1)
    ut = ut_ref[...]
    key = lax.broadcasted_iota(jnp.int32, (tq, tq), 0)
    qry = lax.broadcasted_iota(jnp.int32, (tq, tq), 1)
    past = key < qry
    qt = qt_ref[0]
    chan_head = lax.broadcasted_iota(jnp.int32, qt.shape, 0) // HEAD_DIM
    qh = [jnp.where(chan_head == h, qt, jnp.zeros_like(qt)) for h in range(n_heads)]
    hs = [slice(h * HEAD_DIM, (h + 1) * HEAD_DIM) for h in range(n_heads)]

    def block(j, runs, diagonal):
        start = pl.multiple_of(j * tq, tq)
        heads = range(n_heads)
        kb = k_ref[0, pl.ds(start, tq), :]
        zs = [_dot(kb, qh[h]) for h in heads]
        lks, lbs, pieces = [], [], []
        for h in heads:
            z = zs[h]
            sp = jnp.maximum(z, 0.0) + jnp.log(1.0 + jnp.exp(-jnp.abs(z)))
            lk = -sp
            if diagonal:
                lk = jnp.where(past, lk, 0.0)
            lks.append(lk)
            lbs.append(z - sp)
            pieces.append(_split2(lk))
        betweens = [_dot(ut, hi) + _dot(ut, lo) for hi, lo in pieces]
        atts = []
        for h in heads:
            att = jnp.exp(lbs[h] + betweens[h] + runs[h])
            if diagonal:
                att = jnp.where(past, att, 0.0)
            atts.append(att.astype(BF16))
        for h in heads:
            part = _dot(vt_ref[0, hs[h], pl.ds(start, tq)], atts[h])
            acc_s[hs[h], :] = part if diagonal else acc_s[hs[h], :] + part
        return tuple(runs[h] + jnp.sum(lks[h], axis=0, keepdims=True) for h in heads)

    def live(runs):
        top = functools.reduce(jnp.maximum, [jnp.max(r) for r in runs])
        return (top > SB_DEAD_LOG).astype(jnp.int32)

    runs = block(i, tuple(jnp.zeros((1, tq), F32) for _ in range(n_heads)), True)

    def cond(c):
        jj, go, _ = c
        return jnp.logical_and(jj < i, go > 0)

    def body(c):
        jj, _, runs = c
        runs = block(i - 1 - jj, runs, False)
        return jj + 1, live(runs), runs

    lax.while_loop(cond, body, (jnp.int32(0), live(runs), runs))
    o_ref[0] = acc_s[...].T


def _sb_attention(qt, k, vt):
    b, s, c = k.shape
    tq = SEQ_TILE
    ut = (jnp.arange(tq)[None, :] > jnp.arange(tq)[:, None]).astype(BF16)
    return pl.pallas_call(
        functools.partial(_sb_kernel, tq=tq, n_heads=N_HEADS),
        grid=(b, s // tq),
        in_specs=[
            pl.BlockSpec((1, c, tq), lambda bi, i: (bi, 0, i)),
            pl.BlockSpec((1, s, c), lambda bi, i: (bi, 0, 0)),
            pl.BlockSpec((1, c, s), lambda bi, i: (bi, 0, 0)),
            pl.BlockSpec((tq, tq), lambda bi, i: (0, 0)),
        ],
        out_specs=pl.BlockSpec((1, tq, c), lambda bi, i: (bi, i, 0)),
        out_shape=jax.ShapeDtypeStruct((b, s, c), F32),
        scratch_shapes=[pltpu.VMEM((c, tq), F32)],
        compiler_params=_params("parallel", "parallel"),
        name="sb_attention",
    )(qt, k, vt, ut)


def _rwkv_prep_kernel(*refs, has_vres):
    if has_vres:
        (p_ref, mu_ref, w0_ref, w2_ref, a0_ref, a2_ref, g2_ref, kk_ref, ka_ref, rk_ref, e_ref,
         vf_ref, v0_ref, v1_ref, v2_ref,
         r_o, w_o, k_o, v_o, an_o, bb_o, g_o, bonus_o, carry) = refs
    else:
        (p_ref, mu_ref, w0_ref, w2_ref, a0_ref, a2_ref, g2_ref, kk_ref, ka_ref, rk_ref, e_ref,
         r_o, w_o, k_o, v_o, an_o, bb_o, g_o, bonus_o, carry) = refs
    c = MIX_WIDTH
    p = p_ref[0]
    t = p.shape[0]

    @pl.when(pl.program_id(1) == 0)
    def _():
        carry[...] = jnp.zeros_like(carry)

    rowid = lax.broadcasted_iota(jnp.int32, p.shape, 0)
    prev = jnp.where(rowid == 0, carry[...], pltpu.roll(p, 1, axis=0))
    carry[...] = p[t - 1:t, :]
    xs = p + (prev - p) * mu_ref[...]

    r = xs[:, :c]
    k = xs[:, c:2 * c]
    v = xs[:, 2 * c:3 * c]
    lora = xs[:, 3 * c:3 * c + LANES]
    gl = xs[:, 3 * c + LANES:]
    w_log = -jax.nn.softplus(-(w0_ref[...] + _dot3(jnp.tanh(lora), w2_ref[...]))) - 0.5
    decay = jnp.exp(-jnp.exp(w_log))
    if has_vres:
        mix = jax.nn.sigmoid(v0_ref[...] + _dot3(_dot3(v, v1_ref[...]), v2_ref[...]))
        v = v + (vf_ref[0] - v) * mix
    a = jax.nn.sigmoid(a0_ref[...] + _dot3(lora, a2_ref[...]))
    g = _dot3(jax.nn.sigmoid(gl), g2_ref[...])
    e = e_ref[...]
    kk = k * kk_ref[...]
    kk = kk / jnp.maximum(jnp.sqrt(_dot_f32_lhs(kk * kk, e)), 1e-12)
    k = k * (1.0 + (a - 1.0) * ka_ref[...])
    bonus = _dot_f32_lhs(r * k * rk_ref[...], e) * v
    r_o[0] = r
    w_o[0] = decay
    k_o[0] = k
    v_o[0] = v
    an_o[0] = -kk
    bb_o[0] = kk * a
    g_o[0] = g
    bonus_o[0] = bonus


def _rwkv_prep(p, prm, e, v_first):
    b, s, cols = p.shape
    c = MIX_WIDTH
    t = SEQ_TILE
    has_vres = v_first is not None
    row = lambda n: pl.BlockSpec((1, n), lambda bi, j: (0, 0))
    mat = lambda m, n: pl.BlockSpec((m, n), lambda bi, j: (0, 0))
    seq = lambda n: pl.BlockSpec((1, t, n), lambda bi, j: (bi, j, 0))
    in_specs = [seq(cols), row(cols), row(c), mat(LANES, c), row(c), mat(LANES, c),
                mat(LANES, c), row(c), row(c), row(c), mat(c, c)]
    args = [p, prm["mu"], prm["w0"], prm["w2"], prm["a0"], prm["a2"], prm["g2"],
            prm["k_k"], prm["k_a"], prm["r_k"], e]
    if has_vres:
        in_specs += [seq(c), row(c), mat(c, LANES), mat(LANES, c)]
        args += [v_first, prm["v0"], prm["v1"], prm["v2"]]
    out = jax.ShapeDtypeStruct((b, s, c), F32)
    return pl.pallas_call(
        functools.partial(_rwkv_prep_kernel, has_vres=has_vres),
        grid=(b, s // t),
        in_specs=in_specs,
        out_specs=[seq(c)] * 8,
        out_shape=[out] * 8,
        scratch_shapes=[pltpu.VMEM((1, cols), F32)],
        compiler_params=_params("parallel", "arbitrary"),
        name="rwkv_prep",
    )(*args)


def _rwkv_rec_kernel(an_ref, w_ref, bb_ref, k_ref, r_ref, v_ref, y_ref, st_ref, ex_ref, *, n_tiles, group):
    @pl.when(pl.program_id(0) == 0)
    def _():
        st_ref[...] = jnp.zeros_like(st_ref)

    nb = an_ref.shape[0]
    per_head = LANES // N_HEADS
    lane = lax.broadcasted_iota(jnp.int32, (HEAD_DIM, LANES), 1)
    lane_b = lane // (N_HEADS * group)
    lane_h = (lane // group) % N_HEADS

    def packed(ref, t0):
        w = jnp.concatenate([ref[:, t0 + u, :] for u in range(group)], axis=0)
        pieces = []
        for h in range(N_HEADS):
            col, off = divmod(h * HEAD_DIM, LANES)
            piece = w[:, col * LANES:(col + 1) * LANES]
            pieces.append(pltpu.roll(piece, LANES - off, axis=1) if off else piece)
        return jnp.concatenate(pieces, axis=0).T[:HEAD_DIM, :]

    operands = (an_ref, w_ref, bb_ref, k_ref, r_ref)
    n_groups = an_ref.shape[1] // group

    def body(tp, packs):
        t0 = tp * group
        t_next = jnp.minimum(tp + 1, n_groups - 1) * group
        next_packs = tuple(packed(ref, t_next) for ref in operands)
        for u in range(group):
            idx = lane_h * per_head + u * nb + lane_b
            for o, p in enumerate(packs):
                ex_ref[u, o] = jnp.take_along_axis(p, idx, axis=1)
        vts = [v_ref[t0 + u] for u in range(group)]
        ys = [[] for _ in range(group)]
        for j in range(n_tiles):
            st = st_ref[j]
            for u in range(group):
                sa = jnp.sum(st * ex_ref[u, 0], axis=0, keepdims=True)
                st = st * ex_ref[u, 1] + sa * ex_ref[u, 2] + vts[u][j:j + 1, :] * ex_ref[u, 3]
                ys[u].append(jnp.sum(st * ex_ref[u, 4], axis=0, keepdims=True))
            st_ref[j] = st
        for u in range(group):
            y_ref[t0 + u] = jnp.concatenate(ys[u], axis=0)
        return next_packs

    lax.fori_loop(0, n_groups, body, tuple(packed(ref, 0) for ref in operands))


def _rwkv_recurrence(r, w, k, v, an, bb):
    b, s, c = r.shape
    group = LANES // (b * N_HEADS)
    n_tiles = HEAD_DIM // group
    ve = v.reshape(b, s, N_HEADS, n_tiles, group).transpose(1, 3, 0, 2, 4).reshape(s, n_tiles, LANES)
    big = pl.BlockSpec((b, REC_TILE, c), lambda i: (0, i, 0))
    small = pl.BlockSpec((REC_TILE, n_tiles, LANES), lambda i: (i, 0, 0))
    y = pl.pallas_call(
        functools.partial(_rwkv_rec_kernel, n_tiles=n_tiles, group=group),
        grid=(s // REC_TILE,),
        in_specs=[big] * 5 + [small],
        out_specs=small,
        out_shape=jax.ShapeDtypeStruct((s, n_tiles, LANES), F32),
        scratch_shapes=[pltpu.VMEM((n_tiles, HEAD_DIM, LANES), F32),
                        pltpu.VMEM((group, 5, HEAD_DIM, LANES), F32)],
        compiler_params=_params("arbitrary"),
        name="rwkv_recurrence",
    )(an, w, bb, k, r, ve)
    y = y.reshape(s, n_tiles, b, N_HEADS, group).transpose(2, 0, 3, 1, 4)
    return y.reshape(b, s, c)


def _rwkv_post_kernel(y_ref, g_ref, bonus_ref, gw_ref, gb_ref, e_ref, o_ref):
    y = y_ref[...]
    e = e_ref[...]
    mean = _dot_f32_lhs(y, e) * (1.0 / HEAD_DIM)
    yc = y - mean
    var = _dot_f32_lhs(yc * yc, e) * (1.0 / HEAD_DIM)
    yn = yc * lax.rsqrt(var + RWKV_GN_EPS) * gw_ref[...] + gb_ref[...]
    o_ref[...] = (yn + bonus_ref[...]) * g_ref[...]


def _rwkv_post(y, g, bonus, gn_w, gn_b, e):
    m, c = y.shape
    tile = pl.BlockSpec((ROW_TILE, c), lambda i: (i, 0))
    row = pl.BlockSpec((1, c), lambda i: (0, 0))
    return pl.pallas_call(
        _rwkv_post_kernel,
        grid=(m // ROW_TILE,),
        in_specs=[tile, tile, tile, row, row, pl.BlockSpec((c, c), lambda i: (0, 0))],
        out_specs=tile,
        out_shape=jax.ShapeDtypeStruct((m, c), F32),
        compiler_params=_params("parallel"),
        name="rwkv_post",
    )(y, g, bonus, gn_w, gn_b, e)


def _moba_kernel(qt_ref, k_ref, vt_ref, km_ref, o_ref, acc_s, *, blk, n_sel, n_heads):
    i = pl.program_id(1)
    n_pad = km_ref.shape[1]
    blk_id = lax.broadcasted_iota(jnp.int32, (n_pad, blk), 0)
    fully_past = blk_id < i
    weight = jnp.where(blk_id < 24, jnp.left_shift(1, jnp.minimum(blk_id, 23)), 0).astype(F32)
    key = lax.broadcasted_iota(jnp.int32, (blk, blk), 0)
    qry = lax.broadcasted_iota(jnp.int32, (blk, blk), 1)
    causal = key <= qry
    qt_all = qt_ref[0]
    km_all = km_ref[0]
    chan_head = lax.broadcasted_iota(jnp.int32, qt_all.shape, 0) // HEAD_DIM
    lane_head = lax.broadcasted_iota(jnp.int32, km_all.shape, 1) // HEAD_DIM
    q_scaled = (qt_all * (HEAD_DIM ** -0.5)).astype(BF16)
    hs = [slice(h * HEAD_DIM, (h + 1) * HEAD_DIM) for h in range(n_heads)]

    qs, bits = [], []
    for h in range(n_heads):
        gate = jnp.dot(jnp.where(lane_head == h, km_all, 0.0), qt_all, preferred_element_type=F32,
                       precision=lax.Precision.HIGHEST)
        gate = jnp.where(fully_past, gate, MASK_VALUE)
        chosen = jnp.zeros(gate.shape, F32)
        for _ in range(n_sel):
            top = jnp.max(gate, axis=0, keepdims=True)
            idx = jnp.min(jnp.where(gate == top, blk_id, n_pad), axis=0, keepdims=True)
            pick = blk_id == idx
            chosen = jnp.where(pick & fully_past, 1.0, chosen)
            gate = jnp.where(pick, -jnp.inf, gate)
        bits.append(jnp.sum(chosen * weight, axis=0, keepdims=True).astype(jnp.int32))
        qs.append(jnp.where(chan_head == h, q_scaled, jnp.zeros_like(q_scaled)))

    def scores(n, h):
        start = pl.multiple_of(n * blk, blk)
        return _dot(k_ref[0, pl.ds(start, blk), :], qs[h]), vt_ref[0, hs[h], pl.ds(start, blk)]

    heads = range(n_heads)
    ms, ls, ps = [], [], []
    own = [scores(i, h) for h in heads]
    for h in heads:
        s = jnp.where(causal, own[h][0], MASK_VALUE)
        m = jnp.max(s, axis=0, keepdims=True)
        p = jnp.exp(s - m)
        ms.append(m)
        ls.append(jnp.sum(p, axis=0, keepdims=True))
        ps.append(p.astype(BF16))
    for h in heads:
        acc_s[hs[h], :] = _dot(own[h][1], ps[h])

    def body(n, carry):
        ms, ls = carry
        blocks = [scores(n, h) for h in heads]
        new_ms, new_ls, alphas, ps = [], [], [], []
        for h in heads:
            sel = (jnp.right_shift(bits[h], n) & 1) == 1
            s = jnp.where(sel, blocks[h][0], MASK_VALUE)
            m_new = jnp.maximum(ms[h], jnp.max(s, axis=0, keepdims=True))
            alpha = jnp.exp(ms[h] - m_new)
            p = jnp.exp(s - m_new)
            new_ms.append(m_new)
            new_ls.append(alpha * ls[h] + jnp.sum(p, axis=0, keepdims=True))
            alphas.append(alpha)
            ps.append(p.astype(BF16))
        for h in heads:
            acc_s[hs[h], :] = alphas[h] * acc_s[hs[h], :] + _dot(blocks[h][1], ps[h])
        return tuple(new_ms), tuple(new_ls)

    ms, ls = lax.fori_loop(0, i, body, (tuple(ms), tuple(ls)))
    for h in range(n_heads):
        acc_s[hs[h], :] = acc_s[hs[h], :] / ls[h]
    o_ref[0] = acc_s[...].T


def _moba_attention(qt, k, vt, km):
    b, s, c = k.shape
    blk = MOBA_BLOCK
    n_blk = s // blk
    n_pad = km.shape[1]
    assert s % blk == 0 and n_blk <= 24
    n_sel = min(MOBA_TOPK, n_blk)
    return pl.pallas_call(
        functools.partial(_moba_kernel, blk=blk, n_sel=n_sel, n_heads=N_HEADS),
        grid=(b, n_blk),
        in_specs=[
            pl.BlockSpec((1, c, blk), lambda bi, i: (bi, 0, i)),
            pl.BlockSpec((1, s, c), lambda bi, i: (bi, 0, 0)),
            pl.BlockSpec((1, c, s), lambda bi, i: (bi, 0, 0)),
            pl.BlockSpec((1, n_pad, c), lambda bi, i: (bi, 0, 0)),
        ],
        out_specs=pl.BlockSpec((1, blk, c), lambda bi, i: (bi, i, 0)),
        out_shape=jax.ShapeDtypeStruct((b, s, c), F32),
        scratch_shapes=[pltpu.VMEM((c, blk), F32)],
        compiler_params=_params("parallel", "parallel"),
        name="moba_attention",
    )(qt, k, vt, km)


def _hgrn_kernel(q_ref, f_ref, i_ref, g_ref, lb_ref, gn_ref, e_ref, tri_ref, ones_ref, hm_ref,
                 o_ref, st_ref, q_s, k_s, b_s, v_s, qd_s, ke_s, be_s, o_s, *, chunk):
    c = MIX_WIDTH
    rows = q_ref.shape[1]

    @pl.when(pl.program_id(1) == 0)
    def _():
        st_ref[...] = jnp.zeros_like(st_ref)

    lb = lb_ref[...]
    sig = jax.nn.sigmoid(f_ref[0])
    log_f = jnp.log(lb + (1.0 - lb) * sig)
    key = (1.0 - lb) * (1.0 - sig)
    b = _dot_f32_rhs(tri_ref[...], log_f)
    b_end = _dot_f32_rhs(ones_ref[...], log_f)
    q = q_ref[0]
    q_s[...] = q
    k_s[...] = key
    b_s[...] = b
    v_s[...] = i_ref[0]
    qd_s[...] = q * jnp.exp(b)
    ke_s[...] = key * jnp.exp(b_end - b)
    be_s[...] = b_end
    e = e_ref[...]
    head_mask = hm_ref[...]
    sub = lax.broadcasted_iota(jnp.int32, (chunk, c), 0)

    def body(n, carry):
        r0 = pl.multiple_of(n * chunk, chunk)
        sl = pl.ds(r0, chunk)
        qi = q_s[sl, :]
        ki = k_s[sl, :]
        bi = b_s[sl, :]
        vi = v_s[sl, :]
        parts = []
        for s in range(chunk):
            pr = qi * ki[s:s + 1, :] * jnp.exp(jnp.minimum(bi - bi[s:s + 1, :], 0.0))
            parts.append(jnp.where(sub >= s, pr, 0.0))
        sc = _dot_f32_lhs(jnp.concatenate(parts, axis=0), e)
        o = jnp.zeros((chunk, c), F32)
        for s in range(chunk):
            o = o + sc[s * chunk:(s + 1) * chunk, :] * vi[s:s + 1, :]
        st = st_ref[...]
        o = o + lax.dot_general(qd_s[sl, :].astype(BF16), st.astype(BF16),
                                (((1,), (1,)), ((), ())), preferred_element_type=F32)
        o_s[sl, :] = o
        kv = lax.dot_general(vi.astype(BF16), ke_s[sl, :].astype(BF16),
                             (((0,), (0,)), ((), ())), preferred_element_type=F32)
        st_ref[...] = st * jnp.exp(be_s[pl.ds(r0, 1), :]) + kv * head_mask
        return carry

    lax.fori_loop(0, rows // chunk, body, 0, unroll=4)
    o = o_s[...]
    ms = _dot_f32_lhs(o * o, e) * (1.0 / HEAD_DIM)
    g = g_ref[0]
    o_ref[0] = o * lax.rsqrt(ms + NORM_EPS) * gn_ref[...] * (g * jax.nn.sigmoid(g))


def _hgrn(hc, lb, gn_g, e):
    b, s, _ = hc.shape
    c = MIX_WIDTH
    t = SEQ_TILE
    ch = HGRN_CHUNK
    ridx = jnp.arange(t)
    same = (ridx[:, None] // ch) == (ridx[None, :] // ch)
    tri = (same & (ridx[None, :] <= ridx[:, None])).astype(BF16)
    ones = same.astype(BF16)
    seq = lambda n: pl.BlockSpec((1, t, c), lambda bi, j: (bi, j, n))
    row = pl.BlockSpec((1, c), lambda bi, j: (0, 0))
    mat = lambda n: pl.BlockSpec((n, n), lambda bi, j: (0, 0))
    return pl.pallas_call(
        functools.partial(_hgrn_kernel, chunk=ch),
        grid=(b, s // t),
        in_specs=[seq(0), seq(1), seq(2), seq(3), row, row, mat(c), mat(t), mat(t), mat(c)],
        out_specs=pl.BlockSpec((1, t, c), lambda bi, j: (bi, j, 0)),
        out_shape=jax.ShapeDtypeStruct((b, s, c), F32),
        scratch_shapes=[pltpu.VMEM((c, c), F32)] + [pltpu.VMEM((t, c), F32)] * 8,
        compiler_params=_params("parallel", "arbitrary"),
        name="hgrn2",
    )(hc, hc, hc, hc, lb.reshape(1, c), gn_g.reshape(1, c), e, tri, ones, e.astype(F32))


def _merge_kernel(x_ref, g_ref, wg_ref, y0_ref, y1_ref, y2_ref, y3_ref, wb_ref, wo_ref, o_ref):
    x = x_ref[...]
    d = x.shape[1]
    h = _rms(x, g_ref[...]).astype(BF16)
    merged = None
    for n, y_ref in enumerate((y0_ref, y1_ref, y2_ref, y3_ref)):
        gate = jax.nn.sigmoid(_dot(h, wg_ref[:, n * d:(n + 1) * d]))
        term = gate * _dot(y_ref[...].astype(BF16), wb_ref[n])
        merged = term if merged is None else merged + term
    o_ref[...] = x + _dot(merged.astype(BF16), wo_ref[...])


def _merge(x2, g, w_gate, ys, w_branch, w_out):
    m, d = x2.shape
    c = MIX_WIDTH
    tile = pl.BlockSpec((ROW_TILE, d), lambda i: (i, 0))
    ytile = pl.BlockSpec((ROW_TILE, c), lambda i: (i, 0))
    return pl.pallas_call(
        _merge_kernel,
        grid=(m // ROW_TILE,),
        in_specs=[tile, pl.BlockSpec((1, d), lambda i: (0, 0)),
                  pl.BlockSpec((d, N_BRANCH * d), lambda i: (0, 0)),
                  ytile, ytile, ytile, ytile,
                  pl.BlockSpec((N_BRANCH, c, d), lambda i: (0, 0, 0)),
                  pl.BlockSpec((d, d), lambda i: (0, 0))],
        out_specs=tile,
        out_shape=jax.ShapeDtypeStruct((m, d), F32),
        compiler_params=_params("parallel"),
        name="gated_merge",
    )(x2, g.reshape(1, d), w_gate, *ys, w_branch, w_out)


def _ffn_kernel(*refs, final):
    if final:
        x_ref, g_ref, wg_ref, wu_ref, wd_ref, fg_ref, o_ref = refs
    else:
        x_ref, g_ref, wg_ref, wu_ref, wd_ref, o_ref = refs
    x = x_ref[...]
    h = _rms(x, g_ref[...]).astype(BF16)
    gate = _dot(h, wg_ref[...])
    act = gate * jax.nn.sigmoid(gate) * _dot(h, wu_ref[...])
    y = x + _dot(act.astype(BF16), wd_ref[...])
    o_ref[...] = _rms(y, fg_ref[...]) if final else y


def _ffn(x2, g, w_gate, w_up, w_down, final_g):
    m, d = x2.shape
    dff = w_gate.shape[1]
    final = final_g is not None
    tile = pl.BlockSpec((ROW_TILE, d), lambda i: (i, 0))
    row = pl.BlockSpec((1, d), lambda i: (0, 0))
    resident = lambda r, c: pl.BlockSpec((r, c), lambda i: (0, 0), pipeline_mode=pl.Buffered(1))
    in_specs = [tile, row, resident(d, dff), resident(d, dff), resident(dff, d)]
    args = [x2, g.reshape(1, d), w_gate, w_up, w_down]
    if final:
        in_specs.append(row)
        args.append(final_g.reshape(1, d))
    return pl.pallas_call(
        functools.partial(_ffn_kernel, final=final),
        grid=(m // ROW_TILE,),
        in_specs=in_specs,
        out_specs=tile,
        out_shape=jax.ShapeDtypeStruct((m, d), F32),
        compiler_params=_params("parallel"),
        name="ffn",
    )(*args)


def _rope_tables(s):
    half = ROPE_DIM // 2
    inv_freq = ROPE_THETA ** (-jnp.arange(0, ROPE_DIM, 2, dtype=F32) / ROPE_DIM)
    ang = jnp.arange(s, dtype=F32)[:, None] * inv_freq[None, :]
    cos, sin = jnp.cos(ang), jnp.sin(ang)
    pad = HEAD_DIM - ROPE_DIM
    cos_h = jnp.concatenate([cos, cos, jnp.ones((s, pad), F32)], axis=1)
    sin_h = jnp.concatenate([-sin, sin, jnp.zeros((s, pad), F32)], axis=1)
    return jnp.tile(cos_h, (1, N_HEADS)), jnp.tile(sin_h, (1, N_HEADS))


def kernel(x, norm1_g, w_in, rwkv_mu, rwkv_w0, rwkv_w2, rwkv_a0, rwkv_a2, rwkv_g2, rwkv_k_k, rwkv_k_a, rwkv_r_k, rwkv_gn_w, rwkv_gn_b, rwkv_v0, rwkv_v1, rwkv_v2, hgrn_lb_logits, hgrn_gn_g, w_branch, w_out, norm2_g, w_ffn_gate, w_ffn_up, w_ffn_down, final_g):
    b, s, d = x.shape
    depth = w_in.shape[0]
    c = MIX_WIDTH
    m = b * s
    assert d == D_MODEL and s % SEQ_TILE == 0 and m % ROW_TILE == 0 and s % REC_TILE == 0
    assert LANES % (b * N_HEADS) == 0 and HEAD_DIM % (LANES // (b * N_HEADS)) == 0

    cos, sin = _rope_tables(s)
    lb_w = jax.nn.softmax(hgrn_lb_logits.astype(F32), axis=0)
    lower_bounds = jnp.cumsum(lb_w, axis=0) - lb_w[0]
    head_id = jnp.arange(c) // HEAD_DIM
    e = (head_id[:, None] == head_id[None, :]).astype(BF16)
    row = lambda t: t.reshape(1, -1)
    zpad = lambda t, top, bottom: jnp.pad(t, ((top, bottom), (0, 0)))

    x2 = x.reshape(m, d)
    v_first = None
    for layer in range(depth):
        w_l = w_in[layer].astype(BF16)
        g1 = norm1_g[layer]
        rp, hc, k_sb, qt_sb, vt_sb, k_m, qt_m, vt_m, k_mean = _in_proj(
            x2.reshape(b, s, d), g1, w_l, cos, sin)

        y_sb = _sb_attention(qt_sb, k_sb, vt_sb).reshape(m, c)

        prm = {
            "mu": row(rwkv_mu[layer]), "w0": row(rwkv_w0[layer]),
            "w2": zpad(rwkv_w2[layer], 0, LANES - RWKV_DECAY_LORA),
            "a0": row(rwkv_a0[layer]),
            "a2": zpad(rwkv_a2[layer], RWKV_DECAY_LORA, LANES - RWKV_DECAY_LORA - RWKV_AAA_LORA),
            "g2": rwkv_g2[layer], "k_k": row(rwkv_k_k[layer]), "k_a": row(rwkv_k_a[layer]),
            "r_k": row(rwkv_r_k[layer]),
        }
        if layer > 0:
            prm["v0"] = row(rwkv_v0[layer - 1])
            prm["v1"] = jnp.pad(rwkv_v1[layer - 1], ((0, 0), (0, LANES - RWKV_MV_LORA)))
            prm["v2"] = zpad(rwkv_v2[layer - 1], 0, LANES - RWKV_MV_LORA)
        r_, w_, k_, v_, an_, bb_, g_, bonus_ = _rwkv_prep(rp, prm, e, v_first)
        if layer == 0:
            v_first = v_
        y_rec = _rwkv_recurrence(r_, w_, k_, v_, an_, bb_)
        y_rwkv = _rwkv_post(y_rec.reshape(m, c), g_.reshape(m, c), bonus_.reshape(m, c),
                            row(rwkv_gn_w[layer]), row(rwkv_gn_b[layer]), e)

        n_blk = k_mean.shape[1]
        k_mean = jnp.pad(k_mean.reshape(b, n_blk, c), ((0, 0), (0, -n_blk % 8), (0, 0)))
        y_moba = _moba_attention(qt_m, k_m, vt_m, k_mean).reshape(m, c)

        y_hgrn = _hgrn(hc, lower_bounds[layer], hgrn_gn_g[layer], e).reshape(m, c)

        x2 = _merge(x2, g1, w_l[:, GATE_OFF:], (y_sb, y_rwkv, y_moba, y_hgrn),
                    w_branch[layer].astype(BF16), w_out[layer].astype(BF16))
        x2 = _ffn(x2, norm2_g[layer], w_ffn_gate[layer].astype(BF16), w_ffn_up[layer].astype(BF16),
                  w_ffn_down[layer].astype(BF16), final_g if layer == depth - 1 else None)
    return x2.reshape(b, s, d)
```

```python
import functools

import jax
import jax.numpy as jnp
from jax import lax
from jax.experimental import pallas as pl
from jax.experimental.pallas import tpu as pltpu

F32 = jnp.float32
BF16 = jnp.bfloat16

D_MODEL = 1024
HEAD_DIM = 64
N_HEADS = 4
MIX_WIDTH = N_HEADS * HEAD_DIM
N_BRANCH = 4
NORM_EPS = 1e-6
MASK_VALUE = -1e30

RWKV_DECAY_LORA = 64
RWKV_AAA_LORA = 64
RWKV_MV_LORA = 32
RWKV_GATE_LORA = 128
RWKV_GN_EPS = 64e-5
RWKV_COLS = 3 * MIX_WIDTH + RWKV_DECAY_LORA + RWKV_AAA_LORA + RWKV_GATE_LORA

MOBA_BLOCK = 256
MOBA_TOPK = 3
ROPE_THETA = 500000.0
ROPE_DIM = HEAD_DIM // 4

HGRN_CHUNK = 16

SB_COLS = 3 * MIX_WIDTH
MOBA_COLS = 3 * MIX_WIDTH
HGRN_COLS = 4 * MIX_WIDTH
SB_OFF = 0
RWKV_OFF = SB_OFF + SB_COLS
MOBA_OFF = RWKV_OFF + RWKV_COLS
HGRN_OFF = MOBA_OFF + MOBA_COLS
GATE_OFF = HGRN_OFF + HGRN_COLS

LANES = 128
VMEM_LIMIT = 56 * 1024 * 1024

ROW_TILE = 512
SEQ_TILE = 256
REC_TILE = 128


def _params(*sem):
    return pltpu.CompilerParams(dimension_semantics=sem, vmem_limit_bytes=VMEM_LIMIT)


def _dot(a, b):
    return jnp.dot(a, b, preferred_element_type=F32)


def _split2(a):
    hi = a.astype(BF16)
    lo = (a - hi.astype(F32)).astype(BF16)
    return hi, lo


def _split3(a):
    hi = a.astype(BF16)
    r1 = a - hi.astype(F32)
    mid = r1.astype(BF16)
    lo = (r1 - mid.astype(F32)).astype(BF16)
    return hi, mid, lo


def _dot_f32_lhs(a, m):
    hi, lo = _split2(a)
    return _dot(hi, m) + _dot(lo, m)


def _dot_f32_rhs(m, a):
    hi, mid, lo = _split3(a)
    return _dot(m, hi) + _dot(m, mid) + _dot(m, lo)


def _dot3(a, b):
    ah, al = _split2(a)
    bh, bl = _split2(b)
    return _dot(ah, bh) + _dot(ah, bl) + _dot(al, bh)


def _rms(x, g):
    return x * lax.rsqrt(jnp.mean(x * x, axis=-1, keepdims=True) + NORM_EPS) * g


_NT = (((1,), (1,)), ((), ()))


def _in_proj_kernel(x_ref, g_ref, wr_ref, wh_ref, sk_ref, sqt_ref, svt_ref, mk_ref, mqt_ref, mvt_ref,
                    cos_ref, sin_ref, cost_ref, sint_ref,
                    rw_o, hg_o, sk_o, sqt_o, svt_o, mk_o, mqt_o, mvt_o, km_o):
    h = _rms(x_ref[0], g_ref[...]).astype(BF16)
    nt = lambda w_ref: lax.dot_general(w_ref[...], h, _NT, preferred_element_type=F32)
    rw_o[0] = _dot(h, wr_ref[...])
    hg_o[0] = _dot(h, wh_ref[...])
    sk_o[0] = _dot(h, sk_ref[...]).astype(BF16)
    sqt_o[0] = nt(sqt_ref).astype(BF16)
    svt_o[0] = nt(svt_ref).astype(BF16)
    mvt_o[0] = nt(mvt_ref).astype(BF16)
    k = _dot(h, mk_ref[...])
    qt = nt(mqt_ref)
    c = k.shape[1]
    half = ROPE_DIM // 2
    lane = lax.broadcasted_iota(jnp.int32, k.shape, 1) % HEAD_DIM
    partner = jnp.where(lane < half, pltpu.roll(k, c - half, axis=1), pltpu.roll(k, half, axis=1))
    k = k * cos_ref[...] + partner * sin_ref[...]
    chan = lax.broadcasted_iota(jnp.int32, qt.shape, 0) % HEAD_DIM
    partner = jnp.where(chan < half, pltpu.roll(qt, c - half, axis=0), pltpu.roll(qt, half, axis=0))
    mqt_o[0] = qt * cost_ref[...] + partner * sint_ref[...]
    mk_o[0] = k.astype(BF16)
    for i in range(k.shape[0] // MOBA_BLOCK):
        km_o[0, i] = jnp.mean(k[i * MOBA_BLOCK:(i + 1) * MOBA_BLOCK], axis=0, keepdims=True)


def _in_proj(x3, g, w_l, cos, sin):
    b, s, d = x3.shape
    c = MIX_WIDTH
    t = ROW_TILE
    assert t % MOBA_BLOCK == 0
    resident = lambda m, n: pl.BlockSpec((m, n), lambda bi, j: (0, 0), pipeline_mode=pl.Buffered(1))
    rows = lambda n: pl.BlockSpec((1, t, n), lambda bi, j: (bi, j, 0))
    cols = pl.BlockSpec((1, c, t), lambda bi, j: (bi, 0, j))
    sb, mo = w_l[:, SB_OFF:SB_OFF + SB_COLS], w_l[:, MOBA_OFF:MOBA_OFF + MOBA_COLS]
    args = [x3, g.reshape(1, d), w_l[:, RWKV_OFF:RWKV_OFF + RWKV_COLS], w_l[:, HGRN_OFF:HGRN_OFF + HGRN_COLS],
            sb[:, c:2 * c], (sb[:, :c] * (HEAD_DIM ** -0.5)).T, sb[:, 2 * c:].T,
            mo[:, c:2 * c], mo[:, :c].T, mo[:, 2 * c:].T, cos, sin, cos.T, sin.T]
    in_specs = [rows(d), resident(1, d), resident(d, RWKV_COLS), resident(d, HGRN_COLS),
                resident(d, c), resident(c, d), resident(c, d),
                resident(d, c), resident(c, d), resident(c, d),
                pl.BlockSpec((t, c), lambda bi, j: (j, 0)), pl.BlockSpec((t, c), lambda bi, j: (j, 0)),
                pl.BlockSpec((c, t), lambda bi, j: (0, j)), pl.BlockSpec((c, t), lambda bi, j: (0, j))]
    n_mean = t // MOBA_BLOCK
    out_specs = [rows(RWKV_COLS), rows(HGRN_COLS), rows(c), cols, cols, rows(c), cols, cols,
                 pl.BlockSpec((1, n_mean, 1, c), lambda bi, j: (bi, j, 0, 0))]
    sds = jax.ShapeDtypeStruct
    out_shape = [sds((b, s, RWKV_COLS), F32), sds((b, s, HGRN_COLS), F32),
                 sds((b, s, c), BF16), sds((b, c, s), BF16), sds((b, c, s), BF16),
                 sds((b, s, c), BF16), sds((b, c, s), F32), sds((b, c, s), BF16),
                 sds((b, s // MOBA_BLOCK, 1, c), F32)]
    return pl.pallas_call(
        _in_proj_kernel,
        grid=(b, s // t),
        in_specs=in_specs,
        out_specs=out_specs,
        out_shape=out_shape,
        compiler_params=_params("parallel", "parallel"),
        name="in_proj",
    )(*args)


SB_DEAD_LOG = -110.0


def _sb_kernel(qt_ref, k_ref, vt_ref, ut_ref, o_ref, acc_s, *, tq, n_heads):
    i = pl.program_id(1)
    ut = ut_ref[...]
    key = lax.broadcasted_iota(jnp.int32, (tq, tq), 0)
    qry = lax.broadcasted_iota(jnp.int32, (tq, tq), 1)
    past = key < qry
    qt = qt_ref[0]
    chan_head = lax.broadcasted_iota(jnp.int32, qt.shape, 0) // HEAD_DIM
    qh = [jnp.where(chan_head == h, qt, jnp.zeros_like(qt)) for h in range(n_heads)]
    hs = [slice(h * HEAD_DIM, (h + 1) * HEAD_DIM) for h in range(n_heads)]

    def block(j, runs, diagonal):
        start = pl.multiple_of(j * tq, tq)
        heads = range(n_heads)
        kb = k_ref[0, pl.ds(start, tq), :]
        zs = [_dot(kb, qh[h]) for h in heads]
        lks, lbs, pieces = [], [], []
        for h in heads:
            z = zs[h]
            sp = jnp.maximum(z, 0.0) + jnp.log(1.0 + jnp.exp(-jnp.abs(z)))
            lk = -sp
            if diagonal:
                lk = jnp.where(past, lk, 0.0)
            lks.append(lk)
            lbs.append(z - sp)
            pieces.append(_split2(lk))
        betweens = [_dot(ut, hi) + _dot(ut, lo) for hi, lo in pieces]
        atts = []
        for h in heads:
            att = jnp.exp(lbs[h] + betweens[h] + runs[h])
            if diagonal:
                att = jnp.where(past, att, 0.0)
            atts.append(att.astype(BF16))
        for h in heads:
            part = _dot(vt_ref[0, hs[h], pl.ds(start, tq)], atts[h])
            acc_s[hs[h], :] = part if diagonal else acc_s[hs[h], :] + part
        return tuple(runs[h] + jnp.sum(lks[h], axis=0, keepdims=True) for h in heads)

    def live(runs):
        top = functools.reduce(jnp.maximum, [jnp.max(r) for r in runs])
        return (top > SB_DEAD_LOG).astype(jnp.int32)

    runs = block(i, tuple(jnp.zeros((1, tq), F32) for _ in range(n_heads)), True)

    def cond(c):
        jj, go, _ = c
        return jnp.logical_and(jj < i, go > 0)

    def body(c):
        jj, _, runs = c
        runs = block(i - 1 - jj, runs, False)
        return jj + 1, live(runs), runs

    lax.while_loop(cond, body, (jnp.int32(0), live(runs), runs))
    o_ref[0] = acc_s[...].T


def _sb_attention(qt, k, vt):
    b, s, c = k.shape
    tq = SEQ_TILE
    ut = (jnp.arange(tq)[None, :] > jnp.arange(tq)[:, None]).astype(BF16)
    return pl.pallas_call(
        functools.partial(_sb_kernel, tq=tq, n_heads=N_HEADS),
        grid=(b, s // tq),
        in_specs=[
            pl.BlockSpec((1, c, tq), lambda bi, i: (bi, 0, i)),
            pl.BlockSpec((1, s, c), lambda bi, i: (bi, 0, 0)),
            pl.BlockSpec((1, c, s), lambda bi, i: (bi, 0, 0)),
            pl.BlockSpec((tq, tq), lambda bi, i: (0, 0)),
        ],
        out_specs=pl.BlockSpec((1, tq, c), lambda bi, i: (bi, i, 0)),
        out_shape=jax.ShapeDtypeStruct((b, s, c), F32),
        scratch_shapes=[pltpu.VMEM((c, tq), F32)],
        compiler_params=_params("parallel", "parallel"),
        name="sb_attention",
    )(qt, k, vt, ut)


def _rwkv_prep_kernel(*refs, has_vres):
    if has_vres:
        (p_ref, mu_ref, w0_ref, w2_ref, a0_ref, a2_ref, g2_ref, kk_ref, ka_ref, rk_ref, e_ref,
         tri_ref, ones_ref, vf_ref, v0_ref, v1_ref, v2_ref,
         r_o, w_o, k_o, v_o, an_o, bb_o, g_o, bonus_o, carry) = refs
    else:
        (p_ref, mu_ref, w0_ref, w2_ref, a0_ref, a2_ref, g2_ref, kk_ref, ka_ref, rk_ref, e_ref,
         tri_ref, ones_ref,
         r_o, w_o, k_o, v_o, an_o, bb_o, g_o, bonus_o, carry) = refs
    c = MIX_WIDTH
    p = p_ref[0]
    t = p.shape[0]

    @pl.when(pl.program_id(1) == 0)
    def _():
        carry[...] = jnp.zeros_like(carry)

    rowid = lax.broadcasted_iota(jnp.int32, p.shape, 0)
    prev = jnp.where(rowid == 0, carry[...], pltpu.roll(p, 1, axis=0))
    carry[...] = p[t - 1:t, :]
    xs = p + (prev - p) * mu_ref[...]

    r = xs[:, :c]
    k = xs[:, c:2 * c]
    v = xs[:, 2 * c:3 * c]
    lora = xs[:, 3 * c:3 * c + LANES]
    gl = xs[:, 3 * c + LANES:]
    w_log = -jax.nn.softplus(-(w0_ref[...] + _dot3(jnp.tanh(lora), w2_ref[...]))) - 0.5
    log_decay = -jnp.exp(w_log)
    cum = _dot_f32_rhs(tri_ref[...], log_decay)
    cum_end = _dot_f32_rhs(ones_ref[...], log_decay)
    if has_vres:
        mix = jax.nn.sigmoid(v0_ref[...] + _dot3(_dot3(v, v1_ref[...]), v2_ref[...]))
        v = v + (vf_ref[0] - v) * mix
    a = jax.nn.sigmoid(a0_ref[...] + _dot3(lora, a2_ref[...]))
    g = _dot3(jax.nn.sigmoid(gl), g2_ref[...])
    e = e_ref[...]
    kk = k * kk_ref[...]
    kk = kk / jnp.maximum(jnp.sqrt(_dot_f32_lhs(kk * kk, e)), 1e-12)
    k = k * (1.0 + (a - 1.0) * ka_ref[...])
    bonus = _dot_f32_lhs(r * k * rk_ref[...], e) * v
    inv_gamma = jnp.exp(-cum)
    r_o[0] = r * jnp.exp(cum)
    w_o[0] = jnp.exp(cum_end)
    k_o[0] = k * inv_gamma
    v_o[0] = v
    an_o[0] = -kk * jnp.exp(cum - log_decay)
    bb_o[0] = kk * a * inv_gamma
    g_o[0] = g
    bonus_o[0] = bonus


def _rwkv_prep(p, prm, e, v_first):
    b, s, cols = p.shape
    c = MIX_WIDTH
    t = SEQ_TILE
    group = LANES // (b * N_HEADS)
    has_vres = v_first is not None
    ridx = jnp.arange(t)
    same = (ridx[:, None] // group) == (ridx[None, :] // group)
    tri = (same & (ridx[None, :] <= ridx[:, None])).astype(BF16)
    row = lambda n: pl.BlockSpec((1, n), lambda bi, j: (0, 0))
    mat = lambda m, n: pl.BlockSpec((m, n), lambda bi, j: (0, 0))
    seq = lambda n: pl.BlockSpec((1, t, n), lambda bi, j: (bi, j, 0))
    in_specs = [seq(cols), row(cols), row(c), mat(LANES, c), row(c), mat(LANES, c),
                mat(LANES, c), row(c), row(c), row(c), mat(c, c), mat(t, t), mat(t, t)]
    args = [p, prm["mu"], prm["w0"], prm["w2"], prm["a0"], prm["a2"], prm["g2"],
            prm["k_k"], prm["k_a"], prm["r_k"], e, tri, same.astype(BF16)]
    if has_vres:
        in_specs += [seq(c), row(c), mat(c, LANES), mat(LANES, c)]
        args += [v_first, prm["v0"], prm["v1"], prm["v2"]]
    out = jax.ShapeDtypeStruct((b, s, c), F32)
    return pl.pallas_call(
        functools.partial(_rwkv_prep_kernel, has_vres=has_vres),
        grid=(b, s // t),
        in_specs=in_specs,
        out_specs=[seq(c)] * 8,
        out_shape=[out] * 8,
        scratch_shapes=[pltpu.VMEM((1, cols), F32)],
        compiler_params=_params("parallel", "arbitrary"),
        name="rwkv_prep",
    )(*args)


def _rwkv_rec_kernel(an_ref, w_ref, bb_ref, k_ref, r_ref, v_ref, y_ref, st_ref, ex_ref, *, n_tiles, group):
    @pl.when(pl.program_id(0) == 0)
    def _():
        st_ref[...] = jnp.zeros_like(st_ref)

    nb = an_ref.shape[0]
    per_head = LANES // N_HEADS
    lane = lax.broadcasted_iota(jnp.int32, (HEAD_DIM, LANES), 1)
    lane_b = lane // (N_HEADS * group)
    lane_h = (lane // group) % N_HEADS

    def packed(ref, t0):
        w = jnp.concatenate([ref[:, t0 + u, :] for u in range(group)], axis=0)
        pieces = []
        for h in range(N_HEADS):
            col, off = divmod(h * HEAD_DIM, LANES)
            piece = w[:, col * LANES:(col + 1) * LANES]
            pieces.append(pltpu.roll(piece, LANES - off, axis=1) if off else piece)
        return jnp.concatenate(pieces, axis=0).T[:HEAD_DIM, :]

    operands = (an_ref, w_ref, bb_ref, k_ref, r_ref)
    n_groups = an_ref.shape[1] // group

    def body(tp, packs):
        t0 = tp * group
        t_next = jnp.minimum(tp + 1, n_groups - 1) * group
        next_packs = tuple(packed(ref, t_next) for ref in operands)
        for u in range(group):
            idx = lane_h * per_head + u * nb + lane_b
            for o, p in enumerate(packs):
                if o != 1 or u == 0:
                    ex_ref[u, o] = jnp.take_along_axis(p, idx, axis=1)
        vts = [v_ref[t0 + u] for u in range(group)]
        ys = [[] for _ in range(group)]
        for j in range(n_tiles):
            st = st_ref[j]
            for u in range(group):
                sa = jnp.sum(st * ex_ref[u, 0], axis=0, keepdims=True)
                st = st + sa * ex_ref[u, 2] + vts[u][j:j + 1, :] * ex_ref[u, 3]
                ys[u].append(jnp.sum(st * ex_ref[u, 4], axis=0, keepdims=True))
            st_ref[j] = st * ex_ref[0, 1]
        for u in range(group):
            y_ref[t0 + u] = jnp.concatenate(ys[u], axis=0)
        return next_packs

    lax.fori_loop(0, n_groups, body, tuple(packed(ref, 0) for ref in operands))


def _rwkv_recurrence(r, w, k, v, an, bb):
    b, s, c = r.shape
    group = LANES // (b * N_HEADS)
    n_tiles = HEAD_DIM // group
    ve = v.reshape(b, s, N_HEADS, n_tiles, group).transpose(1, 3, 0, 2, 4).reshape(s, n_tiles, LANES)
    big = pl.BlockSpec((b, REC_TILE, c), lambda i: (0, i, 0))
    small = pl.BlockSpec((REC_TILE, n_tiles, LANES), lambda i: (i, 0, 0))
    y = pl.pallas_call(
        functools.partial(_rwkv_rec_kernel, n_tiles=n_tiles, group=group),
        grid=(s // REC_TILE,),
        in_specs=[big] * 5 + [small],
        out_specs=small,
        out_shape=jax.ShapeDtypeStruct((s, n_tiles, LANES), F32),
        scratch_shapes=[pltpu.VMEM((n_tiles, HEAD_DIM, LANES), F32),
                        pltpu.VMEM((group, 5, HEAD_DIM, LANES), F32)],
        compiler_params=_params("arbitrary"),
        name="rwkv_recurrence",
    )(an, w, bb, k, r, ve)
    y = y.reshape(s, n_tiles, b, N_HEADS, group).transpose(2, 0, 3, 1, 4)
    return y.reshape(b, s, c)


def _rwkv_post_kernel(y_ref, g_ref, bonus_ref, gw_ref, gb_ref, e_ref, o_ref):
    y = y_ref[...]
    e = e_ref[...]
    mean = _dot_f32_lhs(y, e) * (1.0 / HEAD_DIM)
    yc = y - mean
    var = _dot_f32_lhs(yc * yc, e) * (1.0 / HEAD_DIM)
    yn = yc * lax.rsqrt(var + RWKV_GN_EPS) * gw_ref[...] + gb_ref[...]
    o_ref[...] = (yn + bonus_ref[...]) * g_ref[...]


def _rwkv_post(y, g, bonus, gn_w, gn_b, e):
    m, c = y.shape
    tile = pl.BlockSpec((ROW_TILE, c), lambda i: (i, 0))
    row = pl.BlockSpec((1, c), lambda i: (0, 0))
    return pl.pallas_call(
        _rwkv_post_kernel,
        grid=(m // ROW_TILE,),
        in_specs=[tile, tile, tile, row, row, pl.BlockSpec((c, c), lambda i: (0, 0))],
        out_specs=tile,
        out_shape=jax.ShapeDtypeStruct((m, c), F32),
        compiler_params=_params("parallel"),
        name="rwkv_post",
    )(y, g, bonus, gn_w, gn_b, e)


def _moba_kernel(qt_ref, k_ref, vt_ref, km_ref, o_ref, acc_s, *, blk, n_sel, n_heads):
    i = pl.program_id(1)
    n_pad = km_ref.shape[1]
    blk_id = lax.broadcasted_iota(jnp.int32, (n_pad, blk), 0)
    fully_past = blk_id < i
    weight = jnp.where(blk_id < 24, jnp.left_shift(1, jnp.minimum(blk_id, 23)), 0).astype(F32)
    key = lax.broadcasted_iota(jnp.int32, (blk, blk), 0)
    qry = lax.broadcasted_iota(jnp.int32, (blk, blk), 1)
    causal = key <= qry
    qt_all = qt_ref[0]
    km_all = km_ref[0]
    chan_head = lax.broadcasted_iota(jnp.int32, qt_all.shape, 0) // HEAD_DIM
    lane_head = lax.broadcasted_iota(jnp.int32, km_all.shape, 1) // HEAD_DIM
    q_scaled = (qt_all * (HEAD_DIM ** -0.5)).astype(BF16)
    hs = [slice(h * HEAD_DIM, (h + 1) * HEAD_DIM) for h in range(n_heads)]

    qs, bits = [], []
    for h in range(n_heads):
        gate = jnp.dot(jnp.where(lane_head == h, km_all, 0.0), qt_all, preferred_element_type=F32,
                       precision=lax.Precision.HIGHEST)
        gate = jnp.where(fully_past, gate, MASK_VALUE)
        chosen = jnp.zeros(gate.shape, F32)
        for _ in range(n_sel):
            top = jnp.max(gate, axis=0, keepdims=True)
            idx = jnp.min(jnp.where(gate == top, blk_id, n_pad), axis=0, keepdims=True)
            pick = blk_id == idx
            chosen = jnp.where(pick & fully_past, 1.0, chosen)
            gate = jnp.where(pick, -jnp.inf, gate)
        bits.append(jnp.sum(chosen * weight, axis=0, keepdims=True).astype(jnp.int32))
        qs.append(jnp.where(chan_head == h, q_scaled, jnp.zeros_like(q_scaled)))

    def scores(n, h):
        start = pl.multiple_of(n * blk, blk)
        return _dot(k_ref[0, pl.ds(start, blk), :], qs[h]), vt_ref[0, hs[h], pl.ds(start, blk)]

    heads = range(n_heads)
    ms, ls, ps = [], [], []
    own = [scores(i, h) for h in heads]
    for h in heads:
        s = jnp.where(causal, own[h][0], MASK_VALUE)
        m = jnp.max(s, axis=0, keepdims=True)
        p = jnp.exp(s - m)
        ms.append(m)
        ls.append(jnp.sum(p, axis=0, keepdims=True))
        ps.append(p.astype(BF16))
    for h in heads:
        acc_s[hs[h], :] = _dot(own[h][1], ps[h])

    second = lax.broadcasted_iota(jnp.int32, (2 * blk, blk), 0) >= blk

    def pair(p, h):
        start = pl.multiple_of(p * (2 * blk), 2 * blk)
        return (_dot(k_ref[0, pl.ds(start, 2 * blk), :], qs[h]),
                vt_ref[0, hs[h], pl.ds(start, 2 * blk)])

    def body(step, carry):
        ms, ls = carry
        blocks = [pair(step, h) for h in heads]
        new_ms, new_ls, alphas, ps = [], [], [], []
        for h in heads:
            both = jnp.right_shift(bits[h], 2 * step)
            sel = jnp.where(second, jnp.right_shift(both, 1), both) & 1
            s = jnp.where(sel == 1, blocks[h][0], MASK_VALUE)
            m_new = jnp.maximum(ms[h], jnp.max(s, axis=0, keepdims=True))
            alpha = jnp.exp(ms[h] - m_new)
            p = jnp.exp(s - m_new)
            new_ms.append(m_new)
            new_ls.append(alpha * ls[h] + jnp.sum(p, axis=0, keepdims=True))
            alphas.append(alpha)
            ps.append(p.astype(BF16))
        for h in heads:
            acc_s[hs[h], :] = alphas[h] * acc_s[hs[h], :] + _dot(blocks[h][1], ps[h])
        return tuple(new_ms), tuple(new_ls)

    ms, ls = lax.fori_loop(0, (i + 1) // 2, body, (tuple(ms), tuple(ls)))
    for h in range(n_heads):
        acc_s[hs[h], :] = acc_s[hs[h], :] / ls[h]
    o_ref[0] = acc_s[...].T


def _moba_attention(qt, k, vt, km):
    b, s, c = k.shape
    blk = MOBA_BLOCK
    n_blk = s // blk
    n_pad = km.shape[1]
    assert s % blk == 0 and n_blk <= 24
    n_sel = min(MOBA_TOPK, n_blk)
    return pl.pallas_call(
        functools.partial(_moba_kernel, blk=blk, n_sel=n_sel, n_heads=N_HEADS),
        grid=(b, n_blk),
        in_specs=[
            pl.BlockSpec((1, c, blk), lambda bi, i: (bi, 0, i)),
            pl.BlockSpec((1, s, c), lambda bi, i: (bi, 0, 0)),
            pl.BlockSpec((1, c, s), lambda bi, i: (bi, 0, 0)),
            pl.BlockSpec((1, n_pad, c), lambda bi, i: (bi, 0, 0)),
        ],
        out_specs=pl.BlockSpec((1, blk, c), lambda bi, i: (bi, i, 0)),
        out_shape=jax.ShapeDtypeStruct((b, s, c), F32),
        scratch_shapes=[pltpu.VMEM((c, blk), F32)],
        compiler_params=_params("parallel", "parallel"),
        name="moba_attention",
    )(qt, k, vt, km)


def _hgrn_kernel(q_ref, f_ref, i_ref, g_ref, lb_ref, gn_ref, e_ref, tri_ref, ones_ref, hm_ref,
                 o_ref, st_ref, q_s, k_s, b_s, v_s, qd_s, ke_s, be_s, o_s, *, chunk):
    c = MIX_WIDTH
    rows = q_ref.shape[1]

    @pl.when(pl.program_id(1) == 0)
    def _():
        st_ref[...] = jnp.zeros_like(st_ref)

    lb = lb_ref[...]
    sig = jax.nn.sigmoid(f_ref[0])
    log_f = jnp.log(lb + (1.0 - lb) * sig)
    key = (1.0 - lb) * (1.0 - sig)
    b = _dot_f32_rhs(tri_ref[...], log_f)
    b_end = _dot_f32_rhs(ones_ref[...], log_f)
    q = q_ref[0]
    q_s[...] = q
    k_s[...] = key
    b_s[...] = b
    v_s[...] = i_ref[0]
    qd_s[...] = q * jnp.exp(b)
    ke_s[...] = key * jnp.exp(b_end - b)
    be_s[...] = b_end
    e = e_ref[...]
    head_mask = hm_ref[...]
    sub = lax.broadcasted_iota(jnp.int32, (chunk, c), 0)

    def body(n, carry):
        r0 = pl.multiple_of(n * chunk, chunk)
        sl = pl.ds(r0, chunk)
        qi = q_s[sl, :]
        ki = k_s[sl, :]
        bi = b_s[sl, :]
        vi = v_s[sl, :]
        parts = []
        for s in range(chunk):
            pr = qi * ki[s:s + 1, :] * jnp.exp(jnp.minimum(bi - bi[s:s + 1, :], 0.0))
            parts.append(jnp.where(sub >= s, pr, 0.0))
        sc = _dot_f32_lhs(jnp.concatenate(parts, axis=0), e)
        o = jnp.zeros((chunk, c), F32)
        for s in range(chunk):
            o = o + sc[s * chunk:(s + 1) * chunk, :] * vi[s:s + 1, :]
        st = st_ref[...]
        o = o + lax.dot_general(qd_s[sl, :].astype(BF16), st.astype(BF16),
                                (((1,), (1,)), ((), ())), preferred_element_type=F32)
        o_s[sl, :] = o
        kv = lax.dot_general(vi.astype(BF16), ke_s[sl, :].astype(BF16),
                             (((0,), (0,)), ((), ())), preferred_element_type=F32)
        st_ref[...] = st * jnp.exp(be_s[pl.ds(r0, 1), :]) + kv * head_mask
        return carry

    lax.fori_loop(0, rows // chunk, body, 0, unroll=4)
    o = o_s[...]
    ms = _dot_f32_lhs(o * o, e) * (1.0 / HEAD_DIM)
    g = g_ref[0]
    o_ref[0] = o * lax.rsqrt(ms + NORM_EPS) * gn_ref[...] * (g * jax.nn.sigmoid(g))


def _hgrn(hc, lb, gn_g, e):
    b, s, _ = hc.shape
    c = MIX_WIDTH
    t = SEQ_TILE
    ch = HGRN_CHUNK
    ridx = jnp.arange(t)
    same = (ridx[:, None] // ch) == (ridx[None, :] // ch)
    tri = (same & (ridx[None, :] <= ridx[:, None])).astype(BF16)
    ones = same.astype(BF16)
    seq = lambda n: pl.BlockSpec((1, t, c), lambda bi, j: (bi, j, n))
    row = pl.BlockSpec((1, c), lambda bi, j: (0, 0))
    mat = lambda n: pl.BlockSpec((n, n), lambda bi, j: (0, 0))
    return pl.pallas_call(
        functools.partial(_hgrn_kernel, chunk=ch),
        grid=(b, s // t),
        in_specs=[seq(0), seq(1), seq(2), seq(3), row, row, mat(c), mat(t), mat(t), mat(c)],
        out_specs=pl.BlockSpec((1, t, c), lambda bi, j: (bi, j, 0)),
        out_shape=jax.ShapeDtypeStruct((b, s, c), F32),
        scratch_shapes=[pltpu.VMEM((c, c), F32)] + [pltpu.VMEM((t, c), F32)] * 8,
        compiler_params=_params("parallel", "arbitrary"),
        name="hgrn2",
    )(hc, hc, hc, hc, lb.reshape(1, c), gn_g.reshape(1, c), e, tri, ones, e.astype(F32))


def _merge_kernel(x_ref, g_ref, wg_ref, y0_ref, y1_ref, y2_ref, y3_ref, wb_ref, wo_ref, o_ref):
    x = x_ref[...]
    d = x.shape[1]
    h = _rms(x, g_ref[...]).astype(BF16)
    merged = None
    for n, y_ref in enumerate((y0_ref, y1_ref, y2_ref, y3_ref)):
        gate = jax.nn.sigmoid(_dot(h, wg_ref[:, n * d:(n + 1) * d]))
        term = gate * _dot(y_ref[...].astype(BF16), wb_ref[n])
        merged = term if merged is None else merged + term
    o_ref[...] = x + _dot(merged.astype(BF16), wo_ref[...])


def _merge(x2, g, w_gate, ys, w_branch, w_out):
    m, d = x2.shape
    c = MIX_WIDTH
    tile = pl.BlockSpec((ROW_TILE, d), lambda i: (i, 0))
    ytile = pl.BlockSpec((ROW_TILE, c), lambda i: (i, 0))
    return pl.pallas_call(
        _merge_kernel,
        grid=(m // ROW_TILE,),
        in_specs=[tile, pl.BlockSpec((1, d), lambda i: (0, 0)),
                  pl.BlockSpec((d, N_BRANCH * d), lambda i: (0, 0)),
                  ytile, ytile, ytile, ytile,
                  pl.BlockSpec((N_BRANCH, c, d), lambda i: (0, 0, 0)),
                  pl.BlockSpec((d, d), lambda i: (0, 0))],
        out_specs=tile,
        out_shape=jax.ShapeDtypeStruct((m, d), F32),
        compiler_params=_params("parallel"),
        name="gated_merge",
    )(x2, g.reshape(1, d), w_gate, *ys, w_branch, w_out)


def _ffn_kernel(*refs, final):
    if final:
        x_ref, g_ref, wg_ref, wu_ref, wd_ref, fg_ref, o_ref = refs
    else:
        x_ref, g_ref, wg_ref, wu_ref, wd_ref, o_ref = refs
    x = x_ref[...]
    h = _rms(x, g_ref[...]).astype(BF16)
    gate = _dot(h, wg_ref[...])
    act = gate * jax.nn.sigmoid(gate) * _dot(h, wu_ref[...])
    y = x + _dot(act.astype(BF16), wd_ref[...])
    o_ref[...] = _rms(y, fg_ref[...]) if final else y


def _ffn(x2, g, w_gate, w_up, w_down, final_g):
    m, d = x2.shape
    dff = w_gate.shape[1]
    final = final_g is not None
    tile = pl.BlockSpec((ROW_TILE, d), lambda i: (i, 0))
    row = pl.BlockSpec((1, d), lambda i: (0, 0))
    resident = lambda r, c: pl.BlockSpec((r, c), lambda i: (0, 0), pipeline_mode=pl.Buffered(1))
    in_specs = [tile, row, resident(d, dff), resident(d, dff), resident(dff, d)]
    args = [x2, g.reshape(1, d), w_gate, w_up, w_down]
    if final:
        in_specs.append(row)
        args.append(final_g.reshape(1, d))
    return pl.pallas_call(
        functools.partial(_ffn_kernel, final=final),
        grid=(m // ROW_TILE,),
        in_specs=in_specs,
        out_specs=tile,
        out_shape=jax.ShapeDtypeStruct((m, d), F32),
        compiler_params=_params("parallel"),
        name="ffn",
    )(*args)


def _rope_tables(s):
    half = ROPE_DIM // 2
    inv_freq = ROPE_THETA ** (-jnp.arange(0, ROPE_DIM, 2, dtype=F32) / ROPE_DIM)
    ang = jnp.arange(s, dtype=F32)[:, None] * inv_freq[None, :]
    cos, sin = jnp.cos(ang), jnp.sin(ang)
    pad = HEAD_DIM - ROPE_DIM
    cos_h = jnp.concatenate([cos, cos, jnp.ones((s, pad), F32)], axis=1)
    sin_h = jnp.concatenate([-sin, sin, jnp.zeros((s, pad), F32)], axis=1)
    return jnp.tile(cos_h, (1, N_HEADS)), jnp.tile(sin_h, (1, N_HEADS))


def kernel(x, norm1_g, w_in, rwkv_mu, rwkv_w0, rwkv_w2, rwkv_a0, rwkv_a2, rwkv_g2, rwkv_k_k, rwkv_k_a, rwkv_r_k, rwkv_gn_w, rwkv_gn_b, rwkv_v0, rwkv_v1, rwkv_v2, hgrn_lb_logits, hgrn_gn_g, w_branch, w_out, norm2_g, w_ffn_gate, w_ffn_up, w_ffn_down, final_g):
    b, s, d = x.shape
    depth = w_in.shape[0]
    c = MIX_WIDTH
    m = b * s
    assert d == D_MODEL and s % SEQ_TILE == 0 and m % ROW_TILE == 0 and s % REC_TILE == 0
    assert LANES % (b * N_HEADS) == 0 and HEAD_DIM % (LANES // (b * N_HEADS)) == 0

    cos, sin = _rope_tables(s)
    lb_w = jax.nn.softmax(hgrn_lb_logits.astype(F32), axis=0)
    lower_bounds = jnp.cumsum(lb_w, axis=0) - lb_w[0]
    head_id = jnp.arange(c) // HEAD_DIM
    e = (head_id[:, None] == head_id[None, :]).astype(BF16)
    row = lambda t: t.reshape(1, -1)
    zpad = lambda t, top, bottom: jnp.pad(t, ((top, bottom), (0, 0)))

    x2 = x.reshape(m, d)
    v_first = None
    for layer in range(depth):
        w_l = w_in[layer].astype(BF16)
        g1 = norm1_g[layer]
        rp, hc, k_sb, qt_sb, vt_sb, k_m, qt_m, vt_m, k_mean = _in_proj(
            x2.reshape(b, s, d), g1, w_l, cos, sin)

        y_sb = _sb_attention(qt_sb, k_sb, vt_sb).reshape(m, c)

        prm = {
            "mu": row(rwkv_mu[layer]), "w0": row(rwkv_w0[layer]),
            "w2": zpad(rwkv_w2[layer], 0, LANES - RWKV_DECAY_LORA),
            "a0": row(rwkv_a0[layer]),
            "a2": zpad(rwkv_a2[layer], RWKV_DECAY_LORA, LANES - RWKV_DECAY_LORA - RWKV_AAA_LORA),
            "g2": rwkv_g2[layer], "k_k": row(rwkv_k_k[layer]), "k_a": row(rwkv_k_a[layer]),
            "r_k": row(rwkv_r_k[layer]),
        }
        if layer > 0:
            prm["v0"] = row(rwkv_v0[layer - 1])
            prm["v1"] = jnp.pad(rwkv_v1[layer - 1], ((0, 0), (0, LANES - RWKV_MV_LORA)))
            prm["v2"] = zpad(rwkv_v2[layer - 1], 0, LANES - RWKV_MV_LORA)
        r_, w_, k_, v_, an_, bb_, g_, bonus_ = _rwkv_prep(rp, prm, e, v_first)
        if layer == 0:
            v_first = v_
        y_rec = _rwkv_recurrence(r_, w_, k_, v_, an_, bb_)
        y_rwkv = _rwkv_post(y_rec.reshape(m, c), g_.reshape(m, c), bonus_.reshape(m, c),
                            row(rwkv_gn_w[layer]), row(rwkv_gn_b[layer]), e)

        n_blk = k_mean.shape[1]
        k_mean = jnp.pad(k_mean.reshape(b, n_blk, c), ((0, 0), (0, -n_blk % 8), (0, 0)))
        y_moba = _moba_attention(qt_m, k_m, vt_m, k_mean).reshape(m, c)

        y_hgrn = _hgrn(hc, lower_bounds[layer], hgrn_gn_g[layer], e).reshape(m, c)

        x2 = _merge(x2, g1, w_l[:, GATE_OFF:], (y_sb, y_rwkv, y_moba, y_hgrn),
                    w_branch[layer].astype(BF16), w_out[layer].astype(BF16))
        x2 = _ffn(x2, norm2_g[layer], w_ffn_gate[layer].astype(BF16), w_ffn_up[layer].astype(BF16),
                  w_ffn_down[layer].astype(BF16), final_g if layer == depth - 1 else None)
    return x2.reshape(b, s, d)
```

```python
import functools

import jax
import jax.numpy as jnp
from jax import lax
from jax.experimental import pallas as pl
from jax.experimental.pallas import tpu as pltpu

F32 = jnp.float32
BF16 = jnp.bfloat16

D_MODEL = 1024
HEAD_DIM = 64
N_HEADS = 4
MIX_WIDTH = N_HEADS * HEAD_DIM
N_BRANCH = 4
NORM_EPS = 1e-6
MASK_VALUE = -1e30

RWKV_DECAY_LORA = 64
RWKV_AAA_LORA = 64
RWKV_MV_LORA = 32
RWKV_GATE_LORA = 128
RWKV_GN_EPS = 64e-5
RWKV_COLS = 3 * MIX_WIDTH + RWKV_DECAY_LORA + RWKV_AAA_LORA + RWKV_GATE_LORA

MOBA_BLOCK = 256
MOBA_TOPK = 3
ROPE_THETA = 500000.0
ROPE_DIM = HEAD_DIM // 4

HGRN_CHUNK = 16

SB_COLS = 3 * MIX_WIDTH
MOBA_COLS = 3 * MIX_WIDTH
HGRN_COLS = 4 * MIX_WIDTH
SB_OFF = 0
RWKV_OFF = SB_OFF + SB_COLS
MOBA_OFF = RWKV_OFF + RWKV_COLS
HGRN_OFF = MOBA_OFF + MOBA_COLS
GATE_OFF = HGRN_OFF + HGRN_COLS

LANES = 128
VMEM_LIMIT = 56 * 1024 * 1024

ROW_TILE = 512
SEQ_TILE = 256
REC_TILE = 128


def _params(*sem):
    return pltpu.CompilerParams(dimension_semantics=sem, vmem_limit_bytes=VMEM_LIMIT)


def _dot(a, b):
    return jnp.dot(a, b, preferred_element_type=F32)


def _split2(a):
    hi = a.astype(BF16)
    lo = (a - hi.astype(F32)).astype(BF16)
    return hi, lo


def _split3(a):
    hi = a.astype(BF16)
    r1 = a - hi.astype(F32)
    mid = r1.astype(BF16)
    lo = (r1 - mid.astype(F32)).astype(BF16)
    return hi, mid, lo


def _dot_f32_lhs(a, m):
    hi, lo = _split2(a)
    return _dot(hi, m) + _dot(lo, m)


def _dot_f32_rhs(m, a):
    hi, mid, lo = _split3(a)
    return _dot(m, hi) + _dot(m, mid) + _dot(m, lo)


def _dot3(a, b):
    ah, al = _split2(a)
    bh, bl = _split2(b)
    return _dot(ah, bh) + _dot(ah, bl) + _dot(al, bh)


def _rms(x, g):
    return x * lax.rsqrt(jnp.mean(x * x, axis=-1, keepdims=True) + NORM_EPS) * g


_NT = (((1,), (1,)), ((), ()))


def _in_proj_kernel(x_ref, g_ref, wr_ref, wh_ref, sk_ref, sqt_ref, svt_ref, mk_ref, mqt_ref, mvt_ref,
                    cos_ref, sin_ref, cost_ref, sint_ref,
                    rw_o, hg_o, sk_o, sqt_o, svt_o, mk_o, mqt_o, mvt_o, km_o):
    h = _rms(x_ref[0], g_ref[...]).astype(BF16)
    nt = lambda w_ref: lax.dot_general(w_ref[...], h, _NT, preferred_element_type=F32)
    rw_o[0] = _dot(h, wr_ref[...])
    hg_o[0] = _dot(h, wh_ref[...])
    sk_o[0] = _dot(h, sk_ref[...]).astype(BF16)
    sqt_o[0] = nt(sqt_ref).astype(BF16)
    svt_o[0] = nt(svt_ref).astype(BF16)
    mvt_o[0] = nt(mvt_ref).astype(BF16)
    k = _dot(h, mk_ref[...])
    qt = nt(mqt_ref)
    c = k.shape[1]
    half = ROPE_DIM // 2
    lane = lax.broadcasted_iota(jnp.int32, k.shape, 1) % HEAD_DIM
    partner = jnp.where(lane < half, pltpu.roll(k, c - half, axis=1), pltpu.roll(k, half, axis=1))
    k = k * cos_ref[...] + partner * sin_ref[...]
    chan = lax.broadcasted_iota(jnp.int32, qt.shape, 0) % HEAD_DIM
    partner = jnp.where(chan < half, pltpu.roll(qt, c - half, axis=0), pltpu.roll(qt, half, axis=0))
    mqt_o[0] = qt * cost_ref[...] + partner * sint_ref[...]
    mk_o[0] = k.astype(BF16)
    for i in range(k.shape[0] // MOBA_BLOCK):
        km_o[0, i] = jnp.mean(k[i * MOBA_BLOCK:(i + 1) * MOBA_BLOCK], axis=0, keepdims=True)


def _in_proj(x3, g, w_l, cos, sin):
    b, s, d = x3.shape
    c = MIX_WIDTH
    t = ROW_TILE
    assert t % MOBA_BLOCK == 0
    resident = lambda m, n: pl.BlockSpec((m, n), lambda bi, j: (0, 0), pipeline_mode=pl.Buffered(1))
    rows = lambda n: pl.BlockSpec((1, t, n), lambda bi, j: (bi, j, 0))
    cols = pl.BlockSpec((1, c, t), lambda bi, j: (bi, 0, j))
    sb, mo = w_l[:, SB_OFF:SB_OFF + SB_COLS], w_l[:, MOBA_OFF:MOBA_OFF + MOBA_COLS]
    args = [x3, g.reshape(1, d), w_l[:, RWKV_OFF:RWKV_OFF + RWKV_COLS], w_l[:, HGRN_OFF:HGRN_OFF + HGRN_COLS],
            sb[:, c:2 * c], (sb[:, :c] * (HEAD_DIM ** -0.5)).T, sb[:, 2 * c:].T,
            mo[:, c:2 * c], mo[:, :c].T, mo[:, 2 * c:].T, cos, sin, cos.T, sin.T]
    in_specs = [rows(d), resident(1, d), resident(d, RWKV_COLS), resident(d, HGRN_COLS),
                resident(d, c), resident(c, d), resident(c, d),
                resident(d, c), resident(c, d), resident(c, d),
                pl.BlockSpec((t, c), lambda bi, j: (j, 0)), pl.BlockSpec((t, c), lambda bi, j: (j, 0)),
                pl.BlockSpec((c, t), lambda bi, j: (0, j)), pl.BlockSpec((c, t), lambda bi, j: (0, j))]
    n_mean = t // MOBA_BLOCK
    out_specs = [rows(RWKV_COLS), rows(HGRN_COLS), rows(c), cols, cols, rows(c), cols, cols,
                 pl.BlockSpec((1, n_mean, 1, c), lambda bi, j: (bi, j, 0, 0))]
    sds = jax.ShapeDtypeStruct
    out_shape = [sds((b, s, RWKV_COLS), F32), sds((b, s, HGRN_COLS), F32),
                 sds((b, s, c), BF16), sds((b, c, s), BF16), sds((b, c, s), BF16),
                 sds((b, s, c), BF16), sds((b, c, s), F32), sds((b, c, s), BF16),
                 sds((b, s // MOBA_BLOCK, 1, c), F32)]
    return pl.pallas_call(
        _in_proj_kernel,
        grid=(b, s // t),
        in_specs=in_specs,
        out_specs=out_specs,
        out_shape=out_shape,
        compiler_params=_params("parallel", "parallel"),
        name="in_proj",
    )(*args)


SB_DEAD_LOG = -110.0


def _sb_kernel(qt_ref, k_ref, vt_ref, ut_ref, o_ref, acc_s, *, tq, n_heads):
    i = pl.program_id(1)
    ut = ut_ref[...]
    key = lax.broadcasted_iota(jnp.int32, (tq, tq), 0)
    qry = lax.broadcasted_iota(jnp.int32, (tq, tq), 1)
    past = key < qry
    qt = qt_ref[0]
    chan_head = lax.broadcasted_iota(jnp.int32, qt.shape, 0) // HEAD_DIM
    qh = [jnp.where(chan_head == h, qt, jnp.zeros_like(qt)) for h in range(n_heads)]
    hs = [slice(h * HEAD_DIM, (h + 1) * HEAD_DIM) for h in range(n_heads)]

    def block(j, runs, diagonal):
        start = pl.multiple_of(j * tq, tq)
        heads = range(n_heads)
        kb = k_ref[0, pl.ds(start, tq), :]
        zs = [_dot(kb, qh[h]) for h in heads]
        lks, lbs, pieces = [], [], []
        for h in heads:
            z = zs[h]
            sp = jnp.maximum(z, 0.0) + jnp.log(1.0 + jnp.exp(-jnp.abs(z)))
            lk = -sp
            if diagonal:
                lk = jnp.where(past, lk, 0.0)
            lks.append(lk)
            lbs.append(z - sp)
            pieces.append(_split2(lk))
        betweens = [_dot(ut, hi) + _dot(ut, lo) for hi, lo in pieces]
        atts = []
        for h in heads:
            att = jnp.exp(lbs[h] + betweens[h] + runs[h])
            if diagonal:
                att = jnp.where(past, att, 0.0)
            atts.append(att.astype(BF16))
        for h in heads:
            part = _dot(vt_ref[0, hs[h], pl.ds(start, tq)], atts[h])
            acc_s[hs[h], :] = part if diagonal else acc_s[hs[h], :] + part
        return tuple(runs[h] + jnp.sum(lks[h], axis=0, keepdims=True) for h in heads)

    def live(runs):
        top = functools.reduce(jnp.maximum, [jnp.max(r) for r in runs])
        return (top > SB_DEAD_LOG).astype(jnp.int32)

    runs = block(i, tuple(jnp.zeros((1, tq), F32) for _ in range(n_heads)), True)

    def cond(c):
        jj, go, _ = c
        return jnp.logical_and(jj < i, go > 0)

    def body(c):
        jj, _, runs = c
        runs = block(i - 1 - jj, runs, False)
        return jj + 1, live(runs), runs

    lax.while_loop(cond, body, (jnp.int32(0), live(runs), runs))
    o_ref[0] = acc_s[...].T


def _sb_attention(qt, k, vt):
    b, s, c = k.shape
    tq = SEQ_TILE
    ut = (jnp.arange(tq)[None, :] > jnp.arange(tq)[:, None]).astype(BF16)
    return pl.pallas_call(
        functools.partial(_sb_kernel, tq=tq, n_heads=N_HEADS),
        grid=(b, s // tq),
        in_specs=[
            pl.BlockSpec((1, c, tq), lambda bi, i: (bi, 0, i)),
            pl.BlockSpec((1, s, c), lambda bi, i: (bi, 0, 0)),
            pl.BlockSpec((1, c, s), lambda bi, i: (bi, 0, 0)),
            pl.BlockSpec((tq, tq), lambda bi, i: (0, 0)),
        ],
        out_specs=pl.BlockSpec((1, tq, c), lambda bi, i: (bi, i, 0)),
        out_shape=jax.ShapeDtypeStruct((b, s, c), F32),
        scratch_shapes=[pltpu.VMEM((c, tq), F32)],
        compiler_params=_params("parallel", "parallel"),
        name="sb_attention",
    )(qt, k, vt, ut)


def _rwkv_prep_kernel(*refs, has_vres):
    if has_vres:
        (p_ref, mu_ref, w0_ref, w2_ref, a0_ref, a2_ref, g2_ref, kk_ref, ka_ref, rk_ref, e_ref,
         tri_ref, ones_ref, vf_ref, v0_ref, v1_ref, v2_ref,
         r_o, w_o, k_o, v_o, an_o, bb_o, g_o, bonus_o, carry) = refs
    else:
        (p_ref, mu_ref, w0_ref, w2_ref, a0_ref, a2_ref, g2_ref, kk_ref, ka_ref, rk_ref, e_ref,
         tri_ref, ones_ref,
         r_o, w_o, k_o, v_o, an_o, bb_o, g_o, bonus_o, carry) = refs
    c = MIX_WIDTH
    p = p_ref[0]
    t = p.shape[0]

    @pl.when(pl.program_id(1) == 0)
    def _():
        carry[...] = jnp.zeros_like(carry)

    rowid = lax.broadcasted_iota(jnp.int32, p.shape, 0)
    prev = jnp.where(rowid == 0, carry[...], pltpu.roll(p, 1, axis=0))
    carry[...] = p[t - 1:t, :]
    xs = p + (prev - p) * mu_ref[...]

    r = xs[:, :c]
    k = xs[:, c:2 * c]
    v = xs[:, 2 * c:3 * c]
    lora = xs[:, 3 * c:3 * c + LANES]
    gl = xs[:, 3 * c + LANES:]
    w_log = -jax.nn.softplus(-(w0_ref[...] + _dot3(jnp.tanh(lora), w2_ref[...]))) - 0.5
    log_decay = -jnp.exp(w_log)
    cum = _dot_f32_rhs(tri_ref[...], log_decay)
    cum_end = _dot_f32_rhs(ones_ref[...], log_decay)
    if has_vres:
        mix = jax.nn.sigmoid(v0_ref[...] + _dot3(_dot3(v, v1_ref[...]), v2_ref[...]))
        v = v + (vf_ref[0] - v) * mix
    a = jax.nn.sigmoid(a0_ref[...] + _dot3(lora, a2_ref[...]))
    g = _dot3(jax.nn.sigmoid(gl), g2_ref[...])
    e = e_ref[...]
    kk = k * kk_ref[...]
    kk = kk / jnp.maximum(jnp.sqrt(_dot_f32_lhs(kk * kk, e)), 1e-12)
    k = k * (1.0 + (a - 1.0) * ka_ref[...])
    bonus = _dot_f32_lhs(r * k * rk_ref[...], e) * v
    inv_gamma = jnp.exp(-cum)
    r_o[0] = r * jnp.exp(cum)
    w_o[0] = jnp.exp(cum_end)
    k_o[0] = k * inv_gamma
    v_o[0] = v
    an_o[0] = -kk * jnp.exp(cum - log_decay)
    bb_o[0] = kk * a * inv_gamma
    g_o[0] = g
    bonus_o[0] = bonus


def _rwkv_prep(p, prm, e, v_first):
    b, s, cols = p.shape
    c = MIX_WIDTH
    t = SEQ_TILE
    group = LANES // (b * N_HEADS)
    has_vres = v_first is not None
    ridx = jnp.arange(t)
    same = (ridx[:, None] // group) == (ridx[None, :] // group)
    tri = (same & (ridx[None, :] <= ridx[:, None])).astype(BF16)
    row = lambda n: pl.BlockSpec((1, n), lambda bi, j: (0, 0))
    mat = lambda m, n: pl.BlockSpec((m, n), lambda bi, j: (0, 0))
    seq = lambda n: pl.BlockSpec((1, t, n), lambda bi, j: (bi, j, 0))
    in_specs = [seq(cols), row(cols), row(c), mat(LANES, c), row(c), mat(LANES, c),
                mat(LANES, c), row(c), row(c), row(c), mat(c, c), mat(t, t), mat(t, t)]
    args = [p, prm["mu"], prm["w0"], prm["w2"], prm["a0"], prm["a2"], prm["g2"],
            prm["k_k"], prm["k_a"], prm["r_k"], e, tri, same.astype(BF16)]
    if has_vres:
        in_specs += [seq(c), row(c), mat(c, LANES), mat(LANES, c)]
        args += [v_first, prm["v0"], prm["v1"], prm["v2"]]
    out = jax.ShapeDtypeStruct((b, s, c), F32)
    return pl.pallas_call(
        functools.partial(_rwkv_prep_kernel, has_vres=has_vres),
        grid=(b, s // t),
        in_specs=in_specs,
        out_specs=[seq(c)] * 8,
        out_shape=[out] * 8,
        scratch_shapes=[pltpu.VMEM((1, cols), F32)],
        compiler_params=_params("parallel", "arbitrary"),
        name="rwkv_prep",
    )(*args)


def _rwkv_rec_kernel(an_ref, w_ref, bb_ref, k_ref, r_ref, v_ref, y_ref, st_ref, ex_ref, *, n_tiles, group):
    @pl.when(pl.program_id(0) == 0)
    def _():
        st_ref[...] = jnp.zeros_like(st_ref)

    nb = an_ref.shape[0]
    per_head = LANES // N_HEADS
    lane = lax.broadcasted_iota(jnp.int32, (HEAD_DIM, LANES), 1)
    lane_b = lane // (N_HEADS * group)
    lane_h = (lane // group) % N_HEADS

    def packed(ref, t0):
        w = jnp.concatenate([ref[:, t0 + u, :] for u in range(group)], axis=0)
        pieces = []
        for h in range(N_HEADS):
            col, off = divmod(h * HEAD_DIM, LANES)
            piece = w[:, col * LANES:(col + 1) * LANES]
            pieces.append(pltpu.roll(piece, LANES - off, axis=1) if off else piece)
        return jnp.concatenate(pieces, axis=0).T[:HEAD_DIM, :]

    operands = (an_ref, w_ref, bb_ref, k_ref, r_ref)
    n_groups = an_ref.shape[1] // group

    def body(tp, packs):
        t0 = tp * group
        t_next = jnp.minimum(tp + 1, n_groups - 1) * group
        next_packs = tuple(packed(ref, t_next) for ref in operands)
        for u in range(group):
            idx = lane_h * per_head + u * nb + lane_b
            for o, p in enumerate(packs):
                if o != 1 or u == 0:
                    ex_ref[u, o] = jnp.take_along_axis(p, idx, axis=1)
        vts = [v_ref[t0 + u] for u in range(group)]
        ys = [[] for _ in range(group)]
        for j in range(n_tiles):
            st = st_ref[j]
            for u in range(group):
                sa = jnp.sum(st * ex_ref[u, 0], axis=0, keepdims=True)
                st = st + sa * ex_ref[u, 2] + vts[u][j:j + 1, :] * ex_ref[u, 3]
                ys[u].append(jnp.sum(st * ex_ref[u, 4], axis=0, keepdims=True))
            st_ref[j] = st * ex_ref[0, 1]
        for u in range(group):
            y_ref[t0 + u] = jnp.concatenate(ys[u], axis=0)
        return next_packs

    lax.fori_loop(0, n_groups, body, tuple(packed(ref, 0) for ref in operands))


def _rwkv_recurrence(r, w, k, v, an, bb):
    b, s, c = r.shape
    group = LANES // (b * N_HEADS)
    n_tiles = HEAD_DIM // group
    ve = v.reshape(b, s, N_HEADS, n_tiles, group).transpose(1, 3, 0, 2, 4).reshape(s, n_tiles, LANES)
    big = pl.BlockSpec((b, REC_TILE, c), lambda i: (0, i, 0))
    small = pl.BlockSpec((REC_TILE, n_tiles, LANES), lambda i: (i, 0, 0))
    y = pl.pallas_call(
        functools.partial(_rwkv_rec_kernel, n_tiles=n_tiles, group=group),
        grid=(s // REC_TILE,),
        in_specs=[big] * 5 + [small],
        out_specs=small,
        out_shape=jax.ShapeDtypeStruct((s, n_tiles, LANES), F32),
        scratch_shapes=[pltpu.VMEM((n_tiles, HEAD_DIM, LANES), F32),
                        pltpu.VMEM((group, 5, HEAD_DIM, LANES), F32)],
        compiler_params=_params("arbitrary"),
        name="rwkv_recurrence",
    )(an, w, bb, k, r, ve)
    y = y.reshape(s, n_tiles, b, N_HEADS, group).transpose(2, 0, 3, 1, 4)
    return y.reshape(b, s, c)


def _moba_kernel(qt_ref, k_ref, vt_ref, km_ref, o_ref, acc_s, *, blk, n_sel, n_heads):
    i = pl.program_id(1)
    n_pad = km_ref.shape[1]
    blk_id = lax.broadcasted_iota(jnp.int32, (n_pad, blk), 0)
    fully_past = blk_id < i
    weight = jnp.where(blk_id < 24, jnp.left_shift(1, jnp.minimum(blk_id, 23)), 0).astype(F32)
    key = lax.broadcasted_iota(jnp.int32, (blk, blk), 0)
    qry = lax.broadcasted_iota(jnp.int32, (blk, blk), 1)
    causal = key <= qry
    qt_all = qt_ref[0]
    km_all = km_ref[0]
    chan_head = lax.broadcasted_iota(jnp.int32, qt_all.shape, 0) // HEAD_DIM
    lane_head = lax.broadcasted_iota(jnp.int32, km_all.shape, 1) // HEAD_DIM
    q_scaled = (qt_all * (HEAD_DIM ** -0.5)).astype(BF16)
    hs = [slice(h * HEAD_DIM, (h + 1) * HEAD_DIM) for h in range(n_heads)]

    qs, bits = [], []
    for h in range(n_heads):
        gate = jnp.dot(jnp.where(lane_head == h, km_all, 0.0), qt_all, preferred_element_type=F32,
                       precision=lax.Precision.HIGHEST)
        gate = jnp.where(fully_past, gate, MASK_VALUE)
        chosen = jnp.zeros(gate.shape, F32)
        for _ in range(n_sel):
            top = jnp.max(gate, axis=0, keepdims=True)
            idx = jnp.min(jnp.where(gate == top, blk_id, n_pad), axis=0, keepdims=True)
            pick = blk_id == idx
            chosen = jnp.where(pick & fully_past, 1.0, chosen)
            gate = jnp.where(pick, -jnp.inf, gate)
        bits.append(jnp.sum(chosen * weight, axis=0, keepdims=True).astype(jnp.int32))
        qs.append(jnp.where(chan_head == h, q_scaled, jnp.zeros_like(q_scaled)))

    def scores(n, h):
        start = pl.multiple_of(n * blk, blk)
        return _dot(k_ref[0, pl.ds(start, blk), :], qs[h]), vt_ref[0, hs[h], pl.ds(start, blk)]

    heads = range(n_heads)
    ms, ls, ps = [], [], []
    own = [scores(i, h) for h in heads]
    for h in heads:
        s = jnp.where(causal, own[h][0], MASK_VALUE)
        m = jnp.max(s, axis=0, keepdims=True)
        p = jnp.exp(s - m)
        ms.append(m)
        ls.append(jnp.sum(p, axis=0, keepdims=True))
        ps.append(p.astype(BF16))
    for h in heads:
        acc_s[hs[h], :] = _dot(own[h][1], ps[h])

    second = lax.broadcasted_iota(jnp.int32, (2 * blk, blk), 0) >= blk

    def pair(p, h):
        start = pl.multiple_of(p * (2 * blk), 2 * blk)
        return (_dot(k_ref[0, pl.ds(start, 2 * blk), :], qs[h]),
                vt_ref[0, hs[h], pl.ds(start, 2 * blk)])

    def body(step, carry):
        ms, ls = carry
        blocks = [pair(step, h) for h in heads]
        new_ms, new_ls, alphas, ps = [], [], [], []
        for h in heads:
            both = jnp.right_shift(bits[h], 2 * step)
            sel = jnp.where(second, jnp.right_shift(both, 1), both) & 1
            s = jnp.where(sel == 1, blocks[h][0], MASK_VALUE)
            m_new = jnp.maximum(ms[h], jnp.max(s, axis=0, keepdims=True))
            alpha = jnp.exp(ms[h] - m_new)
            p = jnp.exp(s - m_new)
            new_ms.append(m_new)
            new_ls.append(alpha * ls[h] + jnp.sum(p, axis=0, keepdims=True))
            alphas.append(alpha)
            ps.append(p.astype(BF16))
        for h in heads:
            acc_s[hs[h], :] = alphas[h] * acc_s[hs[h], :] + _dot(blocks[h][1], ps[h])
        return tuple(new_ms), tuple(new_ls)

    ms, ls = lax.fori_loop(0, (i + 1) // 2, body, (tuple(ms), tuple(ls)))
    for h in range(n_heads):
        acc_s[hs[h], :] = acc_s[hs[h], :] / ls[h]
    o_ref[0] = acc_s[...].T


def _moba_attention(qt, k, vt, km):
    b, s, c = k.shape
    blk = MOBA_BLOCK
    n_blk = s // blk
    n_pad = km.shape[1]
    assert s % blk == 0 and n_blk <= 24
    n_sel = min(MOBA_TOPK, n_blk)
    return pl.pallas_call(
        functools.partial(_moba_kernel, blk=blk, n_sel=n_sel, n_heads=N_HEADS),
        grid=(b, n_blk),
        in_specs=[
            pl.BlockSpec((1, c, blk), lambda bi, i: (bi, 0, i)),
            pl.BlockSpec((1, s, c), lambda bi, i: (bi, 0, 0)),
            pl.BlockSpec((1, c, s), lambda bi, i: (bi, 0, 0)),
            pl.BlockSpec((1, n_pad, c), lambda bi, i: (bi, 0, 0)),
        ],
        out_specs=pl.BlockSpec((1, blk, c), lambda bi, i: (bi, i, 0)),
        out_shape=jax.ShapeDtypeStruct((b, s, c), F32),
        scratch_shapes=[pltpu.VMEM((c, blk), F32)],
        compiler_params=_params("parallel", "parallel"),
        name="moba_attention",
    )(qt, k, vt, km)


def _hgrn_kernel(q_ref, f_ref, i_ref, g_ref, lb_ref, gn_ref, e_ref, tri_ref, ones_ref, hm_ref,
                 o_ref, st_ref, q_s, k_s, b_s, v_s, qd_s, ke_s, be_s, o_s, *, chunk):
    c = MIX_WIDTH
    rows = q_ref.shape[1]

    @pl.when(pl.program_id(1) == 0)
    def _():
        st_ref[...] = jnp.zeros_like(st_ref)

    lb = lb_ref[...]
    sig = jax.nn.sigmoid(f_ref[0])
    log_f = jnp.log(lb + (1.0 - lb) * sig)
    key = (1.0 - lb) * (1.0 - sig)
    b = _dot_f32_rhs(tri_ref[...], log_f)
    b_end = _dot_f32_rhs(ones_ref[...], log_f)
    q = q_ref[0]
    q_s[...] = q
    k_s[...] = key
    b_s[...] = b
    v_s[...] = i_ref[0]
    qd_s[...] = q * jnp.exp(b)
    ke_s[...] = key * jnp.exp(b_end - b)
    be_s[...] = b_end
    e = e_ref[...]
    head_mask = hm_ref[...]
    sub = lax.broadcasted_iota(jnp.int32, (chunk, c), 0)

    def body(n, carry):
        r0 = pl.multiple_of(n * chunk, chunk)
        sl = pl.ds(r0, chunk)
        qi = q_s[sl, :]
        ki = k_s[sl, :]
        bi = b_s[sl, :]
        vi = v_s[sl, :]
        parts = []
        for s in range(chunk):
            pr = qi * ki[s:s + 1, :] * jnp.exp(jnp.minimum(bi - bi[s:s + 1, :], 0.0))
            parts.append(jnp.where(sub >= s, pr, 0.0))
        sc = _dot_f32_lhs(jnp.concatenate(parts, axis=0), e)
        o = jnp.zeros((chunk, c), F32)
        for s in range(chunk):
            o = o + sc[s * chunk:(s + 1) * chunk, :] * vi[s:s + 1, :]
        st = st_ref[...]
        o = o + lax.dot_general(qd_s[sl, :].astype(BF16), st.astype(BF16),
                                (((1,), (1,)), ((), ())), preferred_element_type=F32)
        o_s[sl, :] = o
        kv = lax.dot_general(vi.astype(BF16), ke_s[sl, :].astype(BF16),
                             (((0,), (0,)), ((), ())), preferred_element_type=F32)
        st_ref[...] = st * jnp.exp(be_s[pl.ds(r0, 1), :]) + kv * head_mask
        return carry

    lax.fori_loop(0, rows // chunk, body, 0, unroll=8)
    o = o_s[...]
    ms = _dot_f32_lhs(o * o, e) * (1.0 / HEAD_DIM)
    g = g_ref[0]
    o_ref[0] = o * lax.rsqrt(ms + NORM_EPS) * gn_ref[...] * (g * jax.nn.sigmoid(g))


def _hgrn(hc, lb, gn_g, e):
    b, s, _ = hc.shape
    c = MIX_WIDTH
    t = SEQ_TILE
    ch = HGRN_CHUNK
    ridx = jnp.arange(t)
    same = (ridx[:, None] // ch) == (ridx[None, :] // ch)
    tri = (same & (ridx[None, :] <= ridx[:, None])).astype(BF16)
    ones = same.astype(BF16)
    seq = lambda n: pl.BlockSpec((1, t, c), lambda bi, j: (bi, j, n))
    row = pl.BlockSpec((1, c), lambda bi, j: (0, 0))
    mat = lambda n: pl.BlockSpec((n, n), lambda bi, j: (0, 0))
    return pl.pallas_call(
        functools.partial(_hgrn_kernel, chunk=ch),
        grid=(b, s // t),
        in_specs=[seq(0), seq(1), seq(2), seq(3), row, row, mat(c), mat(t), mat(t), mat(c)],
        out_specs=pl.BlockSpec((1, t, c), lambda bi, j: (bi, j, 0)),
        out_shape=jax.ShapeDtypeStruct((b, s, c), F32),
        scratch_shapes=[pltpu.VMEM((c, c), F32)] + [pltpu.VMEM((t, c), F32)] * 8,
        compiler_params=_params("parallel", "arbitrary"),
        name="hgrn2",
    )(hc, hc, hc, hc, lb.reshape(1, c), gn_g.reshape(1, c), e, tri, ones, e.astype(F32))


def _merge_kernel(x_ref, g_ref, wg_ref, y0_ref, yr_ref, rg_ref, bonus_ref, gw_ref, gb_ref, e_ref,
                  y2_ref, y3_ref, wb_ref, wo_ref, o_ref):
    x = x_ref[...]
    d = x.shape[1]
    h = _rms(x, g_ref[...]).astype(BF16)
    y = yr_ref[...]
    e = e_ref[...]
    yc = y - _dot_f32_lhs(y, e) * (1.0 / HEAD_DIM)
    var = _dot_f32_lhs(yc * yc, e) * (1.0 / HEAD_DIM)
    y_rwkv = (yc * lax.rsqrt(var + RWKV_GN_EPS) * gw_ref[...] + gb_ref[...] + bonus_ref[...]) * rg_ref[...]
    merged = None
    for n, y_n in enumerate((y0_ref[...], y_rwkv, y2_ref[...], y3_ref[...])):
        gate = jax.nn.sigmoid(_dot(h, wg_ref[:, n * d:(n + 1) * d]))
        term = gate * _dot(y_n.astype(BF16), wb_ref[n])
        merged = term if merged is None else merged + term
    o_ref[...] = x + _dot(merged.astype(BF16), wo_ref[...])


def _merge(x2, g, w_gate, y_sb, rwkv, y_moba, y_hgrn, e, w_branch, w_out):
    m, d = x2.shape
    c = MIX_WIDTH
    tile = pl.BlockSpec((ROW_TILE, d), lambda i: (i, 0))
    ytile = pl.BlockSpec((ROW_TILE, c), lambda i: (i, 0))
    crow = pl.BlockSpec((1, c), lambda i: (0, 0))
    return pl.pallas_call(
        _merge_kernel,
        grid=(m // ROW_TILE,),
        in_specs=[tile, pl.BlockSpec((1, d), lambda i: (0, 0)),
                  pl.BlockSpec((d, N_BRANCH * d), lambda i: (0, 0)),
                  ytile, ytile, ytile, ytile, crow, crow, pl.BlockSpec((c, c), lambda i: (0, 0)),
                  ytile, ytile,
                  pl.BlockSpec((N_BRANCH, c, d), lambda i: (0, 0, 0)),
                  pl.BlockSpec((d, d), lambda i: (0, 0))],
        out_specs=tile,
        out_shape=jax.ShapeDtypeStruct((m, d), F32),
        compiler_params=_params("parallel"),
        name="gated_merge",
    )(x2, g.reshape(1, d), w_gate, y_sb, *rwkv, e, y_moba, y_hgrn, w_branch, w_out)


def _ffn_kernel(*refs, final):
    if final:
        x_ref, g_ref, wg_ref, wu_ref, wd_ref, fg_ref, o_ref = refs
    else:
        x_ref, g_ref, wg_ref, wu_ref, wd_ref, o_ref = refs
    x = x_ref[...]
    h = _rms(x, g_ref[...]).astype(BF16)
    gate = _dot(h, wg_ref[...])
    act = gate * jax.nn.sigmoid(gate) * _dot(h, wu_ref[...])
    y = x + _dot(act.astype(BF16), wd_ref[...])
    o_ref[...] = _rms(y, fg_ref[...]) if final else y


def _ffn(x2, g, w_gate, w_up, w_down, final_g):
    m, d = x2.shape
    dff = w_gate.shape[1]
    final = final_g is not None
    tile = pl.BlockSpec((ROW_TILE, d), lambda i: (i, 0))
    row = pl.BlockSpec((1, d), lambda i: (0, 0))
    resident = lambda r, c: pl.BlockSpec((r, c), lambda i: (0, 0), pipeline_mode=pl.Buffered(1))
    in_specs = [tile, row, resident(d, dff), resident(d, dff), resident(dff, d)]
    args = [x2, g.reshape(1, d), w_gate, w_up, w_down]
    if final:
        in_specs.append(row)
        args.append(final_g.reshape(1, d))
    return pl.pallas_call(
        functools.partial(_ffn_kernel, final=final),
        grid=(m // ROW_TILE,),
        in_specs=in_specs,
        out_specs=tile,
        out_shape=jax.ShapeDtypeStruct((m, d), F32),
        compiler_params=_params("parallel"),
        name="ffn",
    )(*args)


def _rope_tables(s):
    half = ROPE_DIM // 2
    inv_freq = ROPE_THETA ** (-jnp.arange(0, ROPE_DIM, 2, dtype=F32) / ROPE_DIM)
    ang = jnp.arange(s, dtype=F32)[:, None] * inv_freq[None, :]
    cos, sin = jnp.cos(ang), jnp.sin(ang)
    pad = HEAD_DIM - ROPE_DIM
    cos_h = jnp.concatenate([cos, cos, jnp.ones((s, pad), F32)], axis=1)
    sin_h = jnp.concatenate([-sin, sin, jnp.zeros((s, pad), F32)], axis=1)
    return jnp.tile(cos_h, (1, N_HEADS)), jnp.tile(sin_h, (1, N_HEADS))


def kernel(x, norm1_g, w_in, rwkv_mu, rwkv_w0, rwkv_w2, rwkv_a0, rwkv_a2, rwkv_g2, rwkv_k_k, rwkv_k_a, rwkv_r_k, rwkv_gn_w, rwkv_gn_b, rwkv_v0, rwkv_v1, rwkv_v2, hgrn_lb_logits, hgrn_gn_g, w_branch, w_out, norm2_g, w_ffn_gate, w_ffn_up, w_ffn_down, final_g):
    b, s, d = x.shape
    depth = w_in.shape[0]
    c = MIX_WIDTH
    m = b * s
    assert d == D_MODEL and s % SEQ_TILE == 0 and m % ROW_TILE == 0 and s % REC_TILE == 0
    assert LANES % (b * N_HEADS) == 0 and HEAD_DIM % (LANES // (b * N_HEADS)) == 0

    cos, sin = _rope_tables(s)
    lb_w = jax.nn.softmax(hgrn_lb_logits.astype(F32), axis=0)
    lower_bounds = jnp.cumsum(lb_w, axis=0) - lb_w[0]
    head_id = jnp.arange(c) // HEAD_DIM
    e = (head_id[:, None] == head_id[None, :]).astype(BF16)
    row = lambda t: t.reshape(1, -1)
    zpad = lambda t, top, bottom: jnp.pad(t, ((top, bottom), (0, 0)))

    x2 = x.reshape(m, d)
    v_first = None
    for layer in range(depth):
        w_l = w_in[layer].astype(BF16)
        g1 = norm1_g[layer]
        rp, hc, k_sb, qt_sb, vt_sb, k_m, qt_m, vt_m, k_mean = _in_proj(
            x2.reshape(b, s, d), g1, w_l, cos, sin)

        y_sb = _sb_attention(qt_sb, k_sb, vt_sb).reshape(m, c)

        prm = {
            "mu": row(rwkv_mu[layer]), "w0": row(rwkv_w0[layer]),
            "w2": zpad(rwkv_w2[layer], 0, LANES - RWKV_DECAY_LORA),
            "a0": row(rwkv_a0[layer]),
            "a2": zpad(rwkv_a2[layer], RWKV_DECAY_LORA, LANES - RWKV_DECAY_LORA - RWKV_AAA_LORA),
            "g2": rwkv_g2[layer], "k_k": row(rwkv_k_k[layer]), "k_a": row(rwkv_k_a[layer]),
            "r_k": row(rwkv_r_k[layer]),
        }
        if layer > 0:
            prm["v0"] = row(rwkv_v0[layer - 1])
            prm["v1"] = jnp.pad(rwkv_v1[layer - 1], ((0, 0), (0, LANES - RWKV_MV_LORA)))
            prm["v2"] = zpad(rwkv_v2[layer - 1], 0, LANES - RWKV_MV_LORA)
        r_, w_, k_, v_, an_, bb_, g_, bonus_ = _rwkv_prep(rp, prm, e, v_first)
        if layer == 0:
            v_first = v_
        y_rec = _rwkv_recurrence(r_, w_, k_, v_, an_, bb_)
        rwkv_out = (y_rec.reshape(m, c), g_.reshape(m, c), bonus_.reshape(m, c),
                    row(rwkv_gn_w[layer]), row(rwkv_gn_b[layer]))

        n_blk = k_mean.shape[1]
        k_mean = jnp.pad(k_mean.reshape(b, n_blk, c), ((0, 0), (0, -n_blk % 8), (0, 0)))
        y_moba = _moba_attention(qt_m, k_m, vt_m, k_mean).reshape(m, c)

        y_hgrn = _hgrn(hc, lower_bounds[layer], hgrn_gn_g[layer], e).reshape(m, c)

        x2 = _merge(x2, g1, w_l[:, GATE_OFF:], y_sb, rwkv_out, y_moba, y_hgrn, e,
                    w_branch[layer].astype(BF16), w_out[layer].astype(BF16))
        x2 = _ffn(x2, norm2_g[layer], w_ffn_gate[layer].astype(BF16), w_ffn_up[layer].astype(BF16),
                  w_ffn_down[layer].astype(BF16), final_g if layer == depth - 1 else None)
    return x2.reshape(b, s, d)
```

```python
import functools

import jax
import jax.numpy as jnp
from jax import lax
from jax.experimental import pallas as pl
from jax.experimental.pallas import tpu as pltpu

F32 = jnp.float32
BF16 = jnp.bfloat16

D_MODEL = 1024
HEAD_DIM = 64
N_HEADS = 4
MIX_WIDTH = N_HEADS * HEAD_DIM
N_BRANCH = 4
NORM_EPS = 1e-6
MASK_VALUE = -1e30

RWKV_DECAY_LORA = 64
RWKV_AAA_LORA = 64
RWKV_MV_LORA = 32
RWKV_GATE_LORA = 128
RWKV_GN_EPS = 64e-5
RWKV_COLS = 3 * MIX_WIDTH + RWKV_DECAY_LORA + RWKV_AAA_LORA + RWKV_GATE_LORA

MOBA_BLOCK = 256
MOBA_TOPK = 3
ROPE_THETA = 500000.0
ROPE_DIM = HEAD_DIM // 4

HGRN_CHUNK = 16

SB_COLS = 3 * MIX_WIDTH
MOBA_COLS = 3 * MIX_WIDTH
HGRN_COLS = 4 * MIX_WIDTH
SB_OFF = 0
RWKV_OFF = SB_OFF + SB_COLS
MOBA_OFF = RWKV_OFF + RWKV_COLS
HGRN_OFF = MOBA_OFF + MOBA_COLS
GATE_OFF = HGRN_OFF + HGRN_COLS

LANES = 128
VMEM_LIMIT = 56 * 1024 * 1024

ROW_TILE = 512
SEQ_TILE = 256
REC_TILE = 128


def _params(*sem):
    return pltpu.CompilerParams(dimension_semantics=sem, vmem_limit_bytes=VMEM_LIMIT)


def _dot(a, b):
    return jnp.dot(a, b, preferred_element_type=F32)


def _split2(a):
    hi = a.astype(BF16)
    lo = (a - hi.astype(F32)).astype(BF16)
    return hi, lo


def _split3(a):
    hi = a.astype(BF16)
    r1 = a - hi.astype(F32)
    mid = r1.astype(BF16)
    lo = (r1 - mid.astype(F32)).astype(BF16)
    return hi, mid, lo


def _dot_f32_lhs(a, m):
    hi, lo = _split2(a)
    return _dot(hi, m) + _dot(lo, m)


def _dot_f32_rhs(m, a):
    hi, mid, lo = _split3(a)
    return _dot(m, hi) + _dot(m, mid) + _dot(m, lo)


def _dot3(a, b):
    ah, al = _split2(a)
    bh, bl = _split2(b)
    return _dot(ah, bh) + _dot(ah, bl) + _dot(al, bh)


def _rms(x, g):
    return x * lax.rsqrt(jnp.mean(x * x, axis=-1, keepdims=True) + NORM_EPS) * g


_NT = (((1,), (1,)), ((), ()))


def _in_proj_kernel(x_ref, g_ref, wr_ref, wh_ref, sk_ref, sqt_ref, svt_ref, mk_ref, mqt_ref, mvt_ref,
                    cos_ref, sin_ref, cost_ref, sint_ref,
                    rw_o, hg_o, sk_o, sqt_o, svt_o, mk_o, mqt_o, mvt_o, km_o):
    h = _rms(x_ref[0], g_ref[...]).astype(BF16)
    nt = lambda w_ref: lax.dot_general(w_ref[...], h, _NT, preferred_element_type=F32)
    rw_o[0] = _dot(h, wr_ref[...])
    hg_o[0] = _dot(h, wh_ref[...])
    sk_o[0] = _dot(h, sk_ref[...]).astype(BF16)
    sqt_o[0] = nt(sqt_ref).astype(BF16)
    svt_o[0] = nt(svt_ref).astype(BF16)
    mvt_o[0] = nt(mvt_ref).astype(BF16)
    k = _dot(h, mk_ref[...])
    qt = nt(mqt_ref)
    c = k.shape[1]
    half = ROPE_DIM // 2
    lane = lax.broadcasted_iota(jnp.int32, k.shape, 1) % HEAD_DIM
    partner = jnp.where(lane < half, pltpu.roll(k, c - half, axis=1), pltpu.roll(k, half, axis=1))
    k = k * cos_ref[...] + partner * sin_ref[...]
    chan = lax.broadcasted_iota(jnp.int32, qt.shape, 0) % HEAD_DIM
    partner = jnp.where(chan < half, pltpu.roll(qt, c - half, axis=0), pltpu.roll(qt, half, axis=0))
    mqt_o[0] = qt * cost_ref[...] + partner * sint_ref[...]
    mk_o[0] = k.astype(BF16)
    for i in range(k.shape[0] // MOBA_BLOCK):
        km_o[0, i] = jnp.mean(k[i * MOBA_BLOCK:(i + 1) * MOBA_BLOCK], axis=0, keepdims=True)


def _in_proj(x3, g, w_l, cos, sin):
    b, s, d = x3.shape
    c = MIX_WIDTH
    t = ROW_TILE
    assert t % MOBA_BLOCK == 0
    resident = lambda m, n: pl.BlockSpec((m, n), lambda bi, j: (0, 0), pipeline_mode=pl.Buffered(1))
    rows = lambda n: pl.BlockSpec((1, t, n), lambda bi, j: (bi, j, 0))
    cols = pl.BlockSpec((1, c, t), lambda bi, j: (bi, 0, j))
    sb, mo = w_l[:, SB_OFF:SB_OFF + SB_COLS], w_l[:, MOBA_OFF:MOBA_OFF + MOBA_COLS]
    args = [x3, g.reshape(1, d), w_l[:, RWKV_OFF:RWKV_OFF + RWKV_COLS], w_l[:, HGRN_OFF:HGRN_OFF + HGRN_COLS],
            sb[:, c:2 * c], (sb[:, :c] * (HEAD_DIM ** -0.5)).T, sb[:, 2 * c:].T,
            mo[:, c:2 * c], mo[:, :c].T, mo[:, 2 * c:].T, cos, sin, cos.T, sin.T]
    in_specs = [rows(d), resident(1, d), resident(d, RWKV_COLS), resident(d, HGRN_COLS),
                resident(d, c), resident(c, d), resident(c, d),
                resident(d, c), resident(c, d), resident(c, d),
                pl.BlockSpec((t, c), lambda bi, j: (j, 0)), pl.BlockSpec((t, c), lambda bi, j: (j, 0)),
                pl.BlockSpec((c, t), lambda bi, j: (0, j)), pl.BlockSpec((c, t), lambda bi, j: (0, j))]
    n_mean = t // MOBA_BLOCK
    out_specs = [rows(RWKV_COLS), rows(HGRN_COLS), rows(c), cols, cols, rows(c), cols, cols,
                 pl.BlockSpec((1, n_mean, 1, c), lambda bi, j: (bi, j, 0, 0))]
    sds = jax.ShapeDtypeStruct
    out_shape = [sds((b, s, RWKV_COLS), F32), sds((b, s, HGRN_COLS), F32),
                 sds((b, s, c), BF16), sds((b, c, s), BF16), sds((b, c, s), BF16),
                 sds((b, s, c), BF16), sds((b, c, s), F32), sds((b, c, s), BF16),
                 sds((b, s // MOBA_BLOCK, 1, c), F32)]
    return pl.pallas_call(
        _in_proj_kernel,
        grid=(b, s // t),
        in_specs=in_specs,
        out_specs=out_specs,
        out_shape=out_shape,
        compiler_params=_params("parallel", "parallel"),
        name="in_proj",
    )(*args)


SB_DEAD_LOG = -110.0


def _sb_kernel(qt_ref, k_ref, vt_ref, ut_ref, o_ref, acc_s, *, tq, n_heads):
    i = pl.program_id(1)
    ut = ut_ref[...]
    key = lax.broadcasted_iota(jnp.int32, (tq, tq), 0)
    qry = lax.broadcasted_iota(jnp.int32, (tq, tq), 1)
    past = key < qry
    qt = qt_ref[0]
    chan_head = lax.broadcasted_iota(jnp.int32, qt.shape, 0) // HEAD_DIM
    qh = [jnp.where(chan_head == h, qt, jnp.zeros_like(qt)) for h in range(n_heads)]
    hs = [slice(h * HEAD_DIM, (h + 1) * HEAD_DIM) for h in range(n_heads)]

    def block(j, runs, diagonal):
        start = pl.multiple_of(j * tq, tq)
        heads = range(n_heads)
        kb = k_ref[0, pl.ds(start, tq), :]
        zs = [_dot(kb, qh[h]) for h in heads]
        lks, lbs, pieces = [], [], []
        for h in heads:
            z = zs[h]
            sp = jnp.maximum(z, 0.0) + jnp.log(1.0 + jnp.exp(-jnp.abs(z)))
            lk = -sp
            if diagonal:
                lk = jnp.where(past, lk, 0.0)
            lks.append(lk)
            lbs.append(z - sp)
            pieces.append(_split2(lk))
        betweens = [_dot(ut, hi) + _dot(ut, lo) for hi, lo in pieces]
        atts = []
        for h in heads:
            att = jnp.exp(lbs[h] + betweens[h] + runs[h])
            if diagonal:
                att = jnp.where(past, att, 0.0)
            atts.append(att.astype(BF16))
        for h in heads:
            part = _dot(vt_ref[0, hs[h], pl.ds(start, tq)], atts[h])
            acc_s[hs[h], :] = part if diagonal else acc_s[hs[h], :] + part
        return tuple(runs[h] + jnp.sum(lks[h], axis=0, keepdims=True) for h in heads)

    def live(runs):
        top = jnp.max(functools.reduce(jnp.maximum, runs))
        return (top > SB_DEAD_LOG).astype(jnp.int32)

    runs = block(i, tuple(jnp.zeros((1, tq), F32) for _ in range(n_heads)), True)

    def cond(c):
        jj, go, _ = c
        return jnp.logical_and(jj < i, go > 0)

    def body(c):
        jj, _, runs = c
        runs = block(i - 1 - jj, runs, False)
        return jj + 1, live(runs), runs

    lax.while_loop(cond, body, (jnp.int32(0), live(runs), runs))
    o_ref[0] = acc_s[...].T


def _sb_attention(qt, k, vt):
    b, s, c = k.shape
    tq = SEQ_TILE
    ut = (jnp.arange(tq)[None, :] > jnp.arange(tq)[:, None]).astype(BF16)
    return pl.pallas_call(
        functools.partial(_sb_kernel, tq=tq, n_heads=N_HEADS),
        grid=(b, s // tq),
        in_specs=[
            pl.BlockSpec((1, c, tq), lambda bi, i: (bi, 0, i)),
            pl.BlockSpec((1, s, c), lambda bi, i: (bi, 0, 0)),
            pl.BlockSpec((1, c, s), lambda bi, i: (bi, 0, 0)),
            pl.BlockSpec((tq, tq), lambda bi, i: (0, 0)),
        ],
        out_specs=pl.BlockSpec((1, tq, c), lambda bi, i: (bi, i, 0)),
        out_shape=jax.ShapeDtypeStruct((b, s, c), F32),
        scratch_shapes=[pltpu.VMEM((c, tq), F32)],
        compiler_params=_params("parallel", "parallel"),
        name="sb_attention",
    )(qt, k, vt, ut)


def _rwkv_prep_kernel(*refs, has_vres):
    if has_vres:
        (p_ref, mu_ref, w0_ref, w2_ref, a0_ref, a2_ref, g2_ref, kk_ref, ka_ref, rk_ref, e_ref,
         tri_ref, ones_ref, vf_ref, v0_ref, v1_ref, v2_ref,
         r_o, w_o, k_o, v_o, an_o, bb_o, g_o, bonus_o, carry) = refs
    else:
        (p_ref, mu_ref, w0_ref, w2_ref, a0_ref, a2_ref, g2_ref, kk_ref, ka_ref, rk_ref, e_ref,
         tri_ref, ones_ref,
         r_o, w_o, k_o, v_o, an_o, bb_o, g_o, bonus_o, carry) = refs
    c = MIX_WIDTH
    p = p_ref[0]
    t = p.shape[0]

    @pl.when(pl.program_id(1) == 0)
    def _():
        carry[...] = jnp.zeros_like(carry)

    rowid = lax.broadcasted_iota(jnp.int32, p.shape, 0)
    prev = jnp.where(rowid == 0, carry[...], pltpu.roll(p, 1, axis=0))
    carry[...] = p[t - 1:t, :]
    xs = p + (prev - p) * mu_ref[...]

    r = xs[:, :c]
    k = xs[:, c:2 * c]
    v = xs[:, 2 * c:3 * c]
    lora = xs[:, 3 * c:3 * c + LANES]
    gl = xs[:, 3 * c + LANES:]
    w_log = -jax.nn.softplus(-(w0_ref[...] + _dot3(jnp.tanh(lora), w2_ref[...]))) - 0.5
    log_decay = -jnp.exp(w_log)
    cum = _dot_f32_rhs(tri_ref[...], log_decay)
    cum_end = _dot_f32_rhs(ones_ref[...], log_decay)
    if has_vres:
        mix = jax.nn.sigmoid(v0_ref[...] + _dot3(_dot3(v, v1_ref[...]), v2_ref[...]))
        v = v + (vf_ref[0] - v) * mix
    a = jax.nn.sigmoid(a0_ref[...] + _dot3(lora, a2_ref[...]))
    g = _dot3(jax.nn.sigmoid(gl), g2_ref[...])
    e = e_ref[...]
    kk = k * kk_ref[...]
    kk = kk / jnp.maximum(jnp.sqrt(_dot_f32_lhs(kk * kk, e)), 1e-12)
    k = k * (1.0 + (a - 1.0) * ka_ref[...])
    bonus = _dot_f32_lhs(r * k * rk_ref[...], e) * v
    inv_gamma = jnp.exp(-cum)
    r_o[0] = r * jnp.exp(cum)
    w_o[0] = jnp.exp(cum_end)
    k_o[0] = k * inv_gamma
    v_o[0] = v
    an_o[0] = -kk * jnp.exp(cum - log_decay)
    bb_o[0] = kk * a * inv_gamma
    g_o[0] = g
    bonus_o[0] = bonus


def _rwkv_prep(p, prm, e, v_first):
    b, s, cols = p.shape
    c = MIX_WIDTH
    t = SEQ_TILE
    group = LANES // (b * N_HEADS)
    has_vres = v_first is not None
    ridx = jnp.arange(t)
    same = (ridx[:, None] // group) == (ridx[None, :] // group)
    tri = (same & (ridx[None, :] <= ridx[:, None])).astype(BF16)
    row = lambda n: pl.BlockSpec((1, n), lambda bi, j: (0, 0))
    mat = lambda m, n: pl.BlockSpec((m, n), lambda bi, j: (0, 0))
    seq = lambda n: pl.BlockSpec((1, t, n), lambda bi, j: (bi, j, 0))
    in_specs = [seq(cols), row(cols), row(c), mat(LANES, c), row(c), mat(LANES, c),
                mat(LANES, c), row(c), row(c), row(c), mat(c, c), mat(t, t), mat(t, t)]
    args = [p, prm["mu"], prm["w0"], prm["w2"], prm["a0"], prm["a2"], prm["g2"],
            prm["k_k"], prm["k_a"], prm["r_k"], e, tri, same.astype(BF16)]
    if has_vres:
        in_specs += [seq(c), row(c), mat(c, LANES), mat(LANES, c)]
        args += [v_first, prm["v0"], prm["v1"], prm["v2"]]
    out = jax.ShapeDtypeStruct((b, s, c), F32)
    return pl.pallas_call(
        functools.partial(_rwkv_prep_kernel, has_vres=has_vres),
        grid=(b, s // t),
        in_specs=in_specs,
        out_specs=[seq(c)] * 8,
        out_shape=[out] * 8,
        scratch_shapes=[pltpu.VMEM((1, cols), F32)],
        compiler_params=_params("parallel", "arbitrary"),
        name="rwkv_prep",
    )(*args)


def _rwkv_rec_kernel(an_ref, w_ref, bb_ref, k_ref, r_ref, v_ref, y_ref, st_ref, ex_ref, *, n_tiles, group):
    @pl.when(pl.program_id(0) == 0)
    def _():
        st_ref[...] = jnp.zeros_like(st_ref)

    nb = an_ref.shape[0]
    per_head = LANES // N_HEADS
    lane = lax.broadcasted_iota(jnp.int32, (HEAD_DIM, LANES), 1)
    lane_b = lane // (N_HEADS * group)
    lane_h = (lane // group) % N_HEADS

    def packed(ref, t0):
        w = jnp.concatenate([ref[:, t0 + u, :] for u in range(group)], axis=0)
        pieces = []
        for h in range(N_HEADS):
            col, off = divmod(h * HEAD_DIM, LANES)
            piece = w[:, col * LANES:(col + 1) * LANES]
            pieces.append(pltpu.roll(piece, LANES - off, axis=1) if off else piece)
        return jnp.concatenate(pieces, axis=0).T[:HEAD_DIM, :]

    operands = (an_ref, w_ref, bb_ref, k_ref, r_ref)
    n_groups = an_ref.shape[1] // group
    assert n_groups % 2 == 0

    def expand(t0, slot):
        packs = [packed(ref, t0) for ref in operands]
        for u in range(group):
            idx = lane_h * per_head + u * nb + lane_b
            for o, p in enumerate(packs):
                if o != 1 or u == 0:
                    ex_ref[slot, u, o] = jnp.take_along_axis(p, idx, axis=1)

    def advance(slot, t0, t_next):
        expand(t_next, 1 - slot)
        vts = [v_ref[t0 + u] for u in range(group)]
        ys = [[] for _ in range(group)]
        for j in range(n_tiles):
            st = st_ref[j]
            for u in range(group):
                sa = jnp.sum(st * ex_ref[slot, u, 0], axis=0, keepdims=True)
                st = st + sa * ex_ref[slot, u, 2] + vts[u][j:j + 1, :] * ex_ref[slot, u, 3]
                ys[u].append(jnp.sum(st * ex_ref[slot, u, 4], axis=0, keepdims=True))
            st_ref[j] = st * ex_ref[slot, 0, 1]
        for u in range(group):
            y_ref[t0 + u] = jnp.concatenate(ys[u], axis=0)

    def body(pair, carry):
        g = 2 * pair
        advance(0, g * group, (g + 1) * group)
        advance(1, (g + 1) * group, jnp.minimum(g + 2, n_groups - 1) * group)
        return carry

    expand(0, 0)
    lax.fori_loop(0, n_groups // 2, body, 0)


def _rwkv_recurrence(r, w, k, v, an, bb):
    b, s, c = r.shape
    group = LANES // (b * N_HEADS)
    n_tiles = HEAD_DIM // group
    ve = v.reshape(b, s, N_HEADS, n_tiles, group).transpose(1, 3, 0, 2, 4).reshape(s, n_tiles, LANES)
    big = pl.BlockSpec((b, REC_TILE, c), lambda i: (0, i, 0))
    small = pl.BlockSpec((REC_TILE, n_tiles, LANES), lambda i: (i, 0, 0))
    y = pl.pallas_call(
        functools.partial(_rwkv_rec_kernel, n_tiles=n_tiles, group=group),
        grid=(s // REC_TILE,),
        in_specs=[big] * 5 + [small],
        out_specs=small,
        out_shape=jax.ShapeDtypeStruct((s, n_tiles, LANES), F32),
        scratch_shapes=[pltpu.VMEM((n_tiles, HEAD_DIM, LANES), F32),
                        pltpu.VMEM((2, group, 5, HEAD_DIM, LANES), F32)],
        compiler_params=_params("arbitrary"),
        name="rwkv_recurrence",
    )(an, w, bb, k, r, ve)
    y = y.reshape(s, n_tiles, b, N_HEADS, group).transpose(2, 0, 3, 1, 4)
    return y.reshape(b, s, c)


def _moba_kernel(qt_ref, k_ref, vt_ref, km_ref, o_ref, acc_s, *, blk, n_sel, n_heads):
    i = pl.program_id(1)
    n_pad = km_ref.shape[1]
    blk_id = lax.broadcasted_iota(jnp.int32, (n_pad, blk), 0)
    fully_past = blk_id < i
    weight = jnp.where(blk_id < 24, jnp.left_shift(1, jnp.minimum(blk_id, 23)), 0).astype(F32)
    key = lax.broadcasted_iota(jnp.int32, (blk, blk), 0)
    qry = lax.broadcasted_iota(jnp.int32, (blk, blk), 1)
    causal = key <= qry
    qt_all = qt_ref[0]
    km_all = km_ref[0]
    chan_head = lax.broadcasted_iota(jnp.int32, qt_all.shape, 0) // HEAD_DIM
    lane_head = lax.broadcasted_iota(jnp.int32, km_all.shape, 1) // HEAD_DIM
    q_scaled = (qt_all * (HEAD_DIM ** -0.5)).astype(BF16)
    hs = [slice(h * HEAD_DIM, (h + 1) * HEAD_DIM) for h in range(n_heads)]

    qs, bits = [], []
    for h in range(n_heads):
        gate = jnp.dot(jnp.where(lane_head == h, km_all, 0.0), qt_all, preferred_element_type=F32,
                       precision=lax.Precision.HIGHEST)
        gate = jnp.where(fully_past, gate, MASK_VALUE)
        chosen = jnp.zeros(gate.shape, F32)
        for _ in range(n_sel):
            top = jnp.max(gate, axis=0, keepdims=True)
            idx = jnp.min(jnp.where(gate == top, blk_id, n_pad), axis=0, keepdims=True)
            pick = blk_id == idx
            chosen = jnp.where(pick & fully_past, 1.0, chosen)
            gate = jnp.where(pick, -jnp.inf, gate)
        bits.append(jnp.sum(chosen * weight, axis=0, keepdims=True).astype(jnp.int32))
        qs.append(jnp.where(chan_head == h, q_scaled, jnp.zeros_like(q_scaled)))

    def scores(n, h):
        start = pl.multiple_of(n * blk, blk)
        return _dot(k_ref[0, pl.ds(start, blk), :], qs[h]), vt_ref[0, hs[h], pl.ds(start, blk)]

    heads = range(n_heads)
    ms, ls, ps = [], [], []
    own = [scores(i, h) for h in heads]
    for h in heads:
        s = jnp.where(causal, own[h][0], MASK_VALUE)
        m = jnp.max(s, axis=0, keepdims=True)
        p = jnp.exp(s - m)
        ms.append(m)
        ls.append(jnp.sum(p, axis=0, keepdims=True))
        ps.append(p.astype(BF16))
    for h in heads:
        acc_s[hs[h], :] = _dot(own[h][1], ps[h])

    second = lax.broadcasted_iota(jnp.int32, (2 * blk, blk), 0) >= blk

    def pair(p, h):
        start = pl.multiple_of(p * (2 * blk), 2 * blk)
        return (_dot(k_ref[0, pl.ds(start, 2 * blk), :], qs[h]),
                vt_ref[0, hs[h], pl.ds(start, 2 * blk)])

    def body(step, carry):
        ms, ls = carry
        blocks = [pair(step, h) for h in heads]
        new_ms, new_ls, alphas, ps = [], [], [], []
        for h in heads:
            both = jnp.right_shift(bits[h], 2 * step)
            sel = jnp.where(second, jnp.right_shift(both, 1), both) & 1
            s = jnp.where(sel == 1, blocks[h][0], MASK_VALUE)
            m_new = jnp.maximum(ms[h], jnp.max(s, axis=0, keepdims=True))
            alpha = jnp.exp(ms[h] - m_new)
            p = jnp.exp(s - m_new)
            new_ms.append(m_new)
            new_ls.append(alpha * ls[h] + jnp.sum(p, axis=0, keepdims=True))
            alphas.append(alpha)
            ps.append(p.astype(BF16))
        for h in heads:
            acc_s[hs[h], :] = alphas[h] * acc_s[hs[h], :] + _dot(blocks[h][1], ps[h])
        return tuple(new_ms), tuple(new_ls)

    ms, ls = lax.fori_loop(0, (i + 1) // 2, body, (tuple(ms), tuple(ls)))
    for h in range(n_heads):
        acc_s[hs[h], :] = acc_s[hs[h], :] / ls[h]
    o_ref[0] = acc_s[...].T


def _moba_attention(qt, k, vt, km):
    b, s, c = k.shape
    blk = MOBA_BLOCK
    n_blk = s // blk
    n_pad = km.shape[1]
    assert s % blk == 0 and n_blk <= 24
    n_sel = min(MOBA_TOPK, n_blk)
    return pl.pallas_call(
        functools.partial(_moba_kernel, blk=blk, n_sel=n_sel, n_heads=N_HEADS),
        grid=(b, n_blk),
        in_specs=[
            pl.BlockSpec((1, c, blk), lambda bi, i: (bi, 0, i)),
            pl.BlockSpec((1, s, c), lambda bi, i: (bi, 0, 0)),
            pl.BlockSpec((1, c, s), lambda bi, i: (bi, 0, 0)),
            pl.BlockSpec((1, n_pad, c), lambda bi, i: (bi, 0, 0)),
        ],
        out_specs=pl.BlockSpec((1, blk, c), lambda bi, i: (bi, i, 0)),
        out_shape=jax.ShapeDtypeStruct((b, s, c), F32),
        scratch_shapes=[pltpu.VMEM((c, blk), F32)],
        compiler_params=_params("parallel", "parallel"),
        name="moba_attention",
    )(qt, k, vt, km)


def _hgrn_kernel(q_ref, f_ref, i_ref, g_ref, lb_ref, gn_ref, e_ref, tri_ref, ones_ref, hm_ref,
                 o_ref, st_ref, q_s, k_s, b_s, v_s, qd_s, ke_s, be_s, o_s, *, chunk):
    c = MIX_WIDTH
    rows = q_ref.shape[1]

    @pl.when(pl.program_id(1) == 0)
    def _():
        st_ref[...] = jnp.zeros_like(st_ref)

    lb = lb_ref[...]
    sig = jax.nn.sigmoid(f_ref[0])
    log_f = jnp.log(lb + (1.0 - lb) * sig)
    key = (1.0 - lb) * (1.0 - sig)
    b = _dot_f32_rhs(tri_ref[...], log_f)
    b_end = _dot_f32_rhs(ones_ref[...], log_f)
    q = q_ref[0]
    q_s[...] = q
    k_s[...] = key
    b_s[...] = b
    v_s[...] = i_ref[0]
    qd_s[...] = q * jnp.exp(b)
    ke_s[...] = key * jnp.exp(b_end - b)
    be_s[...] = b_end
    e = e_ref[...]
    head_mask = hm_ref[...]
    sub = lax.broadcasted_iota(jnp.int32, (chunk, c), 0)

    def body(n, carry):
        r0 = pl.multiple_of(n * chunk, chunk)
        sl = pl.ds(r0, chunk)
        qi = q_s[sl, :]
        ki = k_s[sl, :]
        bi = b_s[sl, :]
        vi = v_s[sl, :]
        parts = []
        for s in range(chunk):
            pr = qi * ki[s:s + 1, :] * jnp.exp(jnp.minimum(bi - bi[s:s + 1, :], 0.0))
            parts.append(jnp.where(sub >= s, pr, 0.0))
        sc = _dot_f32_lhs(jnp.concatenate(parts, axis=0), e)
        o = jnp.zeros((chunk, c), F32)
        for s in range(chunk):
            o = o + sc[s * chunk:(s + 1) * chunk, :] * vi[s:s + 1, :]
        st = st_ref[...]
        o = o + lax.dot_general(qd_s[sl, :].astype(BF16), st.astype(BF16),
                                (((1,), (1,)), ((), ())), preferred_element_type=F32)
        o_s[sl, :] = o
        kv = lax.dot_general(vi.astype(BF16), ke_s[sl, :].astype(BF16),
                             (((0,), (0,)), ((), ())), preferred_element_type=F32)
        st_ref[...] = st * jnp.exp(be_s[pl.ds(r0, 1), :]) + kv * head_mask
        return carry

    lax.fori_loop(0, rows // chunk, body, 0, unroll=8)
    o = o_s[...]
    ms = _dot_f32_lhs(o * o, e) * (1.0 / HEAD_DIM)
    g = g_ref[0]
    o_ref[0] = o * lax.rsqrt(ms + NORM_EPS) * gn_ref[...] * (g * jax.nn.sigmoid(g))


def _hgrn(hc, lb, gn_g, e):
    b, s, _ = hc.shape
    c = MIX_WIDTH
    t = SEQ_TILE
    ch = HGRN_CHUNK
    ridx = jnp.arange(t)
    same = (ridx[:, None] // ch) == (ridx[None, :] // ch)
    tri = (same & (ridx[None, :] <= ridx[:, None])).astype(BF16)
    ones = same.astype(BF16)
    seq = lambda n: pl.BlockSpec((1, t, c), lambda bi, j: (bi, j, n))
    row = pl.BlockSpec((1, c), lambda bi, j: (0, 0))
    mat = lambda n: pl.BlockSpec((n, n), lambda bi, j: (0, 0))
    return pl.pallas_call(
        functools.partial(_hgrn_kernel, chunk=ch),
        grid=(b, s // t),
        in_specs=[seq(0), seq(1), seq(2), seq(3), row, row, mat(c), mat(t), mat(t), mat(c)],
        out_specs=pl.BlockSpec((1, t, c), lambda bi, j: (bi, j, 0)),
        out_shape=jax.ShapeDtypeStruct((b, s, c), F32),
        scratch_shapes=[pltpu.VMEM((c, c), F32)] + [pltpu.VMEM((t, c), F32)] * 8,
        compiler_params=_params("parallel", "arbitrary"),
        name="hgrn2",
    )(hc, hc, hc, hc, lb.reshape(1, c), gn_g.reshape(1, c), e, tri, ones, e.astype(F32))


def _merge_kernel(x_ref, g_ref, wg_ref, y0_ref, yr_ref, rg_ref, bonus_ref, gw_ref, gb_ref, e_ref,
                  y2_ref, y3_ref, wb_ref, wo_ref, o_ref):
    x = x_ref[...]
    d = x.shape[1]
    h = _rms(x, g_ref[...]).astype(BF16)
    y = yr_ref[...]
    e = e_ref[...]
    yc = y - _dot_f32_lhs(y, e) * (1.0 / HEAD_DIM)
    var = _dot_f32_lhs(yc * yc, e) * (1.0 / HEAD_DIM)
    y_rwkv = (yc * lax.rsqrt(var + RWKV_GN_EPS) * gw_ref[...] + gb_ref[...] + bonus_ref[...]) * rg_ref[...]
    merged = None
    for n, y_n in enumerate((y0_ref[...], y_rwkv, y2_ref[...], y3_ref[...])):
        gate = jax.nn.sigmoid(_dot(h, wg_ref[:, n * d:(n + 1) * d]))
        term = gate * _dot(y_n.astype(BF16), wb_ref[n])
        merged = term if merged is None else merged + term
    o_ref[...] = x + _dot(merged.astype(BF16), wo_ref[...])


def _merge(x2, g, w_gate, y_sb, rwkv, y_moba, y_hgrn, e, w_branch, w_out):
    m, d = x2.shape
    c = MIX_WIDTH
    tile = pl.BlockSpec((ROW_TILE, d), lambda i: (i, 0))
    ytile = pl.BlockSpec((ROW_TILE, c), lambda i: (i, 0))
    crow = pl.BlockSpec((1, c), lambda i: (0, 0))
    return pl.pallas_call(
        _merge_kernel,
        grid=(m // ROW_TILE,),
        in_specs=[tile, pl.BlockSpec((1, d), lambda i: (0, 0)),
                  pl.BlockSpec((d, N_BRANCH * d), lambda i: (0, 0)),
                  ytile, ytile, ytile, ytile, crow, crow, pl.BlockSpec((c, c), lambda i: (0, 0)),
                  ytile, ytile,
                  pl.BlockSpec((N_BRANCH, c, d), lambda i: (0, 0, 0)),
                  pl.BlockSpec((d, d), lambda i: (0, 0))],
        out_specs=tile,
        out_shape=jax.ShapeDtypeStruct((m, d), F32),
        compiler_params=_params("parallel"),
        name="gated_merge",
    )(x2, g.reshape(1, d), w_gate, y_sb, *rwkv, e, y_moba, y_hgrn, w_branch, w_out)


def _ffn_kernel(*refs, final):
    if final:
        x_ref, g_ref, wg_ref, wu_ref, wd_ref, fg_ref, o_ref = refs
    else:
        x_ref, g_ref, wg_ref, wu_ref, wd_ref, o_ref = refs
    x = x_ref[...]
    h = _rms(x, g_ref[...]).astype(BF16)
    gate = _dot(h, wg_ref[...])
    act = gate * jax.nn.sigmoid(gate) * _dot(h, wu_ref[...])
    y = x + _dot(act.astype(BF16), wd_ref[...])
    o_ref[...] = _rms(y, fg_ref[...]) if final else y


def _ffn(x2, g, w_gate, w_up, w_down, final_g):
    m, d = x2.shape
    dff = w_gate.shape[1]
    final = final_g is not None
    tile = pl.BlockSpec((ROW_TILE, d), lambda i: (i, 0))
    row = pl.BlockSpec((1, d), lambda i: (0, 0))
    resident = lambda r, c: pl.BlockSpec((r, c), lambda i: (0, 0), pipeline_mode=pl.Buffered(1))
    in_specs = [tile, row, resident(d, dff), resident(d, dff), resident(dff, d)]
    args = [x2, g.reshape(1, d), w_gate, w_up, w_down]
    if final:
        in_specs.append(row)
        args.append(final_g.reshape(1, d))
    return pl.pallas_call(
        functools.partial(_ffn_kernel, final=final),
        grid=(m // ROW_TILE,),
        in_specs=in_specs,
        out_specs=tile,
        out_shape=jax.ShapeDtypeStruct((m, d), F32),
        compiler_params=_params("parallel"),
        name="ffn",
    )(*args)


def _rope_tables(s):
    half = ROPE_DIM // 2
    inv_freq = ROPE_THETA ** (-jnp.arange(0, ROPE_DIM, 2, dtype=F32) / ROPE_DIM)
    ang = jnp.arange(s, dtype=F32)[:, None] * inv_freq[None, :]
    cos, sin = jnp.cos(ang), jnp.sin(ang)
    pad = HEAD_DIM - ROPE_DIM
    cos_h = jnp.concatenate([cos, cos, jnp.ones((s, pad), F32)], axis=1)
    sin_h = jnp.concatenate([-sin, sin, jnp.zeros((s, pad), F32)], axis=1)
    return jnp.tile(cos_h, (1, N_HEADS)), jnp.tile(sin_h, (1, N_HEADS))


def kernel(x, norm1_g, w_in, rwkv_mu, rwkv_w0, rwkv_w2, rwkv_a0, rwkv_a2, rwkv_g2, rwkv_k_k, rwkv_k_a, rwkv_r_k, rwkv_gn_w, rwkv_gn_b, rwkv_v0, rwkv_v1, rwkv_v2, hgrn_lb_logits, hgrn_gn_g, w_branch, w_out, norm2_g, w_ffn_gate, w_ffn_up, w_ffn_down, final_g):
    b, s, d = x.shape
    depth = w_in.shape[0]
    c = MIX_WIDTH
    m = b * s
    assert d == D_MODEL and s % SEQ_TILE == 0 and m % ROW_TILE == 0 and s % REC_TILE == 0
    assert LANES % (b * N_HEADS) == 0 and HEAD_DIM % (LANES // (b * N_HEADS)) == 0

    cos, sin = _rope_tables(s)
    lb_w = jax.nn.softmax(hgrn_lb_logits.astype(F32), axis=0)
    lower_bounds = jnp.cumsum(lb_w, axis=0) - lb_w[0]
    head_id = jnp.arange(c) // HEAD_DIM
    e = (head_id[:, None] == head_id[None, :]).astype(BF16)
    row = lambda t: t.reshape(1, -1)
    zpad = lambda t, top, bottom: jnp.pad(t, ((top, bottom), (0, 0)))

    x2 = x.reshape(m, d)
    v_first = None
    for layer in range(depth):
        w_l = w_in[layer].astype(BF16)
        g1 = norm1_g[layer]
        rp, hc, k_sb, qt_sb, vt_sb, k_m, qt_m, vt_m, k_mean = _in_proj(
            x2.reshape(b, s, d), g1, w_l, cos, sin)

        y_sb = _sb_attention(qt_sb, k_sb, vt_sb).reshape(m, c)

        prm = {
            "mu": row(rwkv_mu[layer]), "w0": row(rwkv_w0[layer]),
            "w2": zpad(rwkv_w2[layer], 0, LANES - RWKV_DECAY_LORA),
            "a0": row(rwkv_a0[layer]),
            "a2": zpad(rwkv_a2[layer], RWKV_DECAY_LORA, LANES - RWKV_DECAY_LORA - RWKV_AAA_LORA),
            "g2": rwkv_g2[layer], "k_k": row(rwkv_k_k[layer]), "k_a": row(rwkv_k_a[layer]),
            "r_k": row(rwkv_r_k[layer]),
        }
        if layer > 0:
            prm["v0"] = row(rwkv_v0[layer - 1])
            prm["v1"] = jnp.pad(rwkv_v1[layer - 1], ((0, 0), (0, LANES - RWKV_MV_LORA)))
            prm["v2"] = zpad(rwkv_v2[layer - 1], 0, LANES - RWKV_MV_LORA)
        r_, w_, k_, v_, an_, bb_, g_, bonus_ = _rwkv_prep(rp, prm, e, v_first)
        if layer == 0:
            v_first = v_
        y_rec = _rwkv_recurrence(r_, w_, k_, v_, an_, bb_)
        rwkv_out = (y_rec.reshape(m, c), g_.reshape(m, c), bonus_.reshape(m, c),
                    row(rwkv_gn_w[layer]), row(rwkv_gn_b[layer]))

        n_blk = k_mean.shape[1]
        k_mean = jnp.pad(k_mean.reshape(b, n_blk, c), ((0, 0), (0, -n_blk % 8), (0, 0)))
        y_moba = _moba_attention(qt_m, k_m, vt_m, k_mean).reshape(m, c)

        y_hgrn = _hgrn(hc, lower_bounds[layer], hgrn_gn_g[layer], e).reshape(m, c)

        x2 = _merge(x2, g1, w_l[:, GATE_OFF:], y_sb, rwkv_out, y_moba, y_hgrn, e,
                    w_branch[layer].astype(BF16), w_out[layer].astype(BF16))
        x2 = _ffn(x2, norm2_g[layer], w_ffn_gate[layer].astype(BF16), w_ffn_up[layer].astype(BF16),
                  w_ffn_down[layer].astype(BF16), final_g if layer == depth - 1 else None)
    return x2.reshape(b, s, d)
```

```python
import functools

import jax
import jax.numpy as jnp
from jax import lax
from jax.experimental import pallas as pl
from jax.experimental.pallas import tpu as pltpu

F32 = jnp.float32
BF16 = jnp.bfloat16

D_MODEL = 1024
HEAD_DIM = 64
N_HEADS = 4
MIX_WIDTH = N_HEADS * HEAD_DIM
N_BRANCH = 4
NORM_EPS = 1e-6
MASK_VALUE = -1e30

RWKV_DECAY_LORA = 64
RWKV_AAA_LORA = 64
RWKV_MV_LORA = 32
RWKV_GATE_LORA = 128
RWKV_GN_EPS = 64e-5
RWKV_COLS = 3 * MIX_WIDTH + RWKV_DECAY_LORA + RWKV_AAA_LORA + RWKV_GATE_LORA

MOBA_BLOCK = 256
MOBA_TOPK = 3
ROPE_THETA = 500000.0
ROPE_DIM = HEAD_DIM // 4

HGRN_CHUNK = 16

SB_COLS = 3 * MIX_WIDTH
MOBA_COLS = 3 * MIX_WIDTH
HGRN_COLS = 4 * MIX_WIDTH
SB_OFF = 0
RWKV_OFF = SB_OFF + SB_COLS
MOBA_OFF = RWKV_OFF + RWKV_COLS
HGRN_OFF = MOBA_OFF + MOBA_COLS
GATE_OFF = HGRN_OFF + HGRN_COLS

LANES = 128
VMEM_LIMIT = 56 * 1024 * 1024

ROW_TILE = 512
SEQ_TILE = 256
REC_TILE = 256


def _params(*sem):
    return pltpu.CompilerParams(dimension_semantics=sem, vmem_limit_bytes=VMEM_LIMIT)


def _dot(a, b):
    return jnp.dot(a, b, preferred_element_type=F32)


def _split2(a):
    hi = a.astype(BF16)
    lo = (a - hi.astype(F32)).astype(BF16)
    return hi, lo


def _split3(a):
    hi = a.astype(BF16)
    r1 = a - hi.astype(F32)
    mid = r1.astype(BF16)
    lo = (r1 - mid.astype(F32)).astype(BF16)
    return hi, mid, lo


def _dot_f32_lhs(a, m):
    hi, lo = _split2(a)
    return _dot(hi, m) + _dot(lo, m)


def _dot_f32_rhs(m, a):
    hi, mid, lo = _split3(a)
    return _dot(m, hi) + _dot(m, mid) + _dot(m, lo)


def _dot3(a, b):
    ah, al = _split2(a)
    bh, bl = _split2(b)
    return _dot(ah, bh) + _dot(ah, bl) + _dot(al, bh)


def _rms(x, g):
    return x * lax.rsqrt(jnp.mean(x * x, axis=-1, keepdims=True) + NORM_EPS) * g


_NT = (((1,), (1,)), ((), ()))


def _in_proj_kernel(x_ref, g_ref, wr_ref, wh_ref, sk_ref, sqt_ref, svt_ref, mk_ref, mqt_ref, mvt_ref,
                    cos_ref, sin_ref, cost_ref, sint_ref,
                    rw_o, hg_o, sk_o, sqt_o, svt_o, mk_o, mqt_o, mvt_o, km_o):
    h = _rms(x_ref[0], g_ref[...]).astype(BF16)
    nt = lambda w_ref: lax.dot_general(w_ref[...], h, _NT, preferred_element_type=F32)
    rw_o[0] = _dot(h, wr_ref[...])
    hg_o[0] = _dot(h, wh_ref[...])
    sk_o[0] = _dot(h, sk_ref[...]).astype(BF16)
    sqt_o[0] = nt(sqt_ref).astype(BF16)
    svt_o[0] = nt(svt_ref).astype(BF16)
    mvt_o[0] = nt(mvt_ref).astype(BF16)
    k = _dot(h, mk_ref[...])
    qt = nt(mqt_ref)
    c = k.shape[1]
    half = ROPE_DIM // 2
    lane = lax.broadcasted_iota(jnp.int32, k.shape, 1) % HEAD_DIM
    partner = jnp.where(lane < half, pltpu.roll(k, c - half, axis=1), pltpu.roll(k, half, axis=1))
    k = k * cos_ref[...] + partner * sin_ref[...]
    chan = lax.broadcasted_iota(jnp.int32, qt.shape, 0) % HEAD_DIM
    partner = jnp.where(chan < half, pltpu.roll(qt, c - half, axis=0), pltpu.roll(qt, half, axis=0))
    mqt_o[0] = qt * cost_ref[...] + partner * sint_ref[...]
    mk_o[0] = k.astype(BF16)
    for i in range(k.shape[0] // MOBA_BLOCK):
        km_o[0, i] = jnp.mean(k[i * MOBA_BLOCK:(i + 1) * MOBA_BLOCK], axis=0, keepdims=True)


def _in_proj(x3, g, w_l, cos, sin):
    b, s, d = x3.shape
    c = MIX_WIDTH
    t = ROW_TILE
    assert t % MOBA_BLOCK == 0
    resident = lambda m, n: pl.BlockSpec((m, n), lambda bi, j: (0, 0), pipeline_mode=pl.Buffered(1))
    rows = lambda n: pl.BlockSpec((1, t, n), lambda bi, j: (bi, j, 0))
    cols = pl.BlockSpec((1, c, t), lambda bi, j: (bi, 0, j))
    sb, mo = w_l[:, SB_OFF:SB_OFF + SB_COLS], w_l[:, MOBA_OFF:MOBA_OFF + MOBA_COLS]
    args = [x3, g.reshape(1, d), w_l[:, RWKV_OFF:RWKV_OFF + RWKV_COLS], w_l[:, HGRN_OFF:HGRN_OFF + HGRN_COLS],
            sb[:, c:2 * c], (sb[:, :c] * (HEAD_DIM ** -0.5)).T, sb[:, 2 * c:].T,
            mo[:, c:2 * c], mo[:, :c].T, mo[:, 2 * c:].T, cos, sin, cos.T, sin.T]
    in_specs = [rows(d), resident(1, d), resident(d, RWKV_COLS), resident(d, HGRN_COLS),
                resident(d, c), resident(c, d), resident(c, d),
                resident(d, c), resident(c, d), resident(c, d),
                pl.BlockSpec((t, c), lambda bi, j: (j, 0)), pl.BlockSpec((t, c), lambda bi, j: (j, 0)),
                pl.BlockSpec((c, t), lambda bi, j: (0, j)), pl.BlockSpec((c, t), lambda bi, j: (0, j))]
    n_mean = t // MOBA_BLOCK
    out_specs = [rows(RWKV_COLS), rows(HGRN_COLS), rows(c), cols, cols, rows(c), cols, cols,
                 pl.BlockSpec((1, n_mean, 1, c), lambda bi, j: (bi, j, 0, 0))]
    sds = jax.ShapeDtypeStruct
    out_shape = [sds((b, s, RWKV_COLS), F32), sds((b, s, HGRN_COLS), F32),
                 sds((b, s, c), BF16), sds((b, c, s), BF16), sds((b, c, s), BF16),
                 sds((b, s, c), BF16), sds((b, c, s), F32), sds((b, c, s), BF16),
                 sds((b, s // MOBA_BLOCK, 1, c), F32)]
    return pl.pallas_call(
        _in_proj_kernel,
        grid=(b, s // t),
        in_specs=in_specs,
        out_specs=out_specs,
        out_shape=out_shape,
        compiler_params=_params("parallel", "parallel"),
        name="in_proj",
    )(*args)


SB_DEAD_LOG = -110.0


def _sb_kernel(qt_ref, k_ref, vt_ref, ut_ref, o_ref, acc_s, *, tq, n_heads):
    i = pl.program_id(1)
    ut = ut_ref[...]
    key = lax.broadcasted_iota(jnp.int32, (tq, tq), 0)
    qry = lax.broadcasted_iota(jnp.int32, (tq, tq), 1)
    past = key < qry
    qt = qt_ref[0]
    chan_head = lax.broadcasted_iota(jnp.int32, qt.shape, 0) // HEAD_DIM
    qh = [jnp.where(chan_head == h, qt, jnp.zeros_like(qt)) for h in range(n_heads)]
    hs = [slice(h * HEAD_DIM, (h + 1) * HEAD_DIM) for h in range(n_heads)]

    def block(j, runs, diagonal):
        start = pl.multiple_of(j * tq, tq)
        heads = range(n_heads)
        kb = k_ref[0, pl.ds(start, tq), :]
        zs = [_dot(kb, qh[h]) for h in heads]
        lks, lbs, pieces = [], [], []
        for h in heads:
            z = zs[h]
            sp = jnp.maximum(z, 0.0) + jnp.log(1.0 + jnp.exp(-jnp.abs(z)))
            lk = -sp
            if diagonal:
                lk = jnp.where(past, lk, 0.0)
            lks.append(lk)
            lbs.append(z - sp)
            pieces.append(_split2(lk))
        betweens = [_dot(ut, hi) + _dot(ut, lo) for hi, lo in pieces]
        atts = []
        for h in heads:
            att = jnp.exp(lbs[h] + betweens[h] + runs[h])
            if diagonal:
                att = jnp.where(past, att, 0.0)
            atts.append(att.astype(BF16))
        for h in heads:
            part = _dot(vt_ref[0, hs[h], pl.ds(start, tq)], atts[h])
            acc_s[hs[h], :] = part if diagonal else acc_s[hs[h], :] + part
        return tuple(runs[h] + jnp.sum(lks[h], axis=0, keepdims=True) for h in heads)

    def live(runs):
        top = jnp.max(functools.reduce(jnp.maximum, runs))
        return (top > SB_DEAD_LOG).astype(jnp.int32)

    runs = block(i, tuple(jnp.zeros((1, tq), F32) for _ in range(n_heads)), True)

    def cond(c):
        jj, go, _ = c
        return jnp.logical_and(jj < i, go > 0)

    def body(c):
        jj, _, runs = c
        runs = block(i - 1 - jj, runs, False)
        return jj + 1, live(runs), runs

    lax.while_loop(cond, body, (jnp.int32(0), live(runs), runs))
    o_ref[0] = acc_s[...].T


def _sb_attention(qt, k, vt):
    b, s, c = k.shape
    tq = SEQ_TILE
    ut = (jnp.arange(tq)[None, :] > jnp.arange(tq)[:, None]).astype(BF16)
    return pl.pallas_call(
        functools.partial(_sb_kernel, tq=tq, n_heads=N_HEADS),
        grid=(b, s // tq),
        in_specs=[
            pl.BlockSpec((1, c, tq), lambda bi, i: (bi, 0, i)),
            pl.BlockSpec((1, s, c), lambda bi, i: (bi, 0, 0)),
            pl.BlockSpec((1, c, s), lambda bi, i: (bi, 0, 0)),
            pl.BlockSpec((tq, tq), lambda bi, i: (0, 0)),
        ],
        out_specs=pl.BlockSpec((1, tq, c), lambda bi, i: (bi, i, 0)),
        out_shape=jax.ShapeDtypeStruct((b, s, c), F32),
        scratch_shapes=[pltpu.VMEM((c, tq), F32)],
        compiler_params=_params("parallel", "parallel"),
        name="sb_attention",
    )(qt, k, vt, ut)


def _rwkv_prep_kernel(*refs, has_vres):
    if has_vres:
        (p_ref, mu_ref, w0_ref, w2_ref, a0_ref, a2_ref, g2_ref, kk_ref, ka_ref, rk_ref, e_ref,
         tri_ref, ones_ref, vf_ref, v0_ref, v1_ref, v2_ref,
         r_o, w_o, k_o, v_o, an_o, bb_o, g_o, bonus_o, carry) = refs
    else:
        (p_ref, mu_ref, w0_ref, w2_ref, a0_ref, a2_ref, g2_ref, kk_ref, ka_ref, rk_ref, e_ref,
         tri_ref, ones_ref,
         r_o, w_o, k_o, v_o, an_o, bb_o, g_o, bonus_o, carry) = refs
    c = MIX_WIDTH
    p = p_ref[0]
    t = p.shape[0]

    @pl.when(pl.program_id(1) == 0)
    def _():
        carry[...] = jnp.zeros_like(carry)

    rowid = lax.broadcasted_iota(jnp.int32, p.shape, 0)
    prev = jnp.where(rowid == 0, carry[...], pltpu.roll(p, 1, axis=0))
    carry[...] = p[t - 1:t, :]
    xs = p + (prev - p) * mu_ref[...]

    r = xs[:, :c]
    k = xs[:, c:2 * c]
    v = xs[:, 2 * c:3 * c]
    lora = xs[:, 3 * c:3 * c + LANES]
    gl = xs[:, 3 * c + LANES:]
    w_log = -jax.nn.softplus(-(w0_ref[...] + _dot3(jnp.tanh(lora), w2_ref[...]))) - 0.5
    log_decay = -jnp.exp(w_log)
    cum = _dot_f32_rhs(tri_ref[...], log_decay)
    cum_end = _dot_f32_rhs(ones_ref[...], log_decay)
    if has_vres:
        mix = jax.nn.sigmoid(v0_ref[...] + _dot3(_dot3(v, v1_ref[...]), v2_ref[...]))
        v = v + (vf_ref[0] - v) * mix
    a = jax.nn.sigmoid(a0_ref[...] + _dot3(lora, a2_ref[...]))
    g = _dot3(jax.nn.sigmoid(gl), g2_ref[...])
    e = e_ref[...]
    kk = k * kk_ref[...]
    kk = kk / jnp.maximum(jnp.sqrt(_dot_f32_lhs(kk * kk, e)), 1e-12)
    k = k * (1.0 + (a - 1.0) * ka_ref[...])
    bonus = _dot_f32_lhs(r * k * rk_ref[...], e) * v
    inv_gamma = jnp.exp(-cum)
    r_o[0] = r * jnp.exp(cum)
    w_o[0] = jnp.exp(cum_end)
    k_o[0] = k * inv_gamma
    v_o[0] = v
    an_o[0] = -kk * jnp.exp(cum - log_decay)
    bb_o[0] = kk * a * inv_gamma
    g_o[0] = g
    bonus_o[0] = bonus


def _rwkv_prep(p, prm, e, v_first):
    b, s, cols = p.shape
    c = MIX_WIDTH
    t = SEQ_TILE
    group = LANES // (b * N_HEADS)
    has_vres = v_first is not None
    ridx = jnp.arange(t)
    same = (ridx[:, None] // group) == (ridx[None, :] // group)
    tri = (same & (ridx[None, :] <= ridx[:, None])).astype(BF16)
    row = lambda n: pl.BlockSpec((1, n), lambda bi, j: (0, 0))
    mat = lambda m, n: pl.BlockSpec((m, n), lambda bi, j: (0, 0))
    seq = lambda n: pl.BlockSpec((1, t, n), lambda bi, j: (bi, j, 0))
    in_specs = [seq(cols), row(cols), row(c), mat(LANES, c), row(c), mat(LANES, c),
                mat(LANES, c), row(c), row(c), row(c), mat(c, c), mat(t, t), mat(t, t)]
    args = [p, prm["mu"], prm["w0"], prm["w2"], prm["a0"], prm["a2"], prm["g2"],
            prm["k_k"], prm["k_a"], prm["r_k"], e, tri, same.astype(BF16)]
    if has_vres:
        in_specs += [seq(c), row(c), mat(c, LANES), mat(LANES, c)]
        args += [v_first, prm["v0"], prm["v1"], prm["v2"]]
    out = jax.ShapeDtypeStruct((b, s, c), F32)
    return pl.pallas_call(
        functools.partial(_rwkv_prep_kernel, has_vres=has_vres),
        grid=(b, s // t),
        in_specs=in_specs,
        out_specs=[seq(c)] * 8,
        out_shape=[out] * 8,
        scratch_shapes=[pltpu.VMEM((1, cols), F32)],
        compiler_params=_params("parallel", "arbitrary"),
        name="rwkv_prep",
    )(*args)


def _rwkv_rec_kernel(an_ref, w_ref, bb_ref, k_ref, r_ref, v_ref, y_ref, st_ref, ex_ref, *, n_tiles, group):
    @pl.when(pl.program_id(0) == 0)
    def _():
        st_ref[...] = jnp.zeros_like(st_ref)

    nb = an_ref.shape[0]
    per_head = LANES // N_HEADS
    lane = lax.broadcasted_iota(jnp.int32, (HEAD_DIM, LANES), 1)
    lane_b = lane // (N_HEADS * group)
    lane_h = (lane // group) % N_HEADS

    def packed(ref, t0):
        w = jnp.concatenate([ref[:, t0 + u, :] for u in range(group)], axis=0)
        pieces = []
        for h in range(N_HEADS):
            col, off = divmod(h * HEAD_DIM, LANES)
            piece = w[:, col * LANES:(col + 1) * LANES]
            pieces.append(pltpu.roll(piece, LANES - off, axis=1) if off else piece)
        return jnp.concatenate(pieces, axis=0).T[:HEAD_DIM, :]

    operands = (an_ref, w_ref, bb_ref, k_ref, r_ref)
    n_groups = an_ref.shape[1] // group
    assert n_groups % 2 == 0

    def expand(t0, slot):
        packs = [packed(ref, t0) for ref in operands]
        for u in range(group):
            idx = lane_h * per_head + u * nb + lane_b
            for o, p in enumerate(packs):
                if o != 1 or u == 0:
                    ex_ref[slot, u, o] = jnp.take_along_axis(p, idx, axis=1)

    def advance(slot, t0, t_next):
        expand(t_next, 1 - slot)
        for j in range(n_tiles):
            st = st_ref[j]
            for u in range(group):
                sa = jnp.sum(st * ex_ref[slot, u, 0], axis=0, keepdims=True)
                st = st + sa * ex_ref[slot, u, 2] + v_ref[t0 + u, j:j + 1, :] * ex_ref[slot, u, 3]
                y_ref[t0 + u, j:j + 1, :] = jnp.sum(st * ex_ref[slot, u, 4], axis=0, keepdims=True)
            st_ref[j] = st * ex_ref[slot, 0, 1]

    def body(pair, carry):
        g = 2 * pair
        advance(0, g * group, (g + 1) * group)
        advance(1, (g + 1) * group, jnp.minimum(g + 2, n_groups - 1) * group)
        return carry

    expand(0, 0)
    lax.fori_loop(0, n_groups // 2, body, 0)


def _rwkv_recurrence(r, w, k, v, an, bb):
    b, s, c = r.shape
    group = LANES // (b * N_HEADS)
    n_tiles = HEAD_DIM // group
    ve = v.reshape(b, s, N_HEADS, n_tiles, group).transpose(1, 3, 0, 2, 4).reshape(s, n_tiles, LANES)
    big = pl.BlockSpec((b, REC_TILE, c), lambda i: (0, i, 0))
    small = pl.BlockSpec((REC_TILE, n_tiles, LANES), lambda i: (i, 0, 0))
    y = pl.pallas_call(
        functools.partial(_rwkv_rec_kernel, n_tiles=n_tiles, group=group),
        grid=(s // REC_TILE,),
        in_specs=[big] * 5 + [small],
        out_specs=small,
        out_shape=jax.ShapeDtypeStruct((s, n_tiles, LANES), F32),
        scratch_shapes=[pltpu.VMEM((n_tiles, HEAD_DIM, LANES), F32),
                        pltpu.VMEM((2, group, 5, HEAD_DIM, LANES), F32)],
        compiler_params=_params("arbitrary"),
        name="rwkv_recurrence",
    )(an, w, bb, k, r, ve)
    y = y.reshape(s, n_tiles, b, N_HEADS, group).transpose(2, 0, 3, 1, 4)
    return y.reshape(b, s, c)


def _moba_kernel(qt_ref, k_ref, vt_ref, km_ref, o_ref, acc_s, *, blk, n_sel, n_heads):
    i = pl.program_id(1)
    n_pad = km_ref.shape[1]
    blk_id = lax.broadcasted_iota(jnp.int32, (n_pad, blk), 0)
    fully_past = blk_id < i
    weight = jnp.where(blk_id < 24, jnp.left_shift(1, jnp.minimum(blk_id, 23)), 0).astype(F32)
    key = lax.broadcasted_iota(jnp.int32, (blk, blk), 0)
    qry = lax.broadcasted_iota(jnp.int32, (blk, blk), 1)
    causal = key <= qry
    qt_all = qt_ref[0]
    km_all = km_ref[0]
    chan_head = lax.broadcasted_iota(jnp.int32, qt_all.shape, 0) // HEAD_DIM
    lane_head = lax.broadcasted_iota(jnp.int32, km_all.shape, 1) // HEAD_DIM
    q_scaled = (qt_all * (HEAD_DIM ** -0.5)).astype(BF16)
    hs = [slice(h * HEAD_DIM, (h + 1) * HEAD_DIM) for h in range(n_heads)]

    qs, bits = [], []
    for h in range(n_heads):
        gate = jnp.dot(jnp.where(lane_head == h, km_all, 0.0), qt_all, preferred_element_type=F32,
                       precision=lax.Precision.HIGHEST)
        gate = jnp.where(fully_past, gate, MASK_VALUE)
        chosen = jnp.zeros(gate.shape, F32)
        for _ in range(n_sel):
            top = jnp.max(gate, axis=0, keepdims=True)
            idx = jnp.min(jnp.where(gate == top, blk_id, n_pad), axis=0, keepdims=True)
            pick = blk_id == idx
            chosen = jnp.where(pick & fully_past, 1.0, chosen)
            gate = jnp.where(pick, -jnp.inf, gate)
        bits.append(jnp.sum(chosen * weight, axis=0, keepdims=True).astype(jnp.int32))
        qs.append(jnp.where(chan_head == h, q_scaled, jnp.zeros_like(q_scaled)))

    def scores(n, h):
        start = pl.multiple_of(n * blk, blk)
        return _dot(k_ref[0, pl.ds(start, blk), :], qs[h]), vt_ref[0, hs[h], pl.ds(start, blk)]

    heads = range(n_heads)
    ms, ls, ps = [], [], []
    own = [scores(i, h) for h in heads]
    for h in heads:
        s = jnp.where(causal, own[h][0], MASK_VALUE)
        m = jnp.max(s, axis=0, keepdims=True)
        p = jnp.exp(s - m)
        ms.append(m)
        ls.append(jnp.sum(p, axis=0, keepdims=True))
        ps.append(p.astype(BF16))
    for h in heads:
        acc_s[hs[h], :] = _dot(own[h][1], ps[h])

    second = lax.broadcasted_iota(jnp.int32, (2 * blk, blk), 0) >= blk

    def pair(p, h):
        start = pl.multiple_of(p * (2 * blk), 2 * blk)
        return (_dot(k_ref[0, pl.ds(start, 2 * blk), :], qs[h]),
                vt_ref[0, hs[h], pl.ds(start, 2 * blk)])

    def body(step, carry):
        ms, ls = carry
        blocks = [pair(step, h) for h in heads]
        new_ms, new_ls, alphas, ps = [], [], [], []
        for h in heads:
            both = jnp.right_shift(bits[h], 2 * step)
            sel = jnp.where(second, jnp.right_shift(both, 1), both) & 1
            s = jnp.where(sel == 1, blocks[h][0], MASK_VALUE)
            m_new = jnp.maximum(ms[h], jnp.max(s, axis=0, keepdims=True))
            alpha = jnp.exp(ms[h] - m_new)
            p = jnp.exp(s - m_new)
            new_ms.append(m_new)
            new_ls.append(alpha * ls[h] + jnp.sum(p, axis=0, keepdims=True))
            alphas.append(alpha)
            ps.append(p.astype(BF16))
        for h in heads:
            acc_s[hs[h], :] = alphas[h] * acc_s[hs[h], :] + _dot(blocks[h][1], ps[h])
        return tuple(new_ms), tuple(new_ls)

    ms, ls = lax.fori_loop(0, (i + 1) // 2, body, (tuple(ms), tuple(ls)))
    for h in range(n_heads):
        acc_s[hs[h], :] = acc_s[hs[h], :] / ls[h]
    o_ref[0] = acc_s[...].T


def _moba_attention(qt, k, vt, km):
    b, s, c = k.shape
    blk = MOBA_BLOCK
    n_blk = s // blk
    n_pad = km.shape[1]
    assert s % blk == 0 and n_blk <= 24
    n_sel = min(MOBA_TOPK, n_blk)
    return pl.pallas_call(
        functools.partial(_moba_kernel, blk=blk, n_sel=n_sel, n_heads=N_HEADS),
        grid=(b, n_blk),
        in_specs=[
            pl.BlockSpec((1, c, blk), lambda bi, i: (bi, 0, i)),
            pl.BlockSpec((1, s, c), lambda bi, i: (bi, 0, 0)),
            pl.BlockSpec((1, c, s), lambda bi, i: (bi, 0, 0)),
            pl.BlockSpec((1, n_pad, c), lambda bi, i: (bi, 0, 0)),
        ],
        out_specs=pl.BlockSpec((1, blk, c), lambda bi, i: (bi, i, 0)),
        out_shape=jax.ShapeDtypeStruct((b, s, c), F32),
        scratch_shapes=[pltpu.VMEM((c, blk), F32)],
        compiler_params=_params("parallel", "parallel"),
        name="moba_attention",
    )(qt, k, vt, km)


def _hgrn_kernel(q_ref, f_ref, i_ref, g_ref, lb_ref, gn_ref, e_ref, tri_ref, ones_ref, hm_ref,
                 o_ref, st_ref, q_s, k_s, b_s, v_s, qd_s, ke_s, be_s, o_s, *, chunk):
    c = MIX_WIDTH
    rows = q_ref.shape[1]

    @pl.when(pl.program_id(1) == 0)
    def _():
        st_ref[...] = jnp.zeros_like(st_ref)

    lb = lb_ref[...]
    sig = jax.nn.sigmoid(f_ref[0])
    log_f = jnp.log(lb + (1.0 - lb) * sig)
    key = (1.0 - lb) * (1.0 - sig)
    b = _dot_f32_rhs(tri_ref[...], log_f)
    b_end = _dot_f32_rhs(ones_ref[...], log_f)
    q = q_ref[0]
    q_s[...] = q
    k_s[...] = key
    b_s[...] = b
    v_s[...] = i_ref[0]
    qd_s[...] = q * jnp.exp(b)
    ke_s[...] = key * jnp.exp(b_end - b)
    be_s[...] = b_end
    e = e_ref[...]
    head_mask = hm_ref[...]
    sub = lax.broadcasted_iota(jnp.int32, (chunk, c), 0)

    def body(n, carry):
        r0 = pl.multiple_of(n * chunk, chunk)
        sl = pl.ds(r0, chunk)
        qi = q_s[sl, :]
        ki = k_s[sl, :]
        bi = b_s[sl, :]
        vi = v_s[sl, :]
        parts = []
        for s in range(chunk):
            pr = qi * ki[s:s + 1, :] * jnp.exp(jnp.minimum(bi - bi[s:s + 1, :], 0.0))
            parts.append(jnp.where(sub >= s, pr, 0.0))
        sc = _dot_f32_lhs(jnp.concatenate(parts, axis=0), e)
        o = jnp.zeros((chunk, c), F32)
        for s in range(chunk):
            o = o + sc[s * chunk:(s + 1) * chunk, :] * vi[s:s + 1, :]
        st = st_ref[...]
        o = o + lax.dot_general(qd_s[sl, :].astype(BF16), st.astype(BF16),
                                (((1,), (1,)), ((), ())), preferred_element_type=F32)
        o_s[sl, :] = o
        kv = lax.dot_general(vi.astype(BF16), ke_s[sl, :].astype(BF16),
                             (((0,), (0,)), ((), ())), preferred_element_type=F32)
        st_ref[...] = st * jnp.exp(be_s[pl.ds(r0, 1), :]) + kv * head_mask
        return carry

    lax.fori_loop(0, rows // chunk, body, 0, unroll=16)
    o = o_s[...]
    ms = _dot_f32_lhs(o * o, e) * (1.0 / HEAD_DIM)
    g = g_ref[0]
    o_ref[0] = o * lax.rsqrt(ms + NORM_EPS) * gn_ref[...] * (g * jax.nn.sigmoid(g))


def _hgrn(hc, lb, gn_g, e):
    b, s, _ = hc.shape
    c = MIX_WIDTH
    t = SEQ_TILE
    ch = HGRN_CHUNK
    ridx = jnp.arange(t)
    same = (ridx[:, None] // ch) == (ridx[None, :] // ch)
    tri = (same & (ridx[None, :] <= ridx[:, None])).astype(BF16)
    ones = same.astype(BF16)
    seq = lambda n: pl.BlockSpec((1, t, c), lambda bi, j: (bi, j, n))
    row = pl.BlockSpec((1, c), lambda bi, j: (0, 0))
    mat = lambda n: pl.BlockSpec((n, n), lambda bi, j: (0, 0))
    return pl.pallas_call(
        functools.partial(_hgrn_kernel, chunk=ch),
        grid=(b, s // t),
        in_specs=[seq(0), seq(1), seq(2), seq(3), row, row, mat(c), mat(t), mat(t), mat(c)],
        out_specs=pl.BlockSpec((1, t, c), lambda bi, j: (bi, j, 0)),
        out_shape=jax.ShapeDtypeStruct((b, s, c), F32),
        scratch_shapes=[pltpu.VMEM((c, c), F32)] + [pltpu.VMEM((t, c), F32)] * 8,
        compiler_params=_params("parallel", "arbitrary"),
        name="hgrn2",
    )(hc, hc, hc, hc, lb.reshape(1, c), gn_g.reshape(1, c), e, tri, ones, e.astype(F32))


def _merge_kernel(x_ref, g_ref, wg_ref, y0_ref, yr_ref, rg_ref, bonus_ref, gw_ref, gb_ref, e_ref,
                  y2_ref, y3_ref, wb_ref, wo_ref, o_ref):
    x = x_ref[...]
    d = x.shape[1]
    h = _rms(x, g_ref[...]).astype(BF16)
    y = yr_ref[...]
    e = e_ref[...]
    yc = y - _dot_f32_lhs(y, e) * (1.0 / HEAD_DIM)
    var = _dot_f32_lhs(yc * yc, e) * (1.0 / HEAD_DIM)
    y_rwkv = (yc * lax.rsqrt(var + RWKV_GN_EPS) * gw_ref[...] + gb_ref[...] + bonus_ref[...]) * rg_ref[...]
    merged = None
    for n, y_n in enumerate((y0_ref[...], y_rwkv, y2_ref[...], y3_ref[...])):
        gate = jax.nn.sigmoid(_dot(h, wg_ref[:, n * d:(n + 1) * d]))
        term = gate * _dot(y_n.astype(BF16), wb_ref[n])
        merged = term if merged is None else merged + term
    o_ref[...] = x + _dot(merged.astype(BF16), wo_ref[...])


def _merge(x2, g, w_gate, y_sb, rwkv, y_moba, y_hgrn, e, w_branch, w_out):
    m, d = x2.shape
    c = MIX_WIDTH
    tile = pl.BlockSpec((ROW_TILE, d), lambda i: (i, 0))
    ytile = pl.BlockSpec((ROW_TILE, c), lambda i: (i, 0))
    crow = pl.BlockSpec((1, c), lambda i: (0, 0))
    return pl.pallas_call(
        _merge_kernel,
        grid=(m // ROW_TILE,),
        in_specs=[tile, pl.BlockSpec((1, d), lambda i: (0, 0)),
                  pl.BlockSpec((d, N_BRANCH * d), lambda i: (0, 0)),
                  ytile, ytile, ytile, ytile, crow, crow, pl.BlockSpec((c, c), lambda i: (0, 0)),
                  ytile, ytile,
                  pl.BlockSpec((N_BRANCH, c, d), lambda i: (0, 0, 0)),
                  pl.BlockSpec((d, d), lambda i: (0, 0))],
        out_specs=tile,
        out_shape=jax.ShapeDtypeStruct((m, d), F32),
        compiler_params=_params("parallel"),
        name="gated_merge",
    )(x2, g.reshape(1, d), w_gate, y_sb, *rwkv, e, y_moba, y_hgrn, w_branch, w_out)


def _ffn_kernel(*refs, final):
    if final:
        x_ref, g_ref, wg_ref, wu_ref, wd_ref, fg_ref, o_ref = refs
    else:
        x_ref, g_ref, wg_ref, wu_ref, wd_ref, o_ref = refs
    x = x_ref[...]
    h = _rms(x, g_ref[...]).astype(BF16)
    gate = _dot(h, wg_ref[...])
    act = gate * jax.nn.sigmoid(gate) * _dot(h, wu_ref[...])
    y = x + _dot(act.astype(BF16), wd_ref[...])
    o_ref[...] = _rms(y, fg_ref[...]) if final else y


def _ffn(x2, g, w_gate, w_up, w_down, final_g):
    m, d = x2.shape
    dff = w_gate.shape[1]
    final = final_g is not None
    tile = pl.BlockSpec((ROW_TILE, d), lambda i: (i, 0))
    row = pl.BlockSpec((1, d), lambda i: (0, 0))
    resident = lambda r, c: pl.BlockSpec((r, c), lambda i: (0, 0), pipeline_mode=pl.Buffered(1))
    in_specs = [tile, row, resident(d, dff), resident(d, dff), resident(dff, d)]
    args = [x2, g.reshape(1, d), w_gate, w_up, w_down]
    if final:
        in_specs.append(row)
        args.append(final_g.reshape(1, d))
    return pl.pallas_call(
        functools.partial(_ffn_kernel, final=final),
        grid=(m // ROW_TILE,),
        in_specs=in_specs,
        out_specs=tile,
        out_shape=jax.ShapeDtypeStruct((m, d), F32),
        compiler_params=_params("parallel"),
        name="ffn",
    )(*args)


def _rope_tables(s):
    half = ROPE_DIM // 2
    inv_freq = ROPE_THETA ** (-jnp.arange(0, ROPE_DIM, 2, dtype=F32) / ROPE_DIM)
    ang = jnp.arange(s, dtype=F32)[:, None] * inv_freq[None, :]
    cos, sin = jnp.cos(ang), jnp.sin(ang)
    pad = HEAD_DIM - ROPE_DIM
    cos_h = jnp.concatenate([cos, cos, jnp.ones((s, pad), F32)], axis=1)
    sin_h = jnp.concatenate([-sin, sin, jnp.zeros((s, pad), F32)], axis=1)
    return jnp.tile(cos_h, (1, N_HEADS)), jnp.tile(sin_h, (1, N_HEADS))


def kernel(x, norm1_g, w_in, rwkv_mu, rwkv_w0, rwkv_w2, rwkv_a0, rwkv_a2, rwkv_g2, rwkv_k_k, rwkv_k_a, rwkv_r_k, rwkv_gn_w, rwkv_gn_b, rwkv_v0, rwkv_v1, rwkv_v2, hgrn_lb_logits, hgrn_gn_g, w_branch, w_out, norm2_g, w_ffn_gate, w_ffn_up, w_ffn_down, final_g):
    b, s, d = x.shape
    depth = w_in.shape[0]
    c = MIX_WIDTH
    m = b * s
    assert d == D_MODEL and s % SEQ_TILE == 0 and m % ROW_TILE == 0 and s % REC_TILE == 0
    assert LANES % (b * N_HEADS) == 0 and HEAD_DIM % (LANES // (b * N_HEADS)) == 0

    cos, sin = _rope_tables(s)
    lb_w = jax.nn.softmax(hgrn_lb_logits.astype(F32), axis=0)
    lower_bounds = jnp.cumsum(lb_w, axis=0) - lb_w[0]
    head_id = jnp.arange(c) // HEAD_DIM
    e = (head_id[:, None] == head_id[None, :]).astype(BF16)
    row = lambda t: t.reshape(1, -1)
    zpad = lambda t, top, bottom: jnp.pad(t, ((top, bottom), (0, 0)))

    x2 = x.reshape(m, d)
    v_first = None
    for layer in range(depth):
        w_l = w_in[layer].astype(BF16)
        g1 = norm1_g[layer]
        rp, hc, k_sb, qt_sb, vt_sb, k_m, qt_m, vt_m, k_mean = _in_proj(
            x2.reshape(b, s, d), g1, w_l, cos, sin)

        y_sb = _sb_attention(qt_sb, k_sb, vt_sb).reshape(m, c)

        prm = {
            "mu": row(rwkv_mu[layer]), "w0": row(rwkv_w0[layer]),
            "w2": zpad(rwkv_w2[layer], 0, LANES - RWKV_DECAY_LORA),
            "a0": row(rwkv_a0[layer]),
            "a2": zpad(rwkv_a2[layer], RWKV_DECAY_LORA, LANES - RWKV_DECAY_LORA - RWKV_AAA_LORA),
            "g2": rwkv_g2[layer], "k_k": row(rwkv_k_k[layer]), "k_a": row(rwkv_k_a[layer]),
            "r_k": row(rwkv_r_k[layer]),
        }
        if layer > 0:
            prm["v0"] = row(rwkv_v0[layer - 1])
            prm["v1"] = jnp.pad(rwkv_v1[layer - 1], ((0, 0), (0, LANES - RWKV_MV_LORA)))
            prm["v2"] = zpad(rwkv_v2[layer - 1], 0, LANES - RWKV_MV_LORA)
        r_, w_, k_, v_, an_, bb_, g_, bonus_ = _rwkv_prep(rp, prm, e, v_first)
        if layer == 0:
            v_first = v_
        y_rec = _rwkv_recurrence(r_, w_, k_, v_, an_, bb_)
        rwkv_out = (y_rec.reshape(m, c), g_.reshape(m, c), bonus_.reshape(m, c),
                    row(rwkv_gn_w[layer]), row(rwkv_gn_b[layer]))

        n_blk = k_mean.shape[1]
        k_mean = jnp.pad(k_mean.reshape(b, n_blk, c), ((0, 0), (0, -n_blk % 8), (0, 0)))
        y_moba = _moba_attention(qt_m, k_m, vt_m, k_mean).reshape(m, c)

        y_hgrn = _hgrn(hc, lower_bounds[layer], hgrn_gn_g[layer], e).reshape(m, c)

        x2 = _merge(x2, g1, w_l[:, GATE_OFF:], y_sb, rwkv_out, y_moba, y_hgrn, e,
                    w_branch[layer].astype(BF16), w_out[layer].astype(BF16))
        x2 = _ffn(x2, norm2_g[layer], w_ffn_gate[layer].astype(BF16), w_ffn_up[layer].astype(BF16),
                  w_ffn_down[layer].astype(BF16), final_g if layer == depth - 1 else None)
    return x2.reshape(b, s, d)
```

```python
import functools

import jax
import jax.numpy as jnp
from jax import lax
from jax.experimental import pallas as pl
from jax.experimental.pallas import tpu as pltpu

F32 = jnp.float32
BF16 = jnp.bfloat16

D_MODEL = 1024
HEAD_DIM = 64
N_HEADS = 4
MIX_WIDTH = N_HEADS * HEAD_DIM
N_BRANCH = 4
NORM_EPS = 1e-6
MASK_VALUE = -1e30

RWKV_DECAY_LORA = 64
RWKV_AAA_LORA = 64
RWKV_MV_LORA = 32
RWKV_GATE_LORA = 128
RWKV_GN_EPS = 64e-5
RWKV_COLS = 3 * MIX_WIDTH + RWKV_DECAY_LORA + RWKV_AAA_LORA + RWKV_GATE_LORA

MOBA_BLOCK = 256
MOBA_TOPK = 3
ROPE_THETA = 500000.0
ROPE_DIM = HEAD_DIM // 4

HGRN_CHUNK = 16

SB_COLS = 3 * MIX_WIDTH
MOBA_COLS = 3 * MIX_WIDTH
HGRN_COLS = 4 * MIX_WIDTH
SB_OFF = 0
RWKV_OFF = SB_OFF + SB_COLS
MOBA_OFF = RWKV_OFF + RWKV_COLS
HGRN_OFF = MOBA_OFF + MOBA_COLS
GATE_OFF = HGRN_OFF + HGRN_COLS

LANES = 128
VMEM_LIMIT = 56 * 1024 * 1024

ROW_TILE = 512
SEQ_TILE = 256
REC_TILE = 256


def _params(*sem):
    return pltpu.CompilerParams(dimension_semantics=sem, vmem_limit_bytes=VMEM_LIMIT)


def _dot(a, b):
    return jnp.dot(a, b, preferred_element_type=F32)


def _split2(a):
    hi = a.astype(BF16)
    lo = (a - hi.astype(F32)).astype(BF16)
    return hi, lo


def _split3(a):
    hi = a.astype(BF16)
    r1 = a - hi.astype(F32)
    mid = r1.astype(BF16)
    lo = (r1 - mid.astype(F32)).astype(BF16)
    return hi, mid, lo


def _dot_f32_lhs(a, m):
    hi, lo = _split2(a)
    return _dot(hi, m) + _dot(lo, m)


def _dot_f32_rhs(m, a):
    hi, mid, lo = _split3(a)
    return _dot(m, hi) + _dot(m, mid) + _dot(m, lo)


def _dot3(a, b):
    ah, al = _split2(a)
    bh, bl = _split2(b)
    return _dot(ah, bh) + _dot(ah, bl) + _dot(al, bh)


def _rms(x, g):
    return x * lax.rsqrt(jnp.mean(x * x, axis=-1, keepdims=True) + NORM_EPS) * g


_NT = (((1,), (1,)), ((), ()))


def _in_proj_kernel(x_ref, g_ref, wr_ref, wh_ref, sk_ref, sqt_ref, svt_ref, mk_ref, mqt_ref, mvt_ref,
                    cos_ref, sin_ref, cost_ref, sint_ref,
                    rw_o, hg_o, sk_o, sqt_o, svt_o, mk_o, mqt_o, mvt_o, km_o):
    h = _rms(x_ref[0], g_ref[...]).astype(BF16)
    nt = lambda w_ref: lax.dot_general(w_ref[...], h, _NT, preferred_element_type=F32)
    rw_o[0] = _dot(h, wr_ref[...])
    hg_o[0] = _dot(h, wh_ref[...])
    sk_o[0] = _dot(h, sk_ref[...]).astype(BF16)
    sqt_o[0] = nt(sqt_ref).astype(BF16)
    svt_o[0] = nt(svt_ref).astype(BF16)
    mvt_o[0] = nt(mvt_ref).astype(BF16)
    k = _dot(h, mk_ref[...])
    qt = nt(mqt_ref)
    c = k.shape[1]
    half = ROPE_DIM // 2
    lane = lax.broadcasted_iota(jnp.int32, k.shape, 1) % HEAD_DIM
    partner = jnp.where(lane < half, pltpu.roll(k, c - half, axis=1), pltpu.roll(k, half, axis=1))
    k = k * cos_ref[...] + partner * sin_ref[...]
    chan = lax.broadcasted_iota(jnp.int32, qt.shape, 0) % HEAD_DIM
    partner = jnp.where(chan < half, pltpu.roll(qt, c - half, axis=0), pltpu.roll(qt, half, axis=0))
    mqt_o[0] = qt * cost_ref[...] + partner * sint_ref[...]
    mk_o[0] = k.astype(BF16)
    for i in range(k.shape[0] // MOBA_BLOCK):
        km_o[0, i] = jnp.mean(k[i * MOBA_BLOCK:(i + 1) * MOBA_BLOCK], axis=0, keepdims=True)


def _in_proj(x3, g, w_l, cos, sin):
    b, s, d = x3.shape
    c = MIX_WIDTH
    t = ROW_TILE
    assert t % MOBA_BLOCK == 0
    resident = lambda m, n: pl.BlockSpec((m, n), lambda bi, j: (0, 0), pipeline_mode=pl.Buffered(1))
    rows = lambda n: pl.BlockSpec((1, t, n), lambda bi, j: (bi, j, 0))
    cols = pl.BlockSpec((1, c, t), lambda bi, j: (bi, 0, j))
    sb, mo = w_l[:, SB_OFF:SB_OFF + SB_COLS], w_l[:, MOBA_OFF:MOBA_OFF + MOBA_COLS]
    args = [x3, g.reshape(1, d), w_l[:, RWKV_OFF:RWKV_OFF + RWKV_COLS], w_l[:, HGRN_OFF:HGRN_OFF + HGRN_COLS],
            sb[:, c:2 * c], (sb[:, :c] * (HEAD_DIM ** -0.5)).T, sb[:, 2 * c:].T,
            mo[:, c:2 * c], mo[:, :c].T, mo[:, 2 * c:].T, cos, sin, cos.T, sin.T]
    in_specs = [rows(d), resident(1, d), resident(d, RWKV_COLS), resident(d, HGRN_COLS),
                resident(d, c), resident(c, d), resident(c, d),
                resident(d, c), resident(c, d), resident(c, d),
                pl.BlockSpec((t, c), lambda bi, j: (j, 0)), pl.BlockSpec((t, c), lambda bi, j: (j, 0)),
                pl.BlockSpec((c, t), lambda bi, j: (0, j)), pl.BlockSpec((c, t), lambda bi, j: (0, j))]
    n_mean = t // MOBA_BLOCK
    out_specs = [rows(RWKV_COLS), rows(HGRN_COLS), rows(c), cols, cols, rows(c), cols, cols,
                 pl.BlockSpec((1, n_mean, 1, c), lambda bi, j: (bi, j, 0, 0))]
    sds = jax.ShapeDtypeStruct
    out_shape = [sds((b, s, RWKV_COLS), F32), sds((b, s, HGRN_COLS), F32),
                 sds((b, s, c), BF16), sds((b, c, s), BF16), sds((b, c, s), BF16),
                 sds((b, s, c), BF16), sds((b, c, s), F32), sds((b, c, s), BF16),
                 sds((b, s // MOBA_BLOCK, 1, c), F32)]
    return pl.pallas_call(
        _in_proj_kernel,
        grid=(b, s // t),
        in_specs=in_specs,
        out_specs=out_specs,
        out_shape=out_shape,
        compiler_params=_params("parallel", "parallel"),
        name="in_proj",
    )(*args)


SB_DEAD_LOG = -110.0


def _sb_kernel(qt_ref, k_ref, vt_ref, ut_ref, o_ref, acc_s, *, tq, n_heads):
    i = pl.program_id(1)
    ut = ut_ref[...]
    key = lax.broadcasted_iota(jnp.int32, (tq, tq), 0)
    qry = lax.broadcasted_iota(jnp.int32, (tq, tq), 1)
    past = key < qry
    qt = qt_ref[0]
    chan_head = lax.broadcasted_iota(jnp.int32, qt.shape, 0) // HEAD_DIM
    qh = [jnp.where(chan_head == h, qt, jnp.zeros_like(qt)) for h in range(n_heads)]
    hs = [slice(h * HEAD_DIM, (h + 1) * HEAD_DIM) for h in range(n_heads)]

    def block(j, runs, diagonal):
        start = pl.multiple_of(j * tq, tq)
        heads = range(n_heads)
        kb = k_ref[0, pl.ds(start, tq), :]
        zs = [_dot(kb, qh[h]) for h in heads]
        lks, lbs, pieces = [], [], []
        for h in heads:
            z = zs[h]
            sp = jnp.maximum(z, 0.0) + jnp.log(1.0 + jnp.exp(-jnp.abs(z)))
            lk = -sp
            if diagonal:
                lk = jnp.where(past, lk, 0.0)
            lks.append(lk)
            lbs.append(z - sp)
            pieces.append(_split2(lk))
        betweens = [_dot(ut, hi) + _dot(ut, lo) for hi, lo in pieces]
        atts = []
        for h in heads:
            att = jnp.exp(lbs[h] + betweens[h] + runs[h])
            if diagonal:
                att = jnp.where(past, att, 0.0)
            atts.append(att.astype(BF16))
        for h in heads:
            part = _dot(vt_ref[0, hs[h], pl.ds(start, tq)], atts[h])
            acc_s[hs[h], :] = part if diagonal else acc_s[hs[h], :] + part
        return tuple(runs[h] + jnp.sum(lks[h], axis=0, keepdims=True) for h in heads)

    def live(runs):
        top = jnp.max(functools.reduce(jnp.maximum, runs))
        return (top > SB_DEAD_LOG).astype(jnp.int32)

    runs = block(i, tuple(jnp.zeros((1, tq), F32) for _ in range(n_heads)), True)

    def cond(c):
        jj, go, _ = c
        return jnp.logical_and(jj < i, go > 0)

    def body(c):
        jj, _, runs = c
        runs = block(i - 1 - jj, runs, False)
        return jj + 1, live(runs), runs

    lax.while_loop(cond, body, (jnp.int32(0), live(runs), runs))
    o_ref[0] = acc_s[...].T


def _sb_attention(qt, k, vt):
    b, s, c = k.shape
    tq = SEQ_TILE
    ut = (jnp.arange(tq)[None, :] > jnp.arange(tq)[:, None]).astype(BF16)
    return pl.pallas_call(
        functools.partial(_sb_kernel, tq=tq, n_heads=N_HEADS),
        grid=(b, s // tq),
        in_specs=[
            pl.BlockSpec((1, c, tq), lambda bi, i: (bi, 0, i)),
            pl.BlockSpec((1, s, c), lambda bi, i: (bi, 0, 0)),
            pl.BlockSpec((1, c, s), lambda bi, i: (bi, 0, 0)),
            pl.BlockSpec((tq, tq), lambda bi, i: (0, 0)),
        ],
        out_specs=pl.BlockSpec((1, tq, c), lambda bi, i: (bi, i, 0)),
        out_shape=jax.ShapeDtypeStruct((b, s, c), F32),
        scratch_shapes=[pltpu.VMEM((c, tq), F32)],
        compiler_params=_params("parallel", "parallel"),
        name="sb_attention",
    )(qt, k, vt, ut)


def _rwkv_prep_kernel(*refs, has_vres):
    if has_vres:
        (p_ref, mu_ref, w0_ref, w2_ref, a0_ref, a2_ref, g2_ref, kk_ref, ka_ref, rk_ref, e_ref,
         tri_ref, ones_ref, vf_ref, v0_ref, v1_ref, v2_ref,
         r_o, w_o, k_o, v_o, an_o, bb_o, g_o, bonus_o, carry) = refs
    else:
        (p_ref, mu_ref, w0_ref, w2_ref, a0_ref, a2_ref, g2_ref, kk_ref, ka_ref, rk_ref, e_ref,
         tri_ref, ones_ref,
         r_o, w_o, k_o, v_o, an_o, bb_o, g_o, bonus_o, carry) = refs
    c = MIX_WIDTH
    p = p_ref[0]
    t = p.shape[0]

    @pl.when(pl.program_id(1) == 0)
    def _():
        carry[...] = jnp.zeros_like(carry)

    rowid = lax.broadcasted_iota(jnp.int32, p.shape, 0)
    prev = jnp.where(rowid == 0, carry[...], pltpu.roll(p, 1, axis=0))
    carry[...] = p[t - 1:t, :]
    xs = p + (prev - p) * mu_ref[...]

    r = xs[:, :c]
    k = xs[:, c:2 * c]
    v = xs[:, 2 * c:3 * c]
    lora = xs[:, 3 * c:3 * c + LANES]
    gl = xs[:, 3 * c + LANES:]
    w_log = -jax.nn.softplus(-(w0_ref[...] + _dot3(jnp.tanh(lora), w2_ref[...]))) - 0.5
    log_decay = -jnp.exp(w_log)
    cum = _dot_f32_rhs(tri_ref[...], log_decay)
    cum_end = _dot_f32_rhs(ones_ref[...], log_decay)
    if has_vres:
        mix = jax.nn.sigmoid(v0_ref[...] + _dot3(_dot3(v, v1_ref[...]), v2_ref[...]))
        v = v + (vf_ref[0] - v) * mix
    a = jax.nn.sigmoid(a0_ref[...] + _dot3(lora, a2_ref[...]))
    g = _dot3(jax.nn.sigmoid(gl), g2_ref[...])
    e = e_ref[...]
    kk = k * kk_ref[...]
    kk = kk / jnp.maximum(jnp.sqrt(_dot_f32_lhs(kk * kk, e)), 1e-12)
    k = k * (1.0 + (a - 1.0) * ka_ref[...])
    bonus = _dot_f32_lhs(r * k * rk_ref[...], e) * v
    inv_gamma = jnp.exp(-cum)
    r_o[0] = r * jnp.exp(cum)
    w_o[0] = jnp.exp(cum_end)
    k_o[0] = k * inv_gamma
    v_o[0] = v
    an_o[0] = -kk * jnp.exp(cum - log_decay)
    bb_o[0] = kk * a * inv_gamma
    g_o[0] = g
    bonus_o[0] = bonus


def _rwkv_prep(p, prm, e, v_first):
    b, s, cols = p.shape
    c = MIX_WIDTH
    t = SEQ_TILE
    group = LANES // (b * N_HEADS)
    has_vres = v_first is not None
    ridx = jnp.arange(t)
    same = (ridx[:, None] // group) == (ridx[None, :] // group)
    tri = (same & (ridx[None, :] <= ridx[:, None])).astype(BF16)
    row = lambda n: pl.BlockSpec((1, n), lambda bi, j: (0, 0))
    mat = lambda m, n: pl.BlockSpec((m, n), lambda bi, j: (0, 0))
    seq = lambda n: pl.BlockSpec((1, t, n), lambda bi, j: (bi, j, 0))
    in_specs = [seq(cols), row(cols), row(c), mat(LANES, c), row(c), mat(LANES, c),
                mat(LANES, c), row(c), row(c), row(c), mat(c, c), mat(t, t), mat(t, t)]
    args = [p, prm["mu"], prm["w0"], prm["w2"], prm["a0"], prm["a2"], prm["g2"],
            prm["k_k"], prm["k_a"], prm["r_k"], e, tri, same.astype(BF16)]
    if has_vres:
        in_specs += [seq(c), row(c), mat(c, LANES), mat(LANES, c)]
        args += [v_first, prm["v0"], prm["v1"], prm["v2"]]
    out = jax.ShapeDtypeStruct((b, s, c), F32)
    return pl.pallas_call(
        functools.partial(_rwkv_prep_kernel, has_vres=has_vres),
        grid=(b, s // t),
        in_specs=in_specs,
        out_specs=[seq(c)] * 8,
        out_shape=[out] * 8,
        scratch_shapes=[pltpu.VMEM((1, cols), F32)],
        compiler_params=_params("parallel", "arbitrary"),
        name="rwkv_prep",
    )(*args)


def _rwkv_rec_kernel(an_ref, w_ref, bb_ref, k_ref, r_ref, v_ref, y_ref, st_ref, ex_ref, *, n_tiles, group):
    @pl.when(pl.program_id(0) == 0)
    def _():
        st_ref[...] = jnp.zeros_like(st_ref)

    nb = an_ref.shape[0]
    per_head = LANES // N_HEADS
    lane = lax.broadcasted_iota(jnp.int32, (HEAD_DIM, LANES), 1)
    lane_b = lane // (N_HEADS * group)
    lane_h = (lane // group) % N_HEADS

    def packed(ref, t0):
        w = jnp.concatenate([ref[:, t0 + u, :] for u in range(group)], axis=0)
        pieces = []
        for h in range(N_HEADS):
            col, off = divmod(h * HEAD_DIM, LANES)
            piece = w[:, col * LANES:(col + 1) * LANES]
            pieces.append(pltpu.roll(piece, LANES - off, axis=1) if off else piece)
        return jnp.concatenate(pieces, axis=0).T[:HEAD_DIM, :]

    operands = (an_ref, w_ref, bb_ref, k_ref, r_ref)
    n_groups = an_ref.shape[1] // group
    assert n_groups % 2 == 0

    def expand(t0, slot):
        packs = [packed(ref, t0) for ref in operands]
        for u in range(group):
            idx = lane_h * per_head + u * nb + lane_b
            for o, p in enumerate(packs):
                if o != 1 or u == 0:
                    ex_ref[slot, u, o] = jnp.take_along_axis(p, idx, axis=1)

    def advance(slot, t0, t_next):
        expand(t_next, 1 - slot)
        for j in range(n_tiles):
            st = st_ref[j]
            for u in range(group):
                sa = jnp.sum(st * ex_ref[slot, u, 0], axis=0, keepdims=True)
                st = st + sa * ex_ref[slot, u, 2] + v_ref[t0 + u, j:j + 1, :] * ex_ref[slot, u, 3]
                y_ref[t0 + u, j:j + 1, :] = jnp.sum(st * ex_ref[slot, u, 4], axis=0, keepdims=True)
            st_ref[j] = st * ex_ref[slot, 0, 1]

    def body(pair, carry):
        g = 2 * pair
        advance(0, g * group, (g + 1) * group)
        advance(1, (g + 1) * group, jnp.minimum(g + 2, n_groups - 1) * group)
        return carry

    expand(0, 0)
    lax.fori_loop(0, n_groups // 2, body, 0)


def _rwkv_recurrence(r, w, k, v, an, bb):
    b, s, c = r.shape
    group = LANES // (b * N_HEADS)
    n_tiles = HEAD_DIM // group
    ve = v.reshape(b, s, N_HEADS, n_tiles, group).transpose(1, 3, 0, 2, 4).reshape(s, n_tiles, LANES)
    big = pl.BlockSpec((b, REC_TILE, c), lambda i: (0, i, 0))
    small = pl.BlockSpec((REC_TILE, n_tiles, LANES), lambda i: (i, 0, 0))
    y = pl.pallas_call(
        functools.partial(_rwkv_rec_kernel, n_tiles=n_tiles, group=group),
        grid=(s // REC_TILE,),
        in_specs=[big] * 5 + [small],
        out_specs=small,
        out_shape=jax.ShapeDtypeStruct((s, n_tiles, LANES), F32),
        scratch_shapes=[pltpu.VMEM((n_tiles, HEAD_DIM, LANES), F32),
                        pltpu.VMEM((2, group, 5, HEAD_DIM, LANES), F32)],
        compiler_params=_params("arbitrary"),
        name="rwkv_recurrence",
    )(an, w, bb, k, r, ve)
    y = y.reshape(s, n_tiles, b, N_HEADS, group).transpose(2, 0, 3, 1, 4)
    return y.reshape(b, s, c)


def _moba_kernel(qt_ref, k_ref, vt_ref, km_ref, o_ref, acc_s, *, blk, n_sel, n_heads):
    i = pl.program_id(1)
    n_pad = km_ref.shape[1]
    blk_id = lax.broadcasted_iota(jnp.int32, (n_pad, blk), 0)
    fully_past = blk_id < i
    weight = jnp.where(blk_id < 24, jnp.left_shift(1, jnp.minimum(blk_id, 23)), 0).astype(F32)
    key = lax.broadcasted_iota(jnp.int32, (blk, blk), 0)
    qry = lax.broadcasted_iota(jnp.int32, (blk, blk), 1)
    causal = key <= qry
    qt_all = qt_ref[0]
    km_all = km_ref[0]
    chan_head = lax.broadcasted_iota(jnp.int32, qt_all.shape, 0) // HEAD_DIM
    lane_head = lax.broadcasted_iota(jnp.int32, km_all.shape, 1) // HEAD_DIM
    q_scaled = (qt_all * (HEAD_DIM ** -0.5)).astype(BF16)
    hs = [slice(h * HEAD_DIM, (h + 1) * HEAD_DIM) for h in range(n_heads)]

    qs, bits = [], []
    for h in range(n_heads):
        gate = jnp.dot(km_all[:, hs[h]], qt_all[hs[h], :], preferred_element_type=F32,
                       precision=lax.Precision.HIGHEST)
        gate = jnp.where(fully_past, gate, MASK_VALUE)
        chosen = jnp.zeros(gate.shape, F32)
        for _ in range(n_sel):
            top = jnp.max(gate, axis=0, keepdims=True)
            idx = jnp.min(jnp.where(gate == top, blk_id, n_pad), axis=0, keepdims=True)
            pick = blk_id == idx
            chosen = jnp.where(pick & fully_past, 1.0, chosen)
            gate = jnp.where(pick, -jnp.inf, gate)
        bits.append(jnp.sum(chosen * weight, axis=0, keepdims=True).astype(jnp.int32))
        qs.append(jnp.where(chan_head == h, q_scaled, jnp.zeros_like(q_scaled)))

    def scores(n, h):
        start = pl.multiple_of(n * blk, blk)
        return _dot(k_ref[0, pl.ds(start, blk), :], qs[h]), vt_ref[0, hs[h], pl.ds(start, blk)]

    heads = range(n_heads)
    ms, ls, ps = [], [], []
    own = [scores(i, h) for h in heads]
    for h in heads:
        s = jnp.where(causal, own[h][0], MASK_VALUE)
        m = jnp.max(s, axis=0, keepdims=True)
        p = jnp.exp(s - m)
        ms.append(m)
        ls.append(jnp.sum(p, axis=0, keepdims=True))
        ps.append(p.astype(BF16))
    for h in heads:
        acc_s[hs[h], :] = _dot(own[h][1], ps[h])

    second = lax.broadcasted_iota(jnp.int32, (2 * blk, blk), 0) >= blk

    def pair(p, h):
        start = pl.multiple_of(p * (2 * blk), 2 * blk)
        return (_dot(k_ref[0, pl.ds(start, 2 * blk), :], qs[h]),
                vt_ref[0, hs[h], pl.ds(start, 2 * blk)])

    def body(step, carry):
        ms, ls = carry
        blocks = [pair(step, h) for h in heads]
        new_ms, new_ls, alphas, ps = [], [], [], []
        for h in heads:
            both = jnp.right_shift(bits[h], 2 * step)
            sel = jnp.where(second, jnp.right_shift(both, 1), both) & 1
            s = jnp.where(sel == 1, blocks[h][0], MASK_VALUE)
            m_new = jnp.maximum(ms[h], jnp.max(s, axis=0, keepdims=True))
            alpha = jnp.exp(ms[h] - m_new)
            p = jnp.exp(s - m_new)
            new_ms.append(m_new)
            new_ls.append(alpha * ls[h] + jnp.sum(p, axis=0, keepdims=True))
            alphas.append(alpha)
            ps.append(p.astype(BF16))
        for h in heads:
            acc_s[hs[h], :] = alphas[h] * acc_s[hs[h], :] + _dot(blocks[h][1], ps[h])
        return tuple(new_ms), tuple(new_ls)

    ms, ls = lax.fori_loop(0, (i + 1) // 2, body, (tuple(ms), tuple(ls)))
    for h in range(n_heads):
        acc_s[hs[h], :] = acc_s[hs[h], :] / ls[h]
    o_ref[0] = acc_s[...].T


def _moba_attention(qt, k, vt, km):
    b, s, c = k.shape
    blk = MOBA_BLOCK
    n_blk = s // blk
    n_pad = km.shape[1]
    assert s % blk == 0 and n_blk <= 24
    n_sel = min(MOBA_TOPK, n_blk)
    return pl.pallas_call(
        functools.partial(_moba_kernel, blk=blk, n_sel=n_sel, n_heads=N_HEADS),
        grid=(b, n_blk),
        in_specs=[
            pl.BlockSpec((1, c, blk), lambda bi, i: (bi, 0, i)),
            pl.BlockSpec((1, s, c), lambda bi, i: (bi, 0, 0)),
            pl.BlockSpec((1, c, s), lambda bi, i: (bi, 0, 0)),
            pl.BlockSpec((1, n_pad, c), lambda bi, i: (bi, 0, 0)),
        ],
        out_specs=pl.BlockSpec((1, blk, c), lambda bi, i: (bi, i, 0)),
        out_shape=jax.ShapeDtypeStruct((b, s, c), F32),
        scratch_shapes=[pltpu.VMEM((c, blk), F32)],
        compiler_params=_params("parallel", "parallel"),
        name="moba_attention",
    )(qt, k, vt, km)


def _hgrn_kernel(q_ref, f_ref, i_ref, g_ref, lb_ref, gn_ref, e_ref, tri_ref, ones_ref, hm_ref,
                 o_ref, st_ref, q_s, k_s, b_s, v_s, qd_s, ke_s, be_s, o_s, *, chunk):
    c = MIX_WIDTH
    rows = q_ref.shape[1]

    @pl.when(pl.program_id(1) == 0)
    def _():
        st_ref[...] = jnp.zeros_like(st_ref)

    lb = lb_ref[...]
    sig = jax.nn.sigmoid(f_ref[0])
    log_f = jnp.log(lb + (1.0 - lb) * sig)
    key = (1.0 - lb) * (1.0 - sig)
    b = _dot_f32_rhs(tri_ref[...], log_f)
    b_end = _dot_f32_rhs(ones_ref[...], log_f)
    q = q_ref[0]
    q_s[...] = q
    k_s[...] = key
    b_s[...] = b
    v_s[...] = i_ref[0]
    qd_s[...] = q * jnp.exp(b)
    ke_s[...] = key * jnp.exp(b_end - b)
    be_s[...] = b_end
    e = e_ref[...]
    head_mask = hm_ref[...]
    sub = lax.broadcasted_iota(jnp.int32, (chunk, c), 0)

    def body(n, carry):
        r0 = pl.multiple_of(n * chunk, chunk)
        sl = pl.ds(r0, chunk)
        qi = q_s[sl, :]
        ki = k_s[sl, :]
        bi = b_s[sl, :]
        vi = v_s[sl, :]
        parts = []
        for s in range(chunk):
            pr = qi * ki[s:s + 1, :] * jnp.exp(jnp.minimum(bi - bi[s:s + 1, :], 0.0))
            parts.append(jnp.where(sub >= s, pr, 0.0))
        sc = _dot_f32_lhs(jnp.concatenate(parts, axis=0), e)
        o = jnp.zeros((chunk, c), F32)
        for s in range(chunk):
            o = o + sc[s * chunk:(s + 1) * chunk, :] * vi[s:s + 1, :]
        st = st_ref[...]
        o = o + lax.dot_general(qd_s[sl, :].astype(BF16), st.astype(BF16),
                                (((1,), (1,)), ((), ())), preferred_element_type=F32)
        o_s[sl, :] = o
        kv = lax.dot_general(vi.astype(BF16), ke_s[sl, :].astype(BF16),
                             (((0,), (0,)), ((), ())), preferred_element_type=F32)
        st_ref[...] = st * jnp.exp(be_s[pl.ds(r0, 1), :]) + kv * head_mask
        return carry

    lax.fori_loop(0, rows // chunk, body, 0, unroll=16)
    o = o_s[...]
    ms = _dot_f32_lhs(o * o, e) * (1.0 / HEAD_DIM)
    g = g_ref[0]
    o_ref[0] = o * lax.rsqrt(ms + NORM_EPS) * gn_ref[...] * (g * jax.nn.sigmoid(g))


def _hgrn(hc, lb, gn_g, e):
    b, s, _ = hc.shape
    c = MIX_WIDTH
    t = SEQ_TILE
    ch = HGRN_CHUNK
    ridx = jnp.arange(t)
    same = (ridx[:, None] // ch) == (ridx[None, :] // ch)
    tri = (same & (ridx[None, :] <= ridx[:, None])).astype(BF16)
    ones = same.astype(BF16)
    seq = lambda n: pl.BlockSpec((1, t, c), lambda bi, j: (bi, j, n))
    row = pl.BlockSpec((1, c), lambda bi, j: (0, 0))
    mat = lambda n: pl.BlockSpec((n, n), lambda bi, j: (0, 0))
    return pl.pallas_call(
        functools.partial(_hgrn_kernel, chunk=ch),
        grid=(b, s // t),
        in_specs=[seq(0), seq(1), seq(2), seq(3), row, row, mat(c), mat(t), mat(t), mat(c)],
        out_specs=pl.BlockSpec((1, t, c), lambda bi, j: (bi, j, 0)),
        out_shape=jax.ShapeDtypeStruct((b, s, c), F32),
        scratch_shapes=[pltpu.VMEM((c, c), F32)] + [pltpu.VMEM((t, c), F32)] * 8,
        compiler_params=_params("parallel", "arbitrary"),
        name="hgrn2",
    )(hc, hc, hc, hc, lb.reshape(1, c), gn_g.reshape(1, c), e, tri, ones, e.astype(F32))


def _merge_kernel(x_ref, g_ref, wg_ref, y0_ref, yr_ref, rg_ref, bonus_ref, gw_ref, gb_ref, e_ref,
                  y2_ref, y3_ref, wb_ref, wo_ref, o_ref):
    x = x_ref[...]
    d = x.shape[1]
    h = _rms(x, g_ref[...]).astype(BF16)
    y = yr_ref[...]
    e = e_ref[...]
    yc = y - _dot_f32_lhs(y, e) * (1.0 / HEAD_DIM)
    var = _dot_f32_lhs(yc * yc, e) * (1.0 / HEAD_DIM)
    y_rwkv = (yc * lax.rsqrt(var + RWKV_GN_EPS) * gw_ref[...] + gb_ref[...] + bonus_ref[...]) * rg_ref[...]
    merged = None
    for n, y_n in enumerate((y0_ref[...], y_rwkv, y2_ref[...], y3_ref[...])):
        gate = jax.nn.sigmoid(_dot(h, wg_ref[:, n * d:(n + 1) * d]))
        term = gate * _dot(y_n.astype(BF16), wb_ref[n])
        merged = term if merged is None else merged + term
    o_ref[...] = x + _dot(merged.astype(BF16), wo_ref[...])


def _merge(x2, g, w_gate, y_sb, rwkv, y_moba, y_hgrn, e, w_branch, w_out):
    m, d = x2.shape
    c = MIX_WIDTH
    tile = pl.BlockSpec((ROW_TILE, d), lambda i: (i, 0))
    ytile = pl.BlockSpec((ROW_TILE, c), lambda i: (i, 0))
    crow = pl.BlockSpec((1, c), lambda i: (0, 0))
    return pl.pallas_call(
        _merge_kernel,
        grid=(m // ROW_TILE,),
        in_specs=[tile, pl.BlockSpec((1, d), lambda i: (0, 0)),
                  pl.BlockSpec((d, N_BRANCH * d), lambda i: (0, 0)),
                  ytile, ytile, ytile, ytile, crow, crow, pl.BlockSpec((c, c), lambda i: (0, 0)),
                  ytile, ytile,
                  pl.BlockSpec((N_BRANCH, c, d), lambda i: (0, 0, 0)),
                  pl.BlockSpec((d, d), lambda i: (0, 0))],
        out_specs=tile,
        out_shape=jax.ShapeDtypeStruct((m, d), F32),
        compiler_params=_params("parallel"),
        name="gated_merge",
    )(x2, g.reshape(1, d), w_gate, y_sb, *rwkv, e, y_moba, y_hgrn, w_branch, w_out)


def _ffn_kernel(*refs, final):
    if final:
        x_ref, g_ref, wg_ref, wu_ref, wd_ref, fg_ref, o_ref = refs
    else:
        x_ref, g_ref, wg_ref, wu_ref, wd_ref, o_ref = refs
    x = x_ref[...]
    h = _rms(x, g_ref[...]).astype(BF16)
    gate = _dot(h, wg_ref[...])
    act = gate * jax.nn.sigmoid(gate) * _dot(h, wu_ref[...])
    y = x + _dot(act.astype(BF16), wd_ref[...])
    o_ref[...] = _rms(y, fg_ref[...]) if final else y


def _ffn(x2, g, w_gate, w_up, w_down, final_g):
    m, d = x2.shape
    dff = w_gate.shape[1]
    final = final_g is not None
    tile = pl.BlockSpec((ROW_TILE, d), lambda i: (i, 0))
    row = pl.BlockSpec((1, d), lambda i: (0, 0))
    resident = lambda r, c: pl.BlockSpec((r, c), lambda i: (0, 0), pipeline_mode=pl.Buffered(1))
    in_specs = [tile, row, resident(d, dff), resident(d, dff), resident(dff, d)]
    args = [x2, g.reshape(1, d), w_gate, w_up, w_down]
    if final:
        in_specs.append(row)
        args.append(final_g.reshape(1, d))
    return pl.pallas_call(
        functools.partial(_ffn_kernel, final=final),
        grid=(m // ROW_TILE,),
        in_specs=in_specs,
        out_specs=tile,
        out_shape=jax.ShapeDtypeStruct((m, d), F32),
        compiler_params=_params("parallel"),
        name="ffn",
    )(*args)


def _rope_tables(s):
    half = ROPE_DIM // 2
    inv_freq = ROPE_THETA ** (-jnp.arange(0, ROPE_DIM, 2, dtype=F32) / ROPE_DIM)
    ang = jnp.arange(s, dtype=F32)[:, None] * inv_freq[None, :]
    cos, sin = jnp.cos(ang), jnp.sin(ang)
    pad = HEAD_DIM - ROPE_DIM
    cos_h = jnp.concatenate([cos, cos, jnp.ones((s, pad), F32)], axis=1)
    sin_h = jnp.concatenate([-sin, sin, jnp.zeros((s, pad), F32)], axis=1)
    return jnp.tile(cos_h, (1, N_HEADS)), jnp.tile(sin_h, (1, N_HEADS))


def kernel(x, norm1_g, w_in, rwkv_mu, rwkv_w0, rwkv_w2, rwkv_a0, rwkv_a2, rwkv_g2, rwkv_k_k, rwkv_k_a, rwkv_r_k, rwkv_gn_w, rwkv_gn_b, rwkv_v0, rwkv_v1, rwkv_v2, hgrn_lb_logits, hgrn_gn_g, w_branch, w_out, norm2_g, w_ffn_gate, w_ffn_up, w_ffn_down, final_g):
    b, s, d = x.shape
    depth = w_in.shape[0]
    c = MIX_WIDTH
    m = b * s
    assert d == D_MODEL and s % SEQ_TILE == 0 and m % ROW_TILE == 0 and s % REC_TILE == 0
    assert LANES % (b * N_HEADS) == 0 and HEAD_DIM % (LANES // (b * N_HEADS)) == 0

    cos, sin = _rope_tables(s)
    lb_w = jax.nn.softmax(hgrn_lb_logits.astype(F32), axis=0)
    lower_bounds = jnp.cumsum(lb_w, axis=0) - lb_w[0]
    head_id = jnp.arange(c) // HEAD_DIM
    e = (head_id[:, None] == head_id[None, :]).astype(BF16)
    row = lambda t: t.reshape(1, -1)
    zpad = lambda t, top, bottom: jnp.pad(t, ((top, bottom), (0, 0)))

    x2 = x.reshape(m, d)
    v_first = None
    for layer in range(depth):
        w_l = w_in[layer].astype(BF16)
        g1 = norm1_g[layer]
        rp, hc, k_sb, qt_sb, vt_sb, k_m, qt_m, vt_m, k_mean = _in_proj(
            x2.reshape(b, s, d), g1, w_l, cos, sin)

        y_sb = _sb_attention(qt_sb, k_sb, vt_sb).reshape(m, c)

        prm = {
            "mu": row(rwkv_mu[layer]), "w0": row(rwkv_w0[layer]),
            "w2": zpad(rwkv_w2[layer], 0, LANES - RWKV_DECAY_LORA),
            "a0": row(rwkv_a0[layer]),
            "a2": zpad(rwkv_a2[layer], RWKV_DECAY_LORA, LANES - RWKV_DECAY_LORA - RWKV_AAA_LORA),
            "g2": rwkv_g2[layer], "k_k": row(rwkv_k_k[layer]), "k_a": row(rwkv_k_a[layer]),
            "r_k": row(rwkv_r_k[layer]),
        }
        if layer > 0:
            prm["v0"] = row(rwkv_v0[layer - 1])
            prm["v1"] = jnp.pad(rwkv_v1[layer - 1], ((0, 0), (0, LANES - RWKV_MV_LORA)))
            prm["v2"] = zpad(rwkv_v2[layer - 1], 0, LANES - RWKV_MV_LORA)
        r_, w_, k_, v_, an_, bb_, g_, bonus_ = _rwkv_prep(rp, prm, e, v_first)
        if layer == 0:
            v_first = v_
        y_rec = _rwkv_recurrence(r_, w_, k_, v_, an_, bb_)
        rwkv_out = (y_rec.reshape(m, c), g_.reshape(m, c), bonus_.reshape(m, c),
                    row(rwkv_gn_w[layer]), row(rwkv_gn_b[layer]))

        n_blk = k_mean.shape[1]
        k_mean = jnp.pad(k_mean.reshape(b, n_blk, c), ((0, 0), (0, -n_blk % 8), (0, 0)))
        y_moba = _moba_attention(qt_m, k_m, vt_m, k_mean).reshape(m, c)

        y_hgrn = _hgrn(hc, lower_bounds[layer], hgrn_gn_g[layer], e).reshape(m, c)

        x2 = _merge(x2, g1, w_l[:, GATE_OFF:], y_sb, rwkv_out, y_moba, y_hgrn, e,
                    w_branch[layer].astype(BF16), w_out[layer].astype(BF16))
        x2 = _ffn(x2, norm2_g[layer], w_ffn_gate[layer].astype(BF16), w_ffn_up[layer].astype(BF16),
                  w_ffn_down[layer].astype(BF16), final_g if layer == depth - 1 else None)
    return x2.reshape(b, s, d)
```
